```python
import jax
import jax.numpy as jnp
from jax import lax
import numpy as np

D_MODEL = 1024
BATCH = 16
SEQ = 2048
DEPTH = 2
DEC_BATCH = 32
DEC_SEQ = 4
PAST_LEN = 16384
PAGE_SIZE = 128

N_EVEN = (DEPTH + 1) // 2
N_ODD = DEPTH // 2
RMS_EPS = 1e-6
NEG_INF = -1e30
FORCE_SCORE = 1e30

POOL_WIDTH = D_MODEL // 2
POOL_WINDOWS = (2, 4, 8, 16)
N_POOL_GROUPS = len(POOL_WINDOWS)
POOL_GROUP = POOL_WIDTH // N_POOL_GROUPS
POOL_STATE = max(POOL_WINDOWS) - 1

HEAD_DIM = 64
NSA_HEADS = (D_MODEL - POOL_WIDTH) // HEAD_DIM
NSA_KV_HEADS = 2
NSA_WIDTH = NSA_HEADS * HEAD_DIM
KV_COLS = NSA_KV_HEADS * HEAD_DIM
CMP_BLOCK = 32
CMP_STRIDE = 16
CMP_RATIO = CMP_BLOCK // CMP_STRIDE
SEL_BLOCK = 64
SEL_PER_STRIDE = SEL_BLOCK // CMP_STRIDE
SEL_TOPK = 16
WINDOW = 512
Q_BLOCK = 128
SEL_Q_CHUNK = 32
EVEN_IN = POOL_WIDTH + NSA_WIDTH + 6 * KV_COLS + 3 * NSA_HEADS

LRU_WIDTH = D_MODEL
LRU_BLOCKS = 4
LRU_BLOCK = LRU_WIDTH // LRU_BLOCKS
CONV_WIDTH = 4
LRU_C = 8.0

N_MEM = 256
XATTN_HEADS = 4
XATTN_HEAD_DIM = D_MODEL // XATTN_HEADS

D_FF = -(-(8 * D_MODEL) // (3 * 256)) * 256

kernel_name = 'hybrid_pool_nsa_rglru_decoder_step'


def rmsnorm(x, g):
    xf = x.astype(jnp.float32)
    y = xf * lax.rsqrt(jnp.mean(xf * xf, axis=-1, keepdims=True) + RMS_EPS)
    return (y * g.astype(jnp.float32)).astype(x.dtype)


def cat_rows(past, new):
    return jnp.concatenate([past.astype(new.dtype), new], axis=1)


def gqa_attend(q, k, v, mask):
    b, tq, h, dh = q.shape
    g = k.shape[2]
    qg = q.reshape(b, tq, g, h // g, dh)
    s = jnp.einsum('btgod,bsgd->bgots', qg, k).astype(jnp.float32) * (dh ** -0.5)
    s = jnp.where(mask, s, NEG_INF)
    p = jnp.where(mask, jax.nn.softmax(s, axis=-1), 0.0)
    o = jnp.einsum('bgots,bsgd->btgod', p.astype(v.dtype), v)
    return o.reshape(b, tq, h, dh), p


def pool_mix(u_ext, n_new, start_pos, pool_w, pool_scale):
    b, l, c = u_ext.shape
    p = l - n_new
    off = POOL_STATE + 1
    uf = u_ext.astype(jnp.float32)
    cs = jnp.cumsum(jnp.pad(uf, ((0, 0), (off, 0), (0, 0))), axis=1)
    pos = start_pos + jnp.arange(n_new)
    means = []
    for gi, w in enumerate(POOL_WINDOWS):
        ch = slice(gi * POOL_GROUP, (gi + 1) * POOL_GROUP)
        win_sum = cs[:, off + p:off + p + n_new, ch] - cs[:, off + p - w:off + p + n_new - w, ch]
        cnt = jnp.minimum(pos + 1, w).astype(jnp.float32)[None, :, None]
        means.append(win_sum / cnt)
    d = (jnp.concatenate(means, axis=-1) - uf[:, p:]).reshape(b, n_new, N_POOL_GROUPS, POOL_GROUP)
    y = jnp.einsum('btgc,gcd->btgd', d, pool_w.astype(jnp.float32)).reshape(b, n_new, c)
    return (y * pool_scale.astype(jnp.float32)).astype(u_ext.dtype)


def split_even(z):
    b, t, _ = z.shape
    sizes = [POOL_WIDTH, NSA_WIDTH] + [KV_COLS] * 6 + [3 * NSA_HEADS]
    offsets = [int(o) for o in np.cumsum(sizes)[:-1]]
    parts = jnp.split(z, offsets, axis=-1)
    u = parts[0]
    q = parts[1].reshape(b, t, NSA_HEADS, HEAD_DIM)
    kvs = tuple(pp.reshape(b, t, NSA_KV_HEADS, HEAD_DIM) for pp in parts[2:8])
    gate = jax.nn.sigmoid(parts[8].astype(jnp.float32)).reshape(b, t, NSA_HEADS, 3)
    return u, q, kvs, gate


def compress_kv(k, v, w_ck, w_cv):
    b, l, g, dh = k.shape
    n_sub = -(-l // CMP_STRIDE)
    pad = n_sub * CMP_STRIDE - l
    n_cmp = n_sub - CMP_RATIO + 1

    def block_means(x):
        xs = jnp.pad(x.astype(jnp.float32), ((0, 0), (0, pad), (0, 0), (0, 0)))
        xs = xs.reshape(b, n_sub, CMP_STRIDE, g, dh).sum(axis=2)
        acc = xs[:, :n_cmp]
        for r in range(1, CMP_RATIO):
            acc = acc + xs[:, r:r + n_cmp]
        return acc / CMP_BLOCK

    kc = jnp.einsum('bngd,gde->bnge', block_means(k), w_ck.astype(jnp.float32))
    vc = jnp.einsum('bngd,gde->bnge', block_means(v), w_cv.astype(jnp.float32))
    return kc.astype(k.dtype), vc.astype(v.dtype)


def sel_importance(p, n_sel):
    lead = CMP_RATIO - 1
    span = SEL_PER_STRIDE + CMP_RATIO - 1
    total = SEL_PER_STRIDE * (n_sel - 1) + span
    right = max(total - lead - p.shape[-1], 0)
    pp = jnp.pad(p, [(0, 0)] * (p.ndim - 1) + [(lead, right)])
    stop = SEL_PER_STRIDE * (n_sel - 1) + 1
    out = pp[..., 0:stop:SEL_PER_STRIDE]
    for k in range(1, span):
        out = out + pp[..., k:k + stop:SEL_PER_STRIDE]
    return out


def select_blocks(imp, qpos, n_sel):
    j = jnp.arange(n_sel)[None, :]
    cur = (qpos // SEL_BLOCK)[:, None]
    valid = j * SEL_BLOCK <= qpos[:, None]
    forced = (j == 0) | (j == cur) | (j == cur - 1)
    score = jnp.where(forced, FORCE_SCORE, jnp.where(valid, imp, NEG_INF))
    _, idx = lax.top_k(score, min(SEL_TOPK, n_sel))
    return idx


def gather_blocks(blocks, idx):
    return jax.vmap(jax.vmap(lambda blk, ix: blk[ix]))(blocks, idx)


def selected_attend(q, qpos, kb, vb, idx):
    b, tq, h, dh = q.shape
    g = kb.shape[1]
    n_top = idx.shape[-1]
    kg = gather_blocks(kb, idx)
    vg = gather_blocks(vb, idx)
    qg = q.reshape(b, tq, g, h // g, dh)
    s = jnp.einsum('btgod,bgtksd->bgotks', qg, kg).astype(jnp.float32) * (dh ** -0.5)
    kpos = idx[..., None] * SEL_BLOCK + jnp.arange(SEL_BLOCK)
    mask = (kpos <= qpos[None, None, :, None, None])[:, :, None]
    s = jnp.where(mask, s, NEG_INF).reshape(b, g, h // g, tq, n_top * SEL_BLOCK)
    p = jax.nn.softmax(s, axis=-1).reshape(b, g, h // g, tq, n_top, SEL_BLOCK)
    p = jnp.where(mask, p, 0.0)
    o = jnp.einsum('bgotks,bgtksd->btgod', p.astype(vg.dtype), vg)
    return o.reshape(b, tq, h, dh)


def selected_attend_chunked(q, qpos, kb, vb, idx):
    b, t, h, dh = q.shape
    g, n_top = idx.shape[1], idx.shape[-1]
    n = t // SEL_Q_CHUNK
    qc = q.reshape(b, n, SEL_Q_CHUNK, h, dh).transpose(1, 0, 2, 3, 4)
    ic = idx.reshape(b, g, n, SEL_Q_CHUNK, n_top).transpose(2, 0, 1, 3, 4)
    pc = qpos.reshape(n, SEL_Q_CHUNK)
    out = lax.map(lambda a: selected_attend(a[0], a[2], kb, vb, a[1]), (qc, ic, pc))
    return out.transpose(1, 0, 2, 3, 4).reshape(b, t, h, dh)


def nsa_compressed_selected(q, qpos, kc, vc, ks, vs, w_ck, w_cv, chunked):
    b, l, g, dh = ks.shape
    kcmp, vcmp = compress_kv(kc, vc, w_ck, w_cv)
    cmp_end = jnp.arange(kcmp.shape[1]) * CMP_STRIDE + CMP_BLOCK - 1
    o_cmp, p_cmp = gqa_attend(q, kcmp, vcmp, cmp_end[None, :] <= qpos[:, None])
    n_sel = -(-l // SEL_BLOCK)
    idx = select_blocks(sel_importance(p_cmp.sum(axis=2), n_sel), qpos, n_sel)
    pad = ((0, 0), (0, n_sel * SEL_BLOCK - l), (0, 0), (0, 0))
    kb = jnp.pad(ks, pad).reshape(b, n_sel, SEL_BLOCK, g, dh).transpose(0, 3, 1, 2, 4)
    vb = jnp.pad(vs, pad).reshape(b, n_sel, SEL_BLOCK, g, dh).transpose(0, 3, 1, 2, 4)
    if chunked:
        o_slc = selected_attend_chunked(q, qpos, kb, vb, idx)
    else:
        o_slc = selected_attend(q, qpos, kb, vb, idx)
    return o_cmp, o_slc


def window_mask(qpos, kpos):
    d = qpos[:, None] - kpos[None, :]
    return (d >= 0) & (d < WINDOW) & (kpos[None, :] >= 0)


def window_attend_banded(q, kw, vw):
    b, t, h, dh = q.shape
    pad = ((0, 0), (WINDOW, 0), (0, 0), (0, 0))
    kp, vp = jnp.pad(kw, pad), jnp.pad(vw, pad)

    def one_block(i):
        q0 = i * Q_BLOCK
        qb = lax.dynamic_slice_in_dim(q, q0, Q_BLOCK, axis=1)
        kb = lax.dynamic_slice_in_dim(kp, q0, WINDOW + Q_BLOCK, axis=1)
        vb = lax.dynamic_slice_in_dim(vp, q0, WINDOW + Q_BLOCK, axis=1)
        qpos = q0 + jnp.arange(Q_BLOCK)
        kpos = q0 - WINDOW + jnp.arange(WINDOW + Q_BLOCK)
        return gqa_attend(qb, kb, vb, window_mask(qpos, kpos))[0]

    out = lax.map(one_block, jnp.arange(t // Q_BLOCK))
    return out.transpose(1, 0, 2, 3, 4).reshape(b, t, h, dh)


def even_mixer(h, past, start_pos, w_in, pool_w, pool_scale, w_ck, w_cv, w_out):
    b, t, _ = h.shape
    u, q, (kc, vc, ks, vs, kw, vw), gate = split_even(h @ w_in)
    qpos = start_pos + jnp.arange(t)
    if past is None:
        u_ext, kc_f, vc_f, ks_f, vs_f = u, kc, vc, ks, vs
    else:
        u_ext = cat_rows(past[0], u)
        kc_f, vc_f = cat_rows(past[1], kc), cat_rows(past[2], vc)
        ks_f, vs_f = cat_rows(past[3], ks), cat_rows(past[4], vs)
    a_out = pool_mix(u_ext, t, start_pos, pool_w, pool_scale)
    o_cmp, o_slc = nsa_compressed_selected(q, qpos, kc_f, vc_f, ks_f, vs_f, w_ck, w_cv, past is None)
    if past is None:
        n_keep = min(WINDOW, t)
        kw_ext, vw_ext = kw, vw
        o_win = window_attend_banded(q, kw, vw)
    else:
        n_keep = past[5].shape[1]
        kw_ext, vw_ext = cat_rows(past[5], kw), cat_rows(past[6], vw)
        kpos = start_pos - n_keep + jnp.arange(n_keep + t)
        o_win = gqa_attend(q, kw_ext, vw_ext, window_mask(qpos, kpos))[0]
    o_nsa = gate[..., 0:1] * o_cmp + gate[..., 1:2] * o_slc + gate[..., 2:3] * o_win
    mixed = jnp.concatenate([a_out, o_nsa.astype(h.dtype).reshape(b, t, NSA_WIDTH)], axis=-1) @ w_out
    new_state = (u_ext[:, -POOL_STATE:], kc, vc, ks, vs, kw_ext[:, -n_keep:], vw_ext[:, -n_keep:])
    return mixed, new_state


def odd_mixer(h, conv_past, h_past, start_pos, w_in, conv_w, conv_b, wa, ba, wx, bx, lam, w_out):
    b, t, _ = h.shape
    z = h @ w_in
    xb, yb = z[..., :LRU_WIDTH], z[..., LRU_WIDTH:]
    xe = cat_rows(conv_past, xb)
    xc = xe[:, 0:t] * conv_w[0]
    for k in range(1, CONV_WIDTH):
        xc = xc + xe[:, k:k + t] * conv_w[k]
    xf = (xc + conv_b).astype(jnp.float32)
    blocks = xf.reshape(b, t, LRU_BLOCKS, LRU_BLOCK)
    r = jax.nn.sigmoid(jnp.einsum('btnc,ncd->btnd', blocks, wa.astype(jnp.float32)).reshape(b, t, LRU_WIDTH) + ba.astype(jnp.float32))
    i = jax.nn.sigmoid(jnp.einsum('btnc,ncd->btnd', blocks, wx.astype(jnp.float32)).reshape(b, t, LRU_WIDTH) + bx.astype(jnp.float32))
    log_a = -LRU_C * r * jax.nn.softplus(-lam.astype(jnp.float32))
    a = jnp.exp(log_a)
    pos = start_pos + jnp.arange(t)
    mult = jnp.where((pos == 0)[None, :, None], 1.0, jnp.sqrt(-jnp.expm1(2.0 * log_a)))
    drive = mult * i * xf

    def step(carry, inp):
        a_t, d_t = inp
        carry = a_t * carry + d_t
        return carry, carry

    h_last, hs = lax.scan(step, h_past.astype(jnp.float32), (jnp.swapaxes(a, 0, 1), jnp.swapaxes(drive, 0, 1)))
    y = (jnp.swapaxes(hs, 0, 1) * jax.nn.gelu(yb.astype(jnp.float32))).astype(h.dtype) @ w_out
    return y, xe[:, -(CONV_WIDTH - 1):], h_last.astype(h.dtype)


def memory_kv(mem, g, wk, wv):
    b, m, _ = mem.shape
    mn = rmsnorm(mem, g)
    k = (mn @ wk).reshape(b, m, XATTN_HEADS, XATTN_HEAD_DIM)
    v = (mn @ wv).reshape(b, m, XATTN_HEADS, XATTN_HEAD_DIM)
    return k, v


def cross_attend(h, mk, mv, wq, wo):
    b, t, _ = h.shape
    q = (h @ wq).reshape(b, t, XATTN_HEADS, XATTN_HEAD_DIM)
    o, _ = gqa_attend(q, mk, mv, True)
    return o.reshape(b, t, D_MODEL) @ wo


def swiglu(h, wg, wu, wd):
    return (jax.nn.silu(h @ wg) * (h @ wu)) @ wd


def gather_pages(pool, page_table, layer):
    rows = pool[page_table, layer]
    b, n, p, g, dh = rows.shape
    return rows.reshape(b, n * p, g, dh)


def setup_inputs(seed: int = 0) -> dict:
    key = jax.random.key(seed)
    keys = iter(jax.random.split(key, 48))

    def nrm(shape, scale=1.0):
        return jax.random.normal(next(keys), shape, jnp.float32) * scale

    n_pages = PAST_LEN // PAGE_SIZE
    n_phys = (5 * DEC_BATCH * n_pages + 3) // 4
    n_win = min(WINDOW, PAST_LEN)
    perm = jax.random.permutation(next(keys), n_phys)
    page_table = perm[: DEC_BATCH * n_pages].reshape(DEC_BATCH, n_pages).astype(jnp.int32)
    paged = (n_phys, N_EVEN, PAGE_SIZE, NSA_KV_HEADS, HEAD_DIM)
    win = (N_EVEN, DEC_BATCH, n_win, NSA_KV_HEADS, HEAD_DIM)
    mem_shape = (DEPTH, DEC_BATCH, N_MEM, XATTN_HEADS, XATTN_HEAD_DIM)
    a0 = jax.random.uniform(next(keys), (N_ODD, LRU_WIDTH), jnp.float32, 0.9, 0.999) ** (1.0 / LRU_C)
    return {
        'x_prompt': nrm((BATCH, SEQ, D_MODEL)),
        'mem_prompt': nrm((BATCH, N_MEM, D_MODEL)),
        'x_sample': nrm((DEC_BATCH, DEC_SEQ, D_MODEL)),
        'state_pool': nrm((N_EVEN, DEC_BATCH, POOL_STATE, POOL_WIDTH)),
        'cache_cmp_k': nrm(paged),
        'cache_cmp_v': nrm(paged),
        'cache_slc_k': nrm(paged),
        'cache_slc_v': nrm(paged),
        'cache_win_k': nrm(win),
        'cache_win_v': nrm(win),
        'state_lru_h': nrm((N_ODD, DEC_BATCH, LRU_WIDTH), 0.5),
        'state_lru_conv': nrm((N_ODD, DEC_BATCH, CONV_WIDTH - 1, LRU_WIDTH)),
        'cache_mem_k': nrm(mem_shape),
        'cache_mem_v': nrm(mem_shape),
        'page_table': page_table,
        'norm_gain': 1.0 + nrm((DEPTH, 6, D_MODEL), 0.02),
        'mem_norm_gain': 1.0 + nrm((DEPTH, D_MODEL), 0.02),
        'w_in_even': nrm((N_EVEN, D_MODEL, EVEN_IN), D_MODEL ** -0.5),
        'pool_w': nrm((N_EVEN, N_POOL_GROUPS, POOL_GROUP, POOL_GROUP), POOL_GROUP ** -0.5),
        'pool_scale': 1.0 + nrm((N_EVEN, POOL_WIDTH), 0.1),
        'w_cmp_k': nrm((N_EVEN, NSA_KV_HEADS, HEAD_DIM, HEAD_DIM), HEAD_DIM ** -0.5),
        'w_cmp_v': nrm((N_EVEN, NSA_KV_HEADS, HEAD_DIM, HEAD_DIM), HEAD_DIM ** -0.5),
        'w_out_even': nrm((N_EVEN, POOL_WIDTH + NSA_WIDTH, D_MODEL), (POOL_WIDTH + NSA_WIDTH) ** -0.5),
        'w_in_odd': nrm((N_ODD, D_MODEL, 2 * LRU_WIDTH), D_MODEL ** -0.5),
        'conv_w': nrm((N_ODD, CONV_WIDTH, LRU_WIDTH), CONV_WIDTH ** -0.5),
        'conv_b': nrm((N_ODD, LRU_WIDTH), 0.01),
        'lru_wa': nrm((N_ODD, LRU_BLOCKS, LRU_BLOCK, LRU_BLOCK), LRU_BLOCK ** -0.5),
        'lru_ba': nrm((N_ODD, LRU_WIDTH), 0.01),
        'lru_wx': nrm((N_ODD, LRU_BLOCKS, LRU_BLOCK, LRU_BLOCK), LRU_BLOCK ** -0.5),
        'lru_bx': nrm((N_ODD, LRU_WIDTH), 0.01),
        'lru_lambda': jnp.log(a0) - jnp.log1p(-a0),
        'w_out_odd': nrm((N_ODD, LRU_WIDTH, D_MODEL), LRU_WIDTH ** -0.5),
        'w_xq': nrm((DEPTH, D_MODEL, D_MODEL), D_MODEL ** -0.5),
        'w_xk': nrm((DEPTH, D_MODEL, D_MODEL), D_MODEL ** -0.5),
        'w_xv': nrm((DEPTH, D_MODEL, D_MODEL), D_MODEL ** -0.5),
        'w_xo': nrm((DEPTH, D_MODEL, D_MODEL), D_MODEL ** -0.5),
        'w_ffn_gate': nrm((DEPTH, D_MODEL, D_FF), D_MODEL ** -0.5),
        'w_ffn_up': nrm((DEPTH, D_MODEL, D_FF), D_MODEL ** -0.5),
        'w_ffn_down': nrm((DEPTH, D_FF, D_MODEL), D_FF ** -0.5),
    }


def reference(x_prompt, mem_prompt, x_sample, state_pool, cache_cmp_k, cache_cmp_v, cache_slc_k,
              cache_slc_v, cache_win_k, cache_win_v, state_lru_h, state_lru_conv, cache_mem_k,
              cache_mem_v, page_table, norm_gain, mem_norm_gain, w_in_even, pool_w, pool_scale,
              w_cmp_k, w_cmp_v, w_out_even, w_in_odd, conv_w, conv_b, lru_wa, lru_ba, lru_wx,
              lru_bx, lru_lambda, w_out_odd, w_xq, w_xk, w_xv, w_xo, w_ffn_gate, w_ffn_up,
              w_ffn_down):
    past_len = page_table.shape[1] * PAGE_SIZE
    bp = x_prompt.shape[0]
    xp, xs = x_prompt, x_sample
    ev_p, ev_s = [], []
    lru_p, lru_s = [], []
    mem_k_p, mem_v_p = [], []
    for li in range(DEPTH):
        gn = norm_gain[li]
        hp = rmsnorm(xp, gn[0])
        hs = rmsnorm(xs, gn[0])
        if li % 2 == 0:
            e = li // 2
            wts = (w_in_even[e], pool_w[e], pool_scale[e], w_cmp_k[e], w_cmp_v[e], w_out_even[e])
            mix_p, st_p = even_mixer(hp, None, 0, *wts)
            past = (state_pool[e],
                    gather_pages(cache_cmp_k, page_table, e), gather_pages(cache_cmp_v, page_table, e),
                    gather_pages(cache_slc_k, page_table, e), gather_pages(cache_slc_v, page_table, e),
                    cache_win_k[e], cache_win_v[e])
            mix_s, st_s = even_mixer(hs, past, past_len, *wts)
            ev_p.append(st_p)
            ev_s.append(st_s)
        else:
            o = li // 2
            wts = (w_in_odd[o], conv_w[o], conv_b[o], lru_wa[o], lru_ba[o], lru_wx[o], lru_bx[o],
                   lru_lambda[o], w_out_odd[o])
            conv0 = jnp.zeros((bp, CONV_WIDTH - 1, LRU_WIDTH), hp.dtype)
            h0 = jnp.zeros((bp, LRU_WIDTH), jnp.float32)
            mix_p, conv_p, hl_p = odd_mixer(hp, conv0, h0, 0, *wts)
            mix_s, conv_s, hl_s = odd_mixer(hs, state_lru_conv[o], state_lru_h[o], past_len, *wts)
            lru_p.append((conv_p, hl_p))
            lru_s.append((conv_s, hl_s))
        xp = xp + rmsnorm(mix_p, gn[1])
        xs = xs + rmsnorm(mix_s, gn[1])
        mk, mv = memory_kv(mem_prompt, mem_norm_gain[li], w_xk[li], w_xv[li])
        mem_k_p.append(mk)
        mem_v_p.append(mv)
        xp = xp + rmsnorm(cross_attend(rmsnorm(xp, gn[2]), mk, mv, w_xq[li], w_xo[li]), gn[3])
        xs = xs + rmsnorm(cross_attend(rmsnorm(xs, gn[2]), cache_mem_k[li], cache_mem_v[li], w_xq[li], w_xo[li]), gn[3])
        xp = xp + rmsnorm(swiglu(rmsnorm(xp, gn[4]), w_ffn_gate[li], w_ffn_up[li], w_ffn_down[li]), gn[5])
        xs = xs + rmsnorm(swiglu(rmsnorm(xs, gn[4]), w_ffn_gate[li], w_ffn_up[li], w_ffn_down[li]), gn[5])
    return (xp, xs,
            jnp.stack([s[0] for s in ev_p]), jnp.stack([s[0] for s in ev_s]),
            jnp.stack([s[1] for s in ev_p], axis=1), jnp.stack([s[1] for s in ev_s], axis=1),
            jnp.stack([s[2] for s in ev_p], axis=1), jnp.stack([s[2] for s in ev_s], axis=1),
            jnp.stack([s[3] for s in ev_p], axis=1), jnp.stack([s[3] for s in ev_s], axis=1),
            jnp.stack([s[4] for s in ev_p], axis=1), jnp.stack([s[4] for s in ev_s], axis=1),
            jnp.stack([s[5] for s in ev_p]), jnp.stack([s[5] for s in ev_s]),
            jnp.stack([s[6] for s in ev_p]), jnp.stack([s[6] for s in ev_s]),
            jnp.stack([s[1] for s in lru_p]), jnp.stack([s[1] for s in lru_s]),
            jnp.stack([s[0] for s in lru_p]), jnp.stack([s[0] for s in lru_s]),
            jnp.stack(mem_k_p), jnp.stack(mem_v_p))
```

```python
import functools

import numpy as np
import jax
import jax.numpy as jnp
from jax import lax
from jax.experimental import pallas as pl
from jax.experimental.pallas import tpu as pltpu

F32 = jnp.float32
BF16 = jnp.bfloat16

D_MODEL = 1024
RMS_EPS = 1e-6
NEG_INF = -1e30
FORCE_SCORE = 1e30
REMOVED_SCORE = -3e38

POOL_WIDTH = 512
POOL_WINDOWS = (2, 4, 8, 16)
POOL_GROUP = 128
POOL_STATE = 15
POOL_HALO = 16

HEAD_DIM = 64
NSA_HEADS = 8
NSA_KV_HEADS = 2
HEADS_PER_GROUP = NSA_HEADS // NSA_KV_HEADS
NSA_WIDTH = 512
KV_COLS = 128
CMP_BLOCK = 32
CMP_STRIDE = 16
SEL_BLOCK = 64
SEL_TOPK = 16
WINDOW = 512
PAGE_SIZE = 128
EVEN_IN = POOL_WIDTH + NSA_WIDTH + 6 * KV_COLS + 3 * NSA_HEADS
EVEN_IN_PAD = 1920

LRU_WIDTH = 1024
LRU_BLOCKS = 4
LRU_BLOCK = 256
CONV_WIDTH = 4
LRU_C = 8.0

N_MEM = 256
XATTN_HEADS = 4
XATTN_HEAD_DIM = 256
D_FF = 2816
FF_CHUNK = 256

SAMPLE_T_PAD = 8
PAGES_PER_STEP = 16
SEL_PER_STEP = PAGES_PER_STEP * PAGE_SIZE // SEL_BLOCK
SUB_PER_STEP = PAGES_PER_STEP * PAGE_SIZE // CMP_STRIDE
LANE = 128

VMEM_LIMIT_BYTES = 56 * 1024 * 1024


def _cparams(*sem):
    return pltpu.CompilerParams(dimension_semantics=sem, vmem_limit_bytes=VMEM_LIMIT_BYTES)


def _rms(x, g):
    return x * lax.rsqrt(jnp.mean(x * x, axis=-1, keepdims=True) + RMS_EPS) * g


def _dot(a, b):
    return jnp.dot(a, b, preferred_element_type=F32)


def _dot_nt(a, b):
    return lax.dot_general(a, b, (((1,), (1,)), ((), ())), preferred_element_type=F32)


def _split3(x):
    p1 = x.astype(BF16)
    r1 = x - p1.astype(F32)
    p2 = r1.astype(BF16)
    p3 = (r1 - p2.astype(F32)).astype(BF16)
    return p1, p2, p3


def _full(shape):
    n = len(shape)
    return pl.BlockSpec(shape, lambda *_: (0,) * n)


def _norm_matmul_kernel(x_ref, g_ref, w_ref, *o_refs, groups):
    h = _rms(x_ref[...], g_ref[...]).astype(BF16)
    k = 0
    for start, width, outs in groups:
        z = _dot(h, w_ref[:, start:start + width])
        for off, w, post in outs:
            v = z[:, off:off + w]
            if post is not None:
                v = post(v)
            o_refs[k][...] = v.astype(o_refs[k].dtype)
            k += 1


def norm_matmul(x, g, w, groups, out_dtypes, tm):
    n, d = x.shape
    widths = [w_ for _, _, outs in groups for _, w_, _ in outs]
    return pl.pallas_call(
        functools.partial(_norm_matmul_kernel, groups=groups),
        grid=(n // tm,),
        in_specs=[pl.BlockSpec((tm, d), lambda i: (i, 0)), _full(g.shape), _full(w.shape)],
        out_specs=[pl.BlockSpec((tm, wd), lambda i: (i, 0)) for wd in widths],
        out_shape=[jax.ShapeDtypeStruct((n, wd), dt) for wd, dt in zip(widths, out_dtypes)],
        compiler_params=_cparams("parallel"),
    )(x, g, w)


def _even_in_prompt_kernel(x_ref, g_ref, wuq_ref, wgate_ref, wkvt_ref, u_ref, q_ref, gate_ref, *kvt_refs):
    h = _rms(x_ref[...], g_ref[...]).astype(BF16)
    u_ref[...] = _dot(h, wuq_ref[:, 0:POOL_WIDTH])
    q_ref[...] = (_dot(h, wuq_ref[:, POOL_WIDTH:]) * (HEAD_DIM ** -0.5)).astype(q_ref.dtype)
    gate_ref[...] = jax.nn.sigmoid(_dot(h, wgate_ref[...]))
    kv_t = _dot_nt(wkvt_ref[...], h)
    for k, ref in enumerate(kvt_refs):
        ref[0] = kv_t[k * KV_COLS:(k + 1) * KV_COLS, :]


def even_in_prompt(x, g, w_uq, w_gate, w_kv_t, b, t, tm):
    n, d = x.shape
    per_seq = t // tm
    row_tile = lambda w: pl.BlockSpec((tm, w), lambda i: (i, 0))
    kv_spec = pl.BlockSpec((1, KV_COLS, tm), lambda i: (i // per_seq, 0, i % per_seq))
    n_kv = w_kv_t.shape[0] // KV_COLS
    return pl.pallas_call(
        _even_in_prompt_kernel,
        grid=(n // tm,),
        in_specs=[row_tile(d), _full(g.shape), _full(w_uq.shape), _full(w_gate.shape), _full(w_kv_t.shape)],
        out_specs=[row_tile(POOL_WIDTH), row_tile(NSA_WIDTH), row_tile(LANE)] + [kv_spec] * n_kv,
        out_shape=([jax.ShapeDtypeStruct((n, POOL_WIDTH), F32), jax.ShapeDtypeStruct((n, NSA_WIDTH), BF16),
                    jax.ShapeDtypeStruct((n, LANE), F32)]
                   + [jax.ShapeDtypeStruct((b, KV_COLS, t), F32)] * n_kv),
        compiler_params=_cparams("parallel"),
    )(x, g, w_uq, w_gate, w_kv_t)


def _proj_norm_res_kernel(*refs, n_in):
    a_refs, w_refs = refs[:n_in], refs[n_in:2 * n_in]
    g_ref, x_ref, o_ref = refs[2 * n_in:]
    acc = _dot(a_refs[0][...], w_refs[0][...])
    for a_ref, w_ref in zip(a_refs[1:], w_refs[1:]):
        acc = acc + _dot(a_ref[...], w_ref[...])
    o_ref[...] = x_ref[...] + _rms(acc, g_ref[...])


def proj_norm_res(a_list, w_list, g, x, tm):
    n, d = x.shape
    n_in = len(a_list)
    return pl.pallas_call(
        functools.partial(_proj_norm_res_kernel, n_in=n_in),
        grid=(n // tm,),
        in_specs=([pl.BlockSpec((tm, a.shape[1]), lambda i: (i, 0)) for a in a_list]
                  + [_full(w.shape) for w in w_list]
                  + [_full(g.shape), pl.BlockSpec((tm, d), lambda i: (i, 0))]),
        out_specs=pl.BlockSpec((tm, d), lambda i: (i, 0)),
        out_shape=jax.ShapeDtypeStruct((n, d), F32),
        compiler_params=_cparams("parallel"),
    )(*a_list, *w_list, g, x)


def _ffn_kernel(x_ref, g_in_ref, g_out_ref, wg_ref, wu_ref, wd_ref, o_ref):
    x = x_ref[...]
    h = _rms(x, g_in_ref[...]).astype(BF16)
    acc = None
    for c in range(D_FF // FF_CHUNK):
        cols = slice(c * FF_CHUNK, (c + 1) * FF_CHUNK)
        gate = _dot(h, wg_ref[:, cols])
        up = _dot(h, wu_ref[:, cols])
        act = (jax.nn.silu(gate) * up).astype(BF16)
        part = _dot(act, wd_ref[cols, :])
        acc = part if acc is None else acc + part
    o_ref[...] = x + _rms(acc, g_out_ref[...])


def ffn(x, g_in, g_out, wg, wu, wd, tm):
    n, d = x.shape
    resident = functools.partial(pl.BlockSpec, pipeline_mode=pl.Buffered(1))
    return pl.pallas_call(
        _ffn_kernel,
        grid=(n // tm,),
        in_specs=[pl.BlockSpec((tm, d), lambda i: (i, 0)), _full(g_in.shape), _full(g_out.shape),
                  resident(wg.shape, lambda i: (0, 0)), resident(wu.shape, lambda i: (0, 0)),
                  resident(wd.shape, lambda i: (0, 0))],
        out_specs=pl.BlockSpec((tm, d), lambda i: (i, 0)),
        out_shape=jax.ShapeDtypeStruct((n, d), F32),
        compiler_params=_cparams("parallel"),
    )(x, g_in, g_out, wg, wu, wd)


def _xattn_kernel(q_ref, mk_ref, mv_ref, o_ref):
    outs = []
    for hd in range(XATTN_HEADS):
        cols = slice(hd * XATTN_HEAD_DIM, (hd + 1) * XATTN_HEAD_DIM)
        k = mk_ref[0, :, cols].astype(BF16)
        v = mv_ref[0, :, cols].astype(BF16)
        s = _dot_nt(q_ref[0, :, cols], k)
        e = jnp.exp(s - jnp.max(s, axis=-1, keepdims=True))
        p = e / jnp.sum(e, axis=-1, keepdims=True)
        outs.append(_dot(p.astype(BF16), v).astype(BF16))
    o_ref[0] = jnp.concatenate(outs, axis=-1)


def xattn(q, mk, mv, tq):
    b, t, d = q.shape
    return pl.pallas_call(
        _xattn_kernel,
        grid=(b, t // tq),
        in_specs=[pl.BlockSpec((1, tq, d), lambda i, j: (i, j, 0)),
                  pl.BlockSpec((1, N_MEM, d), lambda i, j: (i, 0, 0)),
                  pl.BlockSpec((1, N_MEM, d), lambda i, j: (i, 0, 0))],
        out_specs=pl.BlockSpec((1, tq, d), lambda i, j: (i, j, 0)),
        out_shape=jax.ShapeDtypeStruct((b, t, d), BF16),
        compiler_params=_cparams("parallel", "parallel"),
    )(q, mk, mv)


def _pool_kernel(u_ref, past_ref, w_ref, scale_ref, o_ref, buf, *, t, chunk, start_pos):
    buf[0:POOL_HALO, :] = past_ref[0]
    buf[POOL_HALO:POOL_HALO + t, :] = u_ref[0]
    for c in range(t // chunk):
        base = POOL_HALO + c * chunk
        pos = start_pos + c * chunk + lax.broadcasted_iota(jnp.int32, (chunk, 1), 0)
        for gi, win in enumerate(POOL_WINDOWS):
            cols = slice(gi * POOL_GROUP, (gi + 1) * POOL_GROUP)
            cur = buf[base:base + chunk, cols]
            win_sum = cur
            for k in range(1, win):
                win_sum = win_sum + buf[base - k:base - k + chunk, cols]
            cnt = jnp.minimum(pos + 1, win).astype(F32)
            diff = win_sum / cnt - cur
            y = _dot(diff.astype(BF16), w_ref[gi]) * scale_ref[:, cols]
            o_ref[0, c * chunk:(c + 1) * chunk, cols] = y.astype(o_ref.dtype)


def pool_mix(u, past, w, scale, start_pos):
    b, t, c = u.shape
    chunk = min(t, 256)
    return pl.pallas_call(
        functools.partial(_pool_kernel, t=t, chunk=chunk, start_pos=start_pos),
        grid=(b,),
        in_specs=[pl.BlockSpec((1, t, c), lambda i: (i, 0, 0)),
                  pl.BlockSpec((1, POOL_HALO, c), lambda i: (i, 0, 0)),
                  _full(w.shape), _full(scale.shape)],
        out_specs=pl.BlockSpec((1, t, c), lambda i: (i, 0, 0)),
        out_shape=jax.ShapeDtypeStruct((b, t, c), BF16),
        scratch_shapes=[pltpu.VMEM((POOL_HALO + t, c), F32)],
        compiler_params=_cparams("parallel"),
    )(u, past, w, scale)


def _compress_kernel(kc_ref, vc_ref, pool_ref, wkt_ref, wvt_ref, ident_ref, kt_ref, k_ref, vt_ref):
    def compress(src, wt_ref):
        x = src[0]
        hi = x.astype(BF16)
        lo = (x - hi.astype(F32)).astype(BF16)
        mean_t = _dot(hi, pool_ref[...]) + _dot(lo, pool_ref[...])
        return _dot(wt_ref[...], mean_t.astype(BF16)).astype(BF16)
    kcmp_t = compress(kc_ref, wkt_ref)
    kt_ref[0] = kcmp_t
    k_ref[0] = _dot_nt(ident_ref[...], kcmp_t).astype(BF16)
    vt_ref[0] = compress(vc_ref, wvt_ref)


def compress_kv(kc_t, vc_t, pool_mat, wk_bd_t, wv_bd_t, ident):
    b, c, t = kc_t.shape
    n_cmp = pool_mat.shape[1]
    per_b = pl.BlockSpec((1, c, t), lambda i: (i, 0, 0))
    out = pl.BlockSpec((1, c, n_cmp), lambda i: (i, 0, 0))
    return pl.pallas_call(
        _compress_kernel,
        grid=(b,),
        in_specs=[per_b, per_b, _full(pool_mat.shape), _full(wk_bd_t.shape), _full(wv_bd_t.shape),
                  _full(ident.shape)],
        out_specs=[out, pl.BlockSpec((1, n_cmp, c), lambda i: (i, 0, 0)), out],
        out_shape=[jax.ShapeDtypeStruct((b, c, n_cmp), BF16), jax.ShapeDtypeStruct((b, n_cmp, c), BF16),
                   jax.ShapeDtypeStruct((b, c, n_cmp), BF16)],
        compiler_params=_cparams("parallel"),
    )(kc_t, vc_t, pool_mat, wk_bd_t, wv_bd_t, ident)


def _masked_softmax_rows(s, mask):
    s = jnp.where(mask, s, NEG_INF)
    e = jnp.where(mask, jnp.exp(s - jnp.max(s, axis=-1, keepdims=True)), 0.0)
    return e / jnp.maximum(jnp.sum(e, axis=-1, keepdims=True), 1e-30)


def _flash_step(q_g, k_t, v_t, mask, carry, tq):
    m, l, acc = carry
    tk = k_t.shape[1]
    s = _dot(q_g, k_t).reshape(HEADS_PER_GROUP, tq, tk)
    s = jnp.where(mask[None], s, NEG_INF).reshape(HEADS_PER_GROUP * tq, tk)
    m_new = jnp.maximum(m, jnp.max(s, axis=-1, keepdims=True))
    alpha = jnp.exp(m - m_new)
    e = jnp.exp(s - m_new).reshape(HEADS_PER_GROUP, tq, tk)
    e = jnp.where(mask[None], e, 0.0).reshape(HEADS_PER_GROUP * tq, tk)
    l = alpha * l + jnp.sum(e, axis=-1, keepdims=True)
    acc = alpha * acc + _dot_nt(e.astype(BF16), v_t)
    return m_new, l, acc


def _nsa_prompt_kernel(q_ref, kcmpt_ref, kcmp_ref, vcmpt_ref, ks_ref, vs_ref, kw_ref, vw_ref, gate_ref,
                       mselt_ref, expand_ref, ident_ref, o_ref, mask_s, *, tq, t):
    i = pl.program_id(1)
    q0 = i * tq
    n_cmp = kcmp_ref.shape[1]
    n_sel = mselt_ref.shape[0]
    n_kt = t // tq
    qpos_col = q0 + lax.broadcasted_iota(jnp.int32, (tq, 1), 0)
    qpos_row = q0 + lax.broadcasted_iota(jnp.int32, (1, tq), 1)
    krel_row = lax.broadcasted_iota(jnp.int32, (1, tq), 1)
    cmp_end_col = lax.broadcasted_iota(jnp.int32, (n_cmp, 1), 0) * CMP_STRIDE + (CMP_BLOCK - 1)
    cmp_end_row = lax.broadcasted_iota(jnp.int32, (1, n_cmp), 1) * CMP_STRIDE + (CMP_BLOCK - 1)
    cmp_mask_t = cmp_end_col <= qpos_row
    cmp_mask = cmp_end_row <= qpos_col
    sel_j = lax.broadcasted_iota(jnp.int32, (n_sel, 1), 0)
    cur_row = qpos_row // SEL_BLOCK

    gate_all = gate_ref[0]
    head_out = [None] * NSA_HEADS
    for g in range(NSA_KV_HEADS):
        gcols = slice(g * HEAD_DIM, (g + 1) * HEAD_DIM)
        kc_g = kcmp_ref[0, :, gcols]
        kct_g = kcmpt_ref[0, gcols, :]
        vct_g = vcmpt_ref[0, gcols, :]
        q_heads = [q_ref[0, :, (g * HEADS_PER_GROUP + hh) * HEAD_DIM:(g * HEADS_PER_GROUP + hh + 1) * HEAD_DIM]
                   for hh in range(HEADS_PER_GROUP)]

        psum_t = None
        o_cmp = []
        for q_h in q_heads:
            s_t = jnp.where(cmp_mask_t, _dot_nt(kc_g, q_h), NEG_INF)
            e_t = jnp.where(cmp_mask_t, jnp.exp(s_t - jnp.max(s_t, axis=0, keepdims=True)), 0.0)
            p_t = e_t / jnp.maximum(jnp.sum(e_t, axis=0, keepdims=True), 1e-30)
            psum_t = p_t if psum_t is None else psum_t + p_t
            p = _masked_softmax_rows(_dot(q_h, kct_g), cmp_mask)
            o_cmp.append(_dot_nt(p.astype(BF16), vct_g))

        imp_t = None
        for part in _split3(psum_t):
            d = _dot(mselt_ref[...], part)
            imp_t = d if imp_t is None else imp_t + d
        valid = sel_j * SEL_BLOCK <= qpos_row
        forced = (sel_j == 0) | (sel_j == cur_row) | (sel_j == cur_row - 1)
        score = jnp.where(forced, FORCE_SCORE, jnp.where(valid, imp_t, NEG_INF))
        rank = jnp.zeros((n_sel, tq), F32)
        for jp in range(n_sel):
            sj = score[jp:jp + 1, :]
            beats = (sj > score) | ((sj == score) & (sel_j > jp))
            rank = rank + jnp.where(beats, 1.0, 0.0)
        sel_t = jnp.where(rank < SEL_TOPK, 1.0, 0.0).astype(BF16)
        sel = _dot_nt(ident_ref[...], sel_t).astype(BF16)
        maskf = _dot(sel, expand_ref[...])
        for kt in range(n_kt):
            mask_s[kt] = maskf[:, kt * tq:(kt + 1) * tq]

        q_g = jnp.concatenate(q_heads, axis=0)
        init = (jnp.full((HEADS_PER_GROUP * tq, 1), NEG_INF, F32),
                jnp.zeros((HEADS_PER_GROUP * tq, 1), F32),
                jnp.zeros((HEADS_PER_GROUP * tq, HEAD_DIM), F32))

        def slc_body(kt, carry):
            keys = pl.ds(pl.multiple_of(kt * tq, tq), tq)
            k_t = ks_ref[0, gcols, keys].astype(BF16)
            v_t = vs_ref[0, gcols, keys].astype(BF16)
            mask = (mask_s[kt] > 0.5) & (kt * tq + krel_row <= qpos_col)
            return _flash_step(q_g, k_t, v_t, mask, carry, tq)

        _, l_s, acc_s = lax.fori_loop(0, i + 1, slc_body, init)
        o_slc = acc_s / l_s

        def win_body(kt, carry):
            keys = pl.ds(pl.multiple_of(kt * tq, tq), tq)
            k_t = kw_ref[0, gcols, keys].astype(BF16)
            v_t = vw_ref[0, gcols, keys].astype(BF16)
            dist = qpos_col - (kt * tq + krel_row)
            mask = (dist >= 0) & (dist < WINDOW)
            return _flash_step(q_g, k_t, v_t, mask, carry, tq)

        _, l_w, acc_w = lax.fori_loop(jnp.maximum(i - WINDOW // tq, 0), i + 1, win_body, init)
        o_win = acc_w / l_w

        for hh in range(HEADS_PER_GROUP):
            h = g * HEADS_PER_GROUP + hh
            rows = slice(hh * tq, (hh + 1) * tq)
            gate = gate_all[:, 3 * h:3 * h + 3]
            head_out[h] = (gate[:, 0:1] * o_cmp[hh] + gate[:, 1:2] * o_slc[rows]
                           + gate[:, 2:3] * o_win[rows])
    o_ref[0] = jnp.concatenate(head_out, axis=-1).astype(o_ref.dtype)


def nsa_prompt(q, kcmp_t, kcmp, vcmp_t, ks_t, vs_t, kw_t, vw_t, gate, mselt, expand, ident, tq):
    b, t, _ = q.shape
    tile = lambda w: pl.BlockSpec((1, tq, w), lambda i, j: (i, j, 0))
    whole = lambda a: pl.BlockSpec((1,) + a.shape[1:], lambda i, j: (i, 0, 0))
    per_batch = [kcmp_t, kcmp, vcmp_t, ks_t, vs_t, kw_t, vw_t]
    return pl.pallas_call(
        functools.partial(_nsa_prompt_kernel, tq=tq, t=t),
        grid=(b, t // tq),
        in_specs=([tile(NSA_WIDTH)] + [whole(a) for a in per_batch]
                  + [tile(LANE), _full(mselt.shape), _full(expand.shape), _full(ident.shape)]),
        out_specs=tile(NSA_WIDTH),
        out_shape=jax.ShapeDtypeStruct((b, t, NSA_WIDTH), BF16),
        scratch_shapes=[pltpu.VMEM((t // tq, tq, tq), F32)],
        compiler_params=_cparams("parallel", "arbitrary"),
    )(q, *per_batch, gate, mselt, expand, ident)


def _stack_heads(q, g):
    return jnp.concatenate(
        [q[:, (g * HEADS_PER_GROUP + hh) * HEAD_DIM:(g * HEADS_PER_GROUP + hh + 1) * HEAD_DIM]
         for hh in range(HEADS_PER_GROUP)], axis=0)


def _sample_cmp_kernel(pt_ref, *refs, n_valid, past_len):
    del pt_ref
    p = PAGES_PER_STEP
    tp = SAMPLE_T_PAD
    k_pages, v_pages = refs[:p], refs[p:2 * p]
    (kn_ref, vn_ref, q_ref, wkt_ref, wvt_ref, msel_ref, subpool_ref, ident_ref,
     ocmp_ref, sel_ref, sub_k, sub_v) = refs[2 * p:]
    c = pl.program_id(1)
    n_sub = past_len // CMP_STRIDE
    steps = past_len // (p * PAGE_SIZE)

    for pages, sub in ((k_pages, sub_k), (v_pages, sub_v)):
        x = jnp.concatenate([r[...] for r in pages], axis=1)
        hi = x.astype(BF16)
        lo = (x - hi.astype(F32)).astype(BF16)
        sub[c] = _dot(hi, subpool_ref[...]) + _dot(lo, subpool_ref[...])

    @pl.when(c == steps - 1)
    def _():
        t_col = lax.broadcasted_iota(jnp.int32, (tp, 1), 0)
        new_valid = lax.broadcasted_iota(jnp.int32, (1, tp), 1) < n_valid
        first_lane = lax.broadcasted_iota(jnp.int32, (1, SUB_PER_STEP), 1) == 0
        cmp_kv = []
        for new_ref, sub, wt_ref in ((kn_ref, sub_k, wkt_ref), (vn_ref, sub_v, wvt_ref)):
            new_t = None
            for part in _split3(new_ref[0]):
                d = _dot_nt(ident_ref[...], part)
                new_t = d if new_t is None else new_t + d
            new_sum = jnp.sum(jnp.where(new_valid, new_t, 0.0), axis=1, keepdims=True)
            sub[steps] = jnp.where(first_lane, new_sum, 0.0)
            sums = jnp.concatenate([sub[s] for s in range(steps + 1)], axis=1)
            mean_t = (sums[:, 0:n_sub] + sums[:, 1:n_sub + 1]) * (1.0 / CMP_BLOCK)
            cmp_kv.append(_dot(wt_ref[...], mean_t.astype(BF16)).astype(BF16))
        kcmp_t, vcmp_t = cmp_kv

        rows = HEADS_PER_GROUP * tp
        qpos_col = past_len + lax.broadcasted_iota(jnp.int32, (rows, 1), 0) % tp
        cmp_end_row = lax.broadcasted_iota(jnp.int32, (1, n_sub), 1) * CMP_STRIDE + (CMP_BLOCK - 1)
        cmp_mask = cmp_end_row <= qpos_col

        n_lane = sel_ref.shape[2]
        lane = lax.broadcasted_iota(jnp.int32, (1, n_lane), 1)
        sel_j = (lane // LANE) * SEL_PER_STEP + lane % LANE
        lane_used = lane % LANE < SEL_PER_STEP
        qpos_t = past_len + t_col
        cur = qpos_t // SEL_BLOCK
        valid = lane_used & (sel_j * SEL_BLOCK <= qpos_t)
        forced = lane_used & ((sel_j == 0) | (sel_j == cur) | (sel_j == cur - 1))

        q = q_ref[0]
        o_heads = [None] * NSA_HEADS
        for g in range(NSA_KV_HEADS):
            gcols = slice(g * HEAD_DIM, (g + 1) * HEAD_DIM)
            prob = _masked_softmax_rows(_dot(_stack_heads(q, g), kcmp_t[gcols, :]), cmp_mask)
            o_g = _dot_nt(prob.astype(BF16), vcmp_t[gcols, :])
            psum = prob[0:tp]
            for hh in range(1, HEADS_PER_GROUP):
                psum = psum + prob[hh * tp:(hh + 1) * tp]
                o_heads[g * HEADS_PER_GROUP + hh] = o_g[hh * tp:(hh + 1) * tp]
            o_heads[g * HEADS_PER_GROUP] = o_g[0:tp]
            imp = None
            for part in _split3(psum):
                d = _dot(part, msel_ref[...])
                imp = d if imp is None else imp + d
            score = jnp.where(forced, FORCE_SCORE, jnp.where(valid, imp, NEG_INF))
            chosen = jnp.zeros((tp, n_lane), F32)
            for _ in range(SEL_TOPK):
                best = jnp.max(score, axis=-1, keepdims=True)
                first = jnp.min(jnp.where(score == best, lane, n_lane), axis=-1, keepdims=True)
                hit = lane == first
                chosen = jnp.where(hit, 1.0, chosen)
                score = jnp.where(hit, REMOVED_SCORE, score)
            sel_ref[0, g * tp:(g + 1) * tp, :] = chosen
        ocmp_ref[0] = jnp.concatenate(o_heads, axis=-1)


def _page_spec(k, layer, n_layers):
    return pl.BlockSpec((None, KV_COLS, PAGE_SIZE),
                        lambda i, c, pt: (pt[i, c * PAGES_PER_STEP + k] * n_layers + layer, 0, 0))


def sample_cmp(page_table, cache_k, cache_v, kc_new, vc_new, q, wk_bd_t, wv_bd_t, msel, subpool, ident,
               layer, n_layers, n_valid):
    b, n_pages = page_table.shape
    past_len = n_pages * PAGE_SIZE
    steps = n_pages // PAGES_PER_STEP
    n_lane = msel.shape[1]
    per_b = lambda a: pl.BlockSpec((1,) + a.shape[1:], lambda i, c, pt: (i, 0, 0))
    const = lambda a: pl.BlockSpec(a.shape, lambda i, c, pt: (0,) * a.ndim)
    consts = [wk_bd_t, wv_bd_t, msel, subpool, ident]
    grid_spec = pltpu.PrefetchScalarGridSpec(
        num_scalar_prefetch=1,
        grid=(b, steps),
        in_specs=([_page_spec(k, layer, n_layers) for k in range(PAGES_PER_STEP)] * 2
                  + [per_b(kc_new), per_b(vc_new), per_b(q)] + [const(a) for a in consts]),
        out_specs=[pl.BlockSpec((1, SAMPLE_T_PAD, NSA_WIDTH), lambda i, c, pt: (i, 0, 0)),
                   pl.BlockSpec((1, 2 * SAMPLE_T_PAD, n_lane), lambda i, c, pt: (i, 0, 0))],
        scratch_shapes=[pltpu.VMEM((steps + 1, KV_COLS, SUB_PER_STEP), F32)] * 2,
    )
    return pl.pallas_call(
        functools.partial(_sample_cmp_kernel, n_valid=n_valid, past_len=past_len),
        grid_spec=grid_spec,
        out_shape=[jax.ShapeDtypeStruct((b, SAMPLE_T_PAD, NSA_WIDTH), F32),
                   jax.ShapeDtypeStruct((b, 2 * SAMPLE_T_PAD, n_lane), F32)],
        compiler_params=_cparams("parallel", "arbitrary"),
    )(page_table, *([cache_k] * PAGES_PER_STEP), *([cache_v] * PAGES_PER_STEP),
      kc_new, vc_new, q, *consts)


def _sample_slc_kernel(pt_ref, *refs, n_valid, past_len):
    del pt_ref
    p = PAGES_PER_STEP
    tp = SAMPLE_T_PAD
    rows = HEADS_PER_GROUP * tp
    k_pages, v_pages = refs[:p], refs[p:2 * p]
    (q_ref, sel_ref, expand_ref, ksn_ref, vsn_ref, wk_ref, wv_ref, kwn_ref, vwn_ref, ocmp_ref, gate_ref,
     o_ref, m_s, l_s, acc_s) = refs[2 * p:]
    c = pl.program_id(1)
    q = q_ref[0]
    t_col = lax.broadcasted_iota(jnp.int32, (rows, 1), 0) % tp
    new_row = lax.broadcasted_iota(jnp.int32, (1, tp), 1)
    new_mask = (new_row <= t_col) & (new_row < n_valid)

    @pl.when(c == 0)
    def _():
        m_s[...] = jnp.full(m_s.shape, NEG_INF, F32)
        l_s[...] = jnp.zeros(l_s.shape, F32)
        acc_s[...] = jnp.zeros(acc_s.shape, F32)

    k_all_t = jnp.concatenate([r[...] for r in k_pages], axis=1).astype(BF16)
    v_all_t = jnp.concatenate([r[...] for r in v_pages], axis=1).astype(BF16)
    for g in range(NSA_KV_HEADS):
        gcols = slice(g * HEAD_DIM, (g + 1) * HEAD_DIM)
        q_g = _stack_heads(q, g)
        keymask = _dot(sel_ref[0, g * tp:(g + 1) * tp, :].astype(BF16), expand_ref[...])
        keymask = jnp.concatenate([keymask] * HEADS_PER_GROUP, axis=0) > 0.5
        s = jnp.where(keymask, _dot(q_g, k_all_t[gcols, :]), NEG_INF)
        m_old = m_s[g]
        m_new = jnp.maximum(m_old, jnp.max(s, axis=-1, keepdims=True))
        alpha = jnp.exp(m_old - m_new)
        e = jnp.where(keymask, jnp.exp(s - m_new), 0.0)
        l_s[g] = alpha * l_s[g] + jnp.sum(e, axis=-1, keepdims=True)
        acc_s[g] = alpha * acc_s[g] + _dot_nt(e.astype(BF16), v_all_t[gcols, :])
        m_s[g] = m_new

    @pl.when(c == pl.num_programs(1) - 1)
    def _():
        n_win = wk_ref.shape[2]
        win_row = lax.broadcasted_iota(jnp.int32, (1, n_win), 1)
        dist = t_col + n_win - win_row
        win_mask = (dist >= 0) & (dist < WINDOW)
        gate_all = gate_ref[0]
        head_out = [None] * NSA_HEADS
        for g in range(NSA_KV_HEADS):
            gcols = slice(g * HEAD_DIM, (g + 1) * HEAD_DIM)
            q_g = _stack_heads(q, g)
            s_n = jnp.where(new_mask, _dot_nt(q_g, ksn_ref[0, :, gcols].astype(BF16)), NEG_INF)
            m_old = m_s[g]
            m_new = jnp.maximum(m_old, jnp.max(s_n, axis=-1, keepdims=True))
            alpha = jnp.exp(m_old - m_new)
            e_n = jnp.where(new_mask, jnp.exp(s_n - m_new), 0.0)
            l_fin = alpha * l_s[g] + jnp.sum(e_n, axis=-1, keepdims=True)
            o_slc = (alpha * acc_s[g] + _dot(e_n.astype(BF16), vsn_ref[0, :, gcols].astype(BF16))) / l_fin
            s_p = jnp.where(win_mask, _dot(q_g, wk_ref[0, gcols, :].astype(BF16)), NEG_INF)
            s_w = jnp.where(new_mask, _dot_nt(q_g, kwn_ref[0, :, gcols].astype(BF16)), NEG_INF)
            m_w = jnp.maximum(jnp.max(s_p, axis=-1, keepdims=True), jnp.max(s_w, axis=-1, keepdims=True))
            e_p = jnp.where(win_mask, jnp.exp(s_p - m_w), 0.0)
            e_w = jnp.where(new_mask, jnp.exp(s_w - m_w), 0.0)
            l_w = jnp.sum(e_p, axis=-1, keepdims=True) + jnp.sum(e_w, axis=-1, keepdims=True)
            o_win = (_dot_nt(e_p.astype(BF16), wv_ref[0, gcols, :].astype(BF16))
                     + _dot(e_w.astype(BF16), vwn_ref[0, :, gcols].astype(BF16))) / l_w
            for hh in range(HEADS_PER_GROUP):
                h = g * HEADS_PER_GROUP + hh
                r = slice(hh * tp, (hh + 1) * tp)
                gate = gate_all[:, 3 * h:3 * h + 3]
                head_out[h] = (gate[:, 0:1] * ocmp_ref[0, :, h * HEAD_DIM:(h + 1) * HEAD_DIM]
                               + gate[:, 1:2] * o_slc[r] + gate[:, 2:3] * o_win[r])
        o_ref[0] = jnp.concatenate(head_out, axis=-1).astype(o_ref.dtype)


def sample_slc(page_table, cache_k, cache_v, q, sel, expand, ks_new, vs_new, win_k, win_v, kw_new, vw_new,
               o_cmp, gate, layer, n_layers, n_valid):
    b, n_pages = page_table.shape
    past_len = n_pages * PAGE_SIZE
    steps = n_pages // PAGES_PER_STEP
    rows = HEADS_PER_GROUP * SAMPLE_T_PAD
    per_b = lambda a: pl.BlockSpec((1,) + a.shape[1:], lambda i, c, pt: (i, 0, 0))
    const = lambda a: pl.BlockSpec(a.shape, lambda i, c, pt: (0,) * a.ndim)
    grid_spec = pltpu.PrefetchScalarGridSpec(
        num_scalar_prefetch=1,
        grid=(b, steps),
        in_specs=([_page_spec(k, layer, n_layers) for k in range(PAGES_PER_STEP)] * 2
                  + [per_b(q), pl.BlockSpec((1, 2 * SAMPLE_T_PAD, LANE), lambda i, c, pt: (i, 0, c)),
                     const(expand), per_b(ks_new), per_b(vs_new), per_b(win_k), per_b(win_v),
                     per_b(kw_new), per_b(vw_new), per_b(o_cmp), per_b(gate)]),
        out_specs=pl.BlockSpec((1, SAMPLE_T_PAD, NSA_WIDTH), lambda i, c, pt: (i, 0, 0)),
        scratch_shapes=[pltpu.VMEM((NSA_KV_HEADS, rows, 1), F32), pltpu.VMEM((NSA_KV_HEADS, rows, 1), F32),
                        pltpu.VMEM((NSA_KV_HEADS, rows, HEAD_DIM), F32)],
    )
    return pl.pallas_call(
        functools.partial(_sample_slc_kernel, n_valid=n_valid, past_len=past_len),
        grid_spec=grid_spec,
        out_shape=jax.ShapeDtypeStruct((b, SAMPLE_T_PAD, NSA_WIDTH), BF16),
        compiler_params=_cparams("parallel", "arbitrary"),
    )(page_table, *([cache_k] * PAGES_PER_STEP), *([cache_v] * PAGES_PER_STEP),
      q, sel, expand, ks_new, vs_new, win_k, win_v, kw_new, vw_new, o_cmp, gate)


def _lru_kernel(xb_ref, yb_ref, cpast_ref, h0_ref, cw_ref, cb_ref, wa_ref, wx_ref, ba_ref, bx_ref, lam_ref,
                o_ref, tail_ref, xbuf, h_carry, a_s, d_s, *, tt, first_pos_zero):
    j = pl.program_id(1)
    halo = 8

    @pl.when(j == 0)
    def _():
        xbuf[0:halo, :] = cpast_ref[0]
        h_carry[...] = h0_ref[0]

    xbuf[halo:halo + tt, :] = xb_ref[0]
    xc = xbuf[halo - 3:halo - 3 + tt, :] * cw_ref[0:1, :]
    for k in range(1, CONV_WIDTH):
        xc = xc + xbuf[halo - 3 + k:halo - 3 + k + tt, :] * cw_ref[k:k + 1, :]
    xf = xc + cb_ref[...]
    xf_b = xf.astype(BF16)
    r_parts, i_parts = [], []
    for n in range(LRU_BLOCKS):
        cols = slice(n * LRU_BLOCK, (n + 1) * LRU_BLOCK)
        r_parts.append(_dot(xf_b[:, cols], wa_ref[n]))
        i_parts.append(_dot(xf_b[:, cols], wx_ref[n]))
    r = jax.nn.sigmoid(jnp.concatenate(r_parts, axis=-1) + ba_ref[...])
    gate_i = jax.nn.sigmoid(jnp.concatenate(i_parts, axis=-1) + bx_ref[...])
    neg_lam = -lam_ref[...]
    softplus = jnp.maximum(neg_lam, 0.0) + jnp.log1p(jnp.exp(-jnp.abs(neg_lam)))
    log_a = -LRU_C * r * softplus
    a = jnp.exp(log_a)
    mult = jnp.sqrt(-jnp.tanh(log_a) * (a * a + 1.0))
    if first_pos_zero:
        row = lax.broadcasted_iota(jnp.int32, (tt, 1), 0)
        mult = jnp.where((row == 0) & (j == 0), 1.0, mult)
    a_s[...] = a
    d_s[...] = mult * gate_i * xf

    sub = lax.broadcasted_iota(jnp.int32, (8, LRU_WIDTH), 0)

    def body(blk, h):
        rows = pl.ds(pl.multiple_of(blk * 8, 8), 8)
        a8 = a_s[rows, :]
        d8 = d_s[rows, :]
        for s in (1, 2, 4):
            keep = sub >= s
            d8 = jnp.where(keep, a8 * pltpu.roll(d8, s, 0) + d8, d8)
            a8 = jnp.where(keep, a8 * pltpu.roll(a8, s, 0), a8)
        hs = a8 * h + d8
        d_s[rows, :] = hs
        return hs[7:8, :]

    h_carry[...] = lax.fori_loop(0, tt // 8, body, h_carry[...])
    hs = d_s[...]
    o_ref[0] = (hs * jax.nn.gelu(yb_ref[0])).astype(o_ref.dtype)
    tail_ref[0] = d_s[tt - 8:tt, :]
    xbuf[0:halo, :] = xbuf[tt:tt + halo, :]


def lru(xb, yb, conv_past, h0, cw, cb, wa, wx, ba, bx, lam, tt, first_pos_zero):
    b, t, w = xb.shape
    tile = pl.BlockSpec((1, tt, w), lambda i, j: (i, j, 0))
    per_b = lambda a: pl.BlockSpec((1,) + a.shape[1:], lambda i, j: (i, 0, 0))
    consts = [cw, cb, wa, wx, ba, bx, lam]
    return pl.pallas_call(
        functools.partial(_lru_kernel, tt=tt, first_pos_zero=first_pos_zero),
        grid=(b, t // tt),
        in_specs=[tile, tile, per_b(conv_past), per_b(h0)] + [_full(a.shape) for a in consts],
        out_specs=[tile, pl.BlockSpec((1, 8, w), lambda i, j: (i, 0, 0))],
        out_shape=[jax.ShapeDtypeStruct((b, t, w), BF16), jax.ShapeDtypeStruct((b, 8, w), F32)],
        scratch_shapes=[pltpu.VMEM((8 + tt, w), F32), pltpu.VMEM((1, w), F32),
                        pltpu.VMEM((tt, w), F32), pltpu.VMEM((tt, w), F32)],
        compiler_params=_cparams("parallel", "arbitrary"),
    )(xb, yb, conv_past, h0, *consts)


def _block_mean_matrix(t):
    n_cmp = t // CMP_STRIDE - CMP_BLOCK // CMP_STRIDE + 1
    m = np.zeros((t // CMP_STRIDE, t), np.float32)
    for i in range(n_cmp):
        m[i, i * CMP_STRIDE:i * CMP_STRIDE + CMP_BLOCK] = 1.0 / CMP_BLOCK
    return m


def _sel_from_cmp(n_cmp_rows, n_cmp, n_sel):
    m = np.zeros((n_cmp_rows, n_sel), np.float32)
    for j in range(n_sel):
        for c in range(4 * j - 1, 4 * j + 4):
            if 0 <= c < n_cmp:
                m[c, j] = 1.0
    return m


def _block_diag(w):
    z = jnp.zeros((HEAD_DIM, HEAD_DIM), w.dtype)
    return jnp.concatenate([jnp.concatenate([w[0], z], axis=1), jnp.concatenate([z, w[1]], axis=1)], axis=0)


def _sigmoid(v):
    return jax.nn.sigmoid(v)


def _scale_q(v):
    return v * (HEAD_DIM ** -0.5)


_EVEN_GROUPS = (
    (0, POOL_WIDTH, ((0, POOL_WIDTH, None),)),
    (POOL_WIDTH, NSA_WIDTH, ((0, NSA_WIDTH, _scale_q),)),
    (POOL_WIDTH + NSA_WIDTH, 6 * KV_COLS, tuple((k * KV_COLS, KV_COLS, None) for k in range(6))),
    (POOL_WIDTH + NSA_WIDTH + 6 * KV_COLS, LANE, ((0, LANE, _sigmoid),)),
)
_EVEN_DTYPES = (F32, BF16) + (F32,) * 6 + (F32,)


def _even_in_proj(x2d, g, w_in_pad, tm):
    return norm_matmul(x2d, g, w_in_pad, _EVEN_GROUPS, _EVEN_DTYPES, tm)


def _xattn_block(x2d, b, t, mk, mv, g_pre, g_post, wq, wo, tm, tq):
    q, = norm_matmul(x2d, g_pre, wq, ((0, D_MODEL, ((0, D_MODEL, lambda v: v * (XATTN_HEAD_DIM ** -0.5)),)),),
                     (BF16,), tm)
    o = xattn(q.reshape(b, t, D_MODEL), mk, mv, tq).reshape(b * t, D_MODEL)
    return proj_norm_res([o], [wo], g_post, x2d, tm)


def kernel(x_prompt, mem_prompt, x_sample, state_pool, cache_cmp_k, cache_cmp_v, cache_slc_k, cache_slc_v, cache_win_k, cache_win_v, state_lru_h, state_lru_conv, cache_mem_k, cache_mem_v, page_table, norm_gain, mem_norm_gain, w_in_even, pool_w, pool_scale, w_cmp_k, w_cmp_v, w_out_even, w_in_odd, conv_w, conv_b, lru_wa, lru_ba, lru_wx, lru_bx, lru_lambda, w_out_odd, w_xq, w_xk, w_xv, w_xo, w_ffn_gate, w_ffn_up, w_ffn_down):
    bp, tp, d = x_prompt.shape
    bs, ts, _ = x_sample.shape
    depth = norm_gain.shape[0]
    n_even = w_in_even.shape[0]
    n_pages = page_table.shape[1]
    past_len = n_pages * PAGE_SIZE
    n_phys = cache_cmp_k.shape[0]
    tsp = SAMPLE_T_PAD
    np_tok, ns_tok = bp * tp, bs * tsp
    tm_p, tm_s = 512, ns_tok
    tq = 256

    xp = x_prompt.reshape(np_tok, d)
    xs = jnp.pad(x_sample, ((0, 0), (0, tsp - ts), (0, 0))).reshape(ns_tok, d)

    n_sub_p = tp // CMP_STRIDE
    n_cmp_p = n_sub_p - 1
    n_sel_p = tp // SEL_BLOCK
    pool_mat = jnp.asarray(_block_mean_matrix(tp).T, BF16)
    mselt = jnp.asarray(_sel_from_cmp(n_sub_p, n_cmp_p, n_sel_p).T, BF16)
    expand_p = jnp.asarray(np.repeat(np.eye(n_sel_p, dtype=np.float32), SEL_BLOCK, axis=1), BF16)
    ident = jnp.asarray(np.eye(tq, dtype=np.float32), BF16)
    ident_kv = jnp.asarray(np.eye(KV_COLS, dtype=np.float32), BF16)
    subpool = jnp.asarray(np.repeat(np.eye(SUB_PER_STEP, dtype=np.float32), CMP_STRIDE, axis=0), BF16)
    steps = n_pages // PAGES_PER_STEP
    n_sub_s = past_len // CMP_STRIDE
    n_sel_s = -(-(past_len + ts) // SEL_BLOCK)
    n_lane_s = (steps + 1) * LANE
    msel_np = np.zeros((n_sub_s, n_lane_s), np.float32)
    dense = _sel_from_cmp(n_sub_s, n_sub_s, n_sel_s)
    for j in range(n_sel_s):
        msel_np[:, (j // SEL_PER_STEP) * LANE + j % SEL_PER_STEP] = dense[:, j]
    msel_s = jnp.asarray(msel_np, BF16)
    expand_np = np.zeros((LANE, PAGES_PER_STEP * PAGE_SIZE), np.float32)
    expand_np[:SEL_PER_STEP] = np.repeat(np.eye(SEL_PER_STEP, dtype=np.float32), SEL_BLOCK, axis=1)
    expand_s = jnp.asarray(expand_np, BF16)

    row = lambda v: v.reshape(1, -1)
    out_ev_p, out_ev_s, out_lru_p, out_lru_s, mem_k_p, mem_v_p = [], [], [], [], [], []

    for li in range(depth):
        gn = norm_gain[li]
        if li % 2 == 0:
            e = li // 2
            w_in = jnp.pad(w_in_even[e], ((0, 0), (0, EVEN_IN_PAD - EVEN_IN))).astype(BF16)
            pw = pool_w[e].astype(BF16)
            ps = row(pool_scale[e])
            wk_bd_t = _block_diag(w_cmp_k[e]).T.astype(BF16)
            wv_bd_t = _block_diag(w_cmp_v[e]).T.astype(BF16)
            w_out = w_out_even[e].astype(BF16)
            kv0 = POOL_WIDTH + NSA_WIDTH
            w_uq = w_in[:, :kv0]
            w_kv_t = w_in[:, kv0:kv0 + 6 * KV_COLS].T
            w_gate = w_in[:, kv0 + 6 * KV_COLS:kv0 + 6 * KV_COLS + LANE]

            u, q, gate, kc, vc, ks, vs, kw, vw = even_in_prompt(xp, row(gn[0]), w_uq, w_gate, w_kv_t,
                                                                bp, tp, tm_p)
            as3 = lambda a: a.reshape(bp, tp, a.shape[-1])
            u3 = as3(u)
            a_out = pool_mix(u3, jnp.zeros((bp, POOL_HALO, POOL_WIDTH), F32), pw, ps, 0)
            kcmp_t, kcmp, vcmp_t = compress_kv(kc, vc, pool_mat, wk_bd_t, wv_bd_t, ident_kv)
            o_nsa = nsa_prompt(as3(q), kcmp_t, kcmp, vcmp_t, ks, vs, kw, vw, as3(gate),
                               mselt, expand_p, ident, tq)
            xp = proj_norm_res([a_out.reshape(np_tok, POOL_WIDTH), o_nsa.reshape(np_tok, NSA_WIDTH)],
                               [w_out[:POOL_WIDTH], w_out[POOL_WIDTH:]], row(gn[1]), xp, tm_p)
            kv5 = lambda a: a.reshape(bp, NSA_KV_HEADS, HEAD_DIM, a.shape[-1]).transpose(0, 3, 1, 2)
            n_keep = min(WINDOW, tp)
            out_ev_p.append((u3[:, -POOL_STATE:], kv5(kc), kv5(vc), kv5(ks), kv5(vs),
                             kv5(kw[:, :, -n_keep:]), kv5(vw[:, :, -n_keep:])))

            u, q, kc, vc, ks, vs, kw, vw, gate = _even_in_proj(xs, row(gn[0]), w_in, tm_s)
            as3 = lambda a: a.reshape(bs, tsp, a.shape[-1])
            u3 = as3(u)
            past_pool = jnp.pad(state_pool[e], ((0, 0), (POOL_HALO - POOL_STATE, 0), (0, 0)))
            a_out = pool_mix(u3, past_pool, pw, ps, past_len)
            paged = lambda c: c.transpose(0, 1, 3, 4, 2).reshape(n_phys * n_even, KV_COLS, PAGE_SIZE)
            o_cmp, sel = sample_cmp(page_table, paged(cache_cmp_k), paged(cache_cmp_v), as3(kc), as3(vc),
                                    as3(q), wk_bd_t, wv_bd_t, msel_s, subpool, ident_kv, e, n_even, ts)
            n_win = cache_win_k.shape[2]
            win_k = cache_win_k[e].transpose(0, 2, 3, 1).reshape(bs, KV_COLS, n_win)
            win_v = cache_win_v[e].transpose(0, 2, 3, 1).reshape(bs, KV_COLS, n_win)
            o_nsa = sample_slc(page_table, paged(cache_slc_k), paged(cache_slc_v), as3(q), sel, expand_s,
                               as3(ks), as3(vs), win_k, win_v, as3(kw), as3(vw), o_cmp, as3(gate),
                               e, n_even, ts)
            xs = proj_norm_res([a_out.reshape(ns_tok, POOL_WIDTH), o_nsa.reshape(ns_tok, NSA_WIDTH)],
                               [w_out[:POOL_WIDTH], w_out[POOL_WIDTH:]], row(gn[1]), xs, tm_s)
            new4 = lambda a: as3(a)[:, :ts].reshape(bs, ts, NSA_KV_HEADS, HEAD_DIM)
            out_ev_s.append((jnp.concatenate([state_pool[e], u3[:, :ts]], axis=1)[:, -POOL_STATE:],
                             new4(kc), new4(vc), new4(ks), new4(vs),
                             jnp.concatenate([cache_win_k[e], new4(kw)], axis=1)[:, -n_win:],
                             jnp.concatenate([cache_win_v[e], new4(vw)], axis=1)[:, -n_win:]))
        else:
            o = li // 2
            w_in = w_in_odd[o].astype(BF16)
            cw = jnp.pad(conv_w[o], ((0, 8 - CONV_WIDTH), (0, 0)))
            consts = (cw, row(conv_b[o]), lru_wa[o].astype(BF16), lru_wx[o].astype(BF16),
                      row(lru_ba[o]), row(lru_bx[o]), row(lru_lambda[o]))
            w_out = w_out_odd[o].astype(BF16)
            groups = ((0, LRU_WIDTH, ((0, LRU_WIDTH, None),)), (LRU_WIDTH, LRU_WIDTH, ((0, LRU_WIDTH, None),)))

            xb, yb = norm_matmul(xp, row(gn[0]), w_in, groups, (F32, F32), tm_p)
            xb3, yb3 = xb.reshape(bp, tp, LRU_WIDTH), yb.reshape(bp, tp, LRU_WIDTH)
            gated, tail = lru(xb3, yb3, jnp.zeros((bp, 8, LRU_WIDTH), F32), jnp.zeros((bp, 1, LRU_WIDTH), F32),
                              *consts, tt=256, first_pos_zero=True)
            xp = proj_norm_res([gated.reshape(np_tok, LRU_WIDTH)], [w_out], row(gn[1]), xp, tm_p)
            out_lru_p.append((xb3[:, -(CONV_WIDTH - 1):], tail[:, 7]))

            xb, yb = norm_matmul(xs, row(gn[0]), w_in, groups, (F32, F32), tm_s)
            xb3, yb3 = xb.reshape(bs, tsp, LRU_WIDTH), yb.reshape(bs, tsp, LRU_WIDTH)
            conv_past = jnp.pad(state_lru_conv[o], ((0, 0), (8 - (CONV_WIDTH - 1), 0), (0, 0)))
            gated, tail = lru(xb3, yb3, conv_past, state_lru_h[o][:, None, :], *consts, tt=tsp,
                              first_pos_zero=False)
            xs = proj_norm_res([gated.reshape(ns_tok, LRU_WIDTH)], [w_out], row(gn[1]), xs, tm_s)
            conv_new = jnp.concatenate([state_lru_conv[o], xb3[:, :ts]], axis=1)[:, -(CONV_WIDTH - 1):]
            out_lru_s.append((conv_new, tail[:, ts - 1]))

        wq, wo = w_xq[li].astype(BF16), w_xo[li].astype(BF16)
        wkv = jnp.concatenate([w_xk[li], w_xv[li]], axis=1).astype(BF16)
        mem2d = mem_prompt.reshape(bp * N_MEM, d)
        kv_groups = ((0, D_MODEL, ((0, D_MODEL, None),)), (D_MODEL, D_MODEL, ((0, D_MODEL, None),)))
        mk, mv = norm_matmul(mem2d, row(mem_norm_gain[li]), wkv, kv_groups, (F32, F32), min(tm_p, bp * N_MEM))
        mk3, mv3 = mk.reshape(bp, N_MEM, d), mv.reshape(bp, N_MEM, d)
        mem_k_p.append(mk3.reshape(bp, N_MEM, XATTN_HEADS, XATTN_HEAD_DIM))
        mem_v_p.append(mv3.reshape(bp, N_MEM, XATTN_HEADS, XATTN_HEAD_DIM))
        xp = _xattn_block(xp, bp, tp, mk3, mv3, row(gn[2]), row(gn[3]), wq, wo, tm_p, 512)
        xs = _xattn_block(xs, bs, tsp, cache_mem_k[li].reshape(bs, N_MEM, d), cache_mem_v[li].reshape(bs, N_MEM, d),
                          row(gn[2]), row(gn[3]), wq, wo, tm_s, tsp)

        wg, wu, wd = w_ffn_gate[li].astype(BF16), w_ffn_up[li].astype(BF16), w_ffn_down[li].astype(BF16)
        xp = ffn(xp, row(gn[4]), row(gn[5]), wg, wu, wd, tm_p)
        xs = ffn(xs, row(gn[4]), row(gn[5]), wg, wu, wd, tm_s)

    stack = lambda items, k, axis=0: jnp.stack([s[k] for s in items], axis=axis)
    y_prompt = xp.reshape(bp, tp, d)
    y_sample = xs.reshape(bs, tsp, d)[:, :ts]
    return (y_prompt, y_sample,
            stack(out_ev_p, 0), stack(out_ev_s, 0),
            stack(out_ev_p, 1, 1), stack(out_ev_s, 1, 1),
            stack(out_ev_p, 2, 1), stack(out_ev_s, 2, 1),
            stack(out_ev_p, 3, 1), stack(out_ev_s, 3, 1),
            stack(out_ev_p, 4, 1), stack(out_ev_s, 4, 1),
            stack(out_ev_p, 5), stack(out_ev_s, 5),
            stack(out_ev_p, 6), stack(out_ev_s, 6),
            stack(out_lru_p, 1), stack(out_lru_s, 1),
            stack(out_lru_p, 0), stack(out_lru_s, 0),
            jnp.stack(mem_k_p), jnp.stack(mem_v_p))
```

```python
import functools

import numpy as np
import jax
import jax.numpy as jnp
from jax import lax
from jax.experimental import pallas as pl
from jax.experimental.pallas import tpu as pltpu

F32 = jnp.float32
BF16 = jnp.bfloat16

D_MODEL = 1024
RMS_EPS = 1e-6
NEG_INF = -1e30
FORCE_SCORE = 1e30
REMOVED_SCORE = -3e38
M_INIT = -1e29

POOL_WIDTH = 512
POOL_WINDOWS = (2, 4, 8, 16)
POOL_GROUP = 128
POOL_STATE = 15
POOL_HALO = 16

HEAD_DIM = 64
NSA_HEADS = 8
NSA_KV_HEADS = 2
HEADS_PER_GROUP = NSA_HEADS // NSA_KV_HEADS
NSA_WIDTH = 512
KV_COLS = 128
CMP_BLOCK = 32
CMP_STRIDE = 16
SEL_BLOCK = 64
SEL_TOPK = 16
WINDOW = 512
PAGE_SIZE = 128
EVEN_IN = POOL_WIDTH + NSA_WIDTH + 6 * KV_COLS + 3 * NSA_HEADS
EVEN_IN_PAD = 1920

LRU_WIDTH = 1024
LRU_BLOCKS = 4
LRU_BLOCK = 256
CONV_WIDTH = 4
LRU_C = 8.0

N_MEM = 256
XATTN_HEADS = 4
XATTN_HEAD_DIM = 256
D_FF = 2816
FF_CHUNK = 256

SAMPLE_T_PAD = 8
PAGES_PER_STEP = 16
SEL_PER_STEP = PAGES_PER_STEP * PAGE_SIZE // SEL_BLOCK
SUB_PER_STEP = PAGES_PER_STEP * PAGE_SIZE // CMP_STRIDE
LANE = 128

VMEM_LIMIT_BYTES = 56 * 1024 * 1024


def _cparams(*sem):
    return pltpu.CompilerParams(dimension_semantics=sem, vmem_limit_bytes=VMEM_LIMIT_BYTES)


def _rms(x, g):
    return x * lax.rsqrt(jnp.mean(x * x, axis=-1, keepdims=True) + RMS_EPS) * g


def _dot(a, b):
    return jnp.dot(a, b, preferred_element_type=F32)


def _dot_nt(a, b):
    return lax.dot_general(a, b, (((1,), (1,)), ((), ())), preferred_element_type=F32)


def _split3(x):
    p1 = x.astype(BF16)
    r1 = x - p1.astype(F32)
    p2 = r1.astype(BF16)
    p3 = (r1 - p2.astype(F32)).astype(BF16)
    return p1, p2, p3


def _full(shape):
    n = len(shape)
    return pl.BlockSpec(shape, lambda *_: (0,) * n)


def _norm_matmul_kernel(x_ref, g_ref, w_ref, *o_refs, groups):
    h = _rms(x_ref[...], g_ref[...]).astype(BF16)
    k = 0
    for start, width, outs in groups:
        z = _dot(h, w_ref[:, start:start + width])
        for off, w, post in outs:
            v = z[:, off:off + w]
            if post is not None:
                v = post(v)
            o_refs[k][...] = v.astype(o_refs[k].dtype)
            k += 1


def norm_matmul(x, g, w, groups, out_dtypes, tm):
    n, d = x.shape
    widths = [w_ for _, _, outs in groups for _, w_, _ in outs]
    return pl.pallas_call(
        functools.partial(_norm_matmul_kernel, groups=groups),
        grid=(n // tm,),
        in_specs=[pl.BlockSpec((tm, d), lambda i: (i, 0)), _full(g.shape), _full(w.shape)],
        out_specs=[pl.BlockSpec((tm, wd), lambda i: (i, 0)) for wd in widths],
        out_shape=[jax.ShapeDtypeStruct((n, wd), dt) for wd, dt in zip(widths, out_dtypes)],
        compiler_params=_cparams("parallel"),
    )(x, g, w)


def _even_in_prompt_kernel(x_ref, g_ref, wu_ref, wk_ref, wt_ref, u_ref, kpm_ref, qt_ref, gatet_ref, *kvt_refs):
    h = _rms(x_ref[...], g_ref[...]).astype(BF16)
    u_ref[...] = _dot(h, wu_ref[...])
    kpm_ref[...] = _dot(h, wk_ref[...]).astype(kpm_ref.dtype)
    z_t = _dot_nt(wt_ref[...], h)
    qt_ref[0] = (z_t[0:NSA_WIDTH] * (HEAD_DIM ** -0.5)).astype(qt_ref.dtype)
    for k, ref in enumerate(kvt_refs):
        ref[0] = z_t[NSA_WIDTH + k * KV_COLS:NSA_WIDTH + (k + 1) * KV_COLS, :]
    gatet_ref[0] = jax.nn.sigmoid(z_t[NSA_WIDTH + 6 * KV_COLS:, :])


def even_in_prompt(x, g, w_u, w_kpm, w_t, b, t, tm):
    n, d = x.shape
    per_seq = t // tm
    row_tile = lambda w: pl.BlockSpec((tm, w), lambda i: (i, 0))
    feat_tile = lambda f: pl.BlockSpec((1, f, tm), lambda i: (i // per_seq, 0, i % per_seq))
    n_kv = 6
    return pl.pallas_call(
        _even_in_prompt_kernel,
        grid=(n // tm,),
        in_specs=[row_tile(d), _full(g.shape), _full(w_u.shape), _full(w_kpm.shape), _full(w_t.shape)],
        out_specs=([row_tile(POOL_WIDTH), row_tile(2 * KV_COLS), feat_tile(NSA_WIDTH), feat_tile(LANE)]
                   + [feat_tile(KV_COLS)] * n_kv),
        out_shape=([jax.ShapeDtypeStruct((n, POOL_WIDTH), F32), jax.ShapeDtypeStruct((n, 2 * KV_COLS), BF16),
                    jax.ShapeDtypeStruct((b, NSA_WIDTH, t), BF16), jax.ShapeDtypeStruct((b, LANE, t), F32)]
                   + [jax.ShapeDtypeStruct((b, KV_COLS, t), F32)] * n_kv),
        compiler_params=_cparams("parallel"),
    )(x, g, w_u, w_kpm, w_t)


def _proj_norm_res_kernel(*refs, n_in):
    a_refs, w_refs = refs[:n_in], refs[n_in:2 * n_in]
    g_ref, x_ref, o_ref = refs[2 * n_in:]
    acc = _dot(a_refs[0][...], w_refs[0][...])
    for a_ref, w_ref in zip(a_refs[1:], w_refs[1:]):
        acc = acc + _dot(a_ref[...], w_ref[...])
    o_ref[...] = x_ref[...] + _rms(acc, g_ref[...])


def proj_norm_res(a_list, w_list, g, x, tm):
    n, d = x.shape
    n_in = len(a_list)
    return pl.pallas_call(
        functools.partial(_proj_norm_res_kernel, n_in=n_in),
        grid=(n // tm,),
        in_specs=([pl.BlockSpec((tm, a.shape[1]), lambda i: (i, 0)) for a in a_list]
                  + [_full(w.shape) for w in w_list]
                  + [_full(g.shape), pl.BlockSpec((tm, d), lambda i: (i, 0))]),
        out_specs=pl.BlockSpec((tm, d), lambda i: (i, 0)),
        out_shape=jax.ShapeDtypeStruct((n, d), F32),
        compiler_params=_cparams("parallel"),
    )(*a_list, *w_list, g, x)


def _ffn_kernel(x_ref, g_in_ref, g_out_ref, wg_ref, wu_ref, wd_ref, o_ref):
    x = x_ref[...]
    h = _rms(x, g_in_ref[...]).astype(BF16)
    acc = None
    for c in range(D_FF // FF_CHUNK):
        cols = slice(c * FF_CHUNK, (c + 1) * FF_CHUNK)
        gate = _dot(h, wg_ref[:, cols])
        up = _dot(h, wu_ref[:, cols])
        act = (jax.nn.silu(gate) * up).astype(BF16)
        part = _dot(act, wd_ref[cols, :])
        acc = part if acc is None else acc + part
    o_ref[...] = x + _rms(acc, g_out_ref[...])


def ffn(x, g_in, g_out, wg, wu, wd, tm):
    n, d = x.shape
    resident = functools.partial(pl.BlockSpec, pipeline_mode=pl.Buffered(1))
    return pl.pallas_call(
        _ffn_kernel,
        grid=(n // tm,),
        in_specs=[pl.BlockSpec((tm, d), lambda i: (i, 0)), _full(g_in.shape), _full(g_out.shape),
                  resident(wg.shape, lambda i: (0, 0)), resident(wu.shape, lambda i: (0, 0)),
                  resident(wd.shape, lambda i: (0, 0))],
        out_specs=pl.BlockSpec((tm, d), lambda i: (i, 0)),
        out_shape=jax.ShapeDtypeStruct((n, d), F32),
        compiler_params=_cparams("parallel"),
    )(x, g_in, g_out, wg, wu, wd)


def _xattn_kernel(q_ref, mk_ref, mv_ref, o_ref):
    outs = []
    for hd in range(XATTN_HEADS):
        cols = slice(hd * XATTN_HEAD_DIM, (hd + 1) * XATTN_HEAD_DIM)
        k = mk_ref[0, :, cols].astype(BF16)
        v = mv_ref[0, :, cols].astype(BF16)
        s = _dot_nt(q_ref[0, :, cols], k)
        e = jnp.exp(s - jnp.max(s, axis=-1, keepdims=True))
        p = e / jnp.sum(e, axis=-1, keepdims=True)
        outs.append(_dot(p.astype(BF16), v).astype(BF16))
    o_ref[0] = jnp.concatenate(outs, axis=-1)


def xattn(q, mk, mv, tq):
    b, t, d = q.shape
    return pl.pallas_call(
        _xattn_kernel,
        grid=(b, t // tq),
        in_specs=[pl.BlockSpec((1, tq, d), lambda i, j: (i, j, 0)),
                  pl.BlockSpec((1, N_MEM, d), lambda i, j: (i, 0, 0)),
                  pl.BlockSpec((1, N_MEM, d), lambda i, j: (i, 0, 0))],
        out_specs=pl.BlockSpec((1, tq, d), lambda i, j: (i, j, 0)),
        out_shape=jax.ShapeDtypeStruct((b, t, d), BF16),
        compiler_params=_cparams("parallel", "parallel"),
    )(q, mk, mv)


def _pool_kernel(u_ref, past_ref, w_ref, scale_ref, o_ref, buf, *, t, chunk, start_pos):
    buf[0:POOL_HALO, :] = past_ref[0]
    buf[POOL_HALO:POOL_HALO + t, :] = u_ref[0]
    for c in range(t // chunk):
        base = POOL_HALO + c * chunk
        pos = start_pos + c * chunk + lax.broadcasted_iota(jnp.int32, (chunk, 1), 0)
        for gi, win in enumerate(POOL_WINDOWS):
            cols = slice(gi * POOL_GROUP, (gi + 1) * POOL_GROUP)
            cur = buf[base:base + chunk, cols]
            win_sum = cur
            for k in range(1, win):
                win_sum = win_sum + buf[base - k:base - k + chunk, cols]
            cnt = jnp.minimum(pos + 1, win).astype(F32)
            diff = win_sum / cnt - cur
            y = _dot(diff.astype(BF16), w_ref[gi]) * scale_ref[:, cols]
            o_ref[0, c * chunk:(c + 1) * chunk, cols] = y.astype(o_ref.dtype)


def pool_mix(u, past, w, scale, start_pos):
    b, t, c = u.shape
    chunk = min(t, 256)
    return pl.pallas_call(
        functools.partial(_pool_kernel, t=t, chunk=chunk, start_pos=start_pos),
        grid=(b,),
        in_specs=[pl.BlockSpec((1, t, c), lambda i: (i, 0, 0)),
                  pl.BlockSpec((1, POOL_HALO, c), lambda i: (i, 0, 0)),
                  _full(w.shape), _full(scale.shape)],
        out_specs=pl.BlockSpec((1, t, c), lambda i: (i, 0, 0)),
        out_shape=jax.ShapeDtypeStruct((b, t, c), BF16),
        scratch_shapes=[pltpu.VMEM((POOL_HALO + t, c), F32)],
        compiler_params=_cparams("parallel"),
    )(u, past, w, scale)


def _compress_kernel(kc_ref, vc_ref, pool_ref, wkt_ref, wvt_ref, ident_ref, k_ref, vt_ref):
    def compress(src, wt_ref):
        x = src[0]
        hi = x.astype(BF16)
        lo = (x - hi.astype(F32)).astype(BF16)
        mean_t = _dot(hi, pool_ref[...]) + _dot(lo, pool_ref[...])
        return _dot(wt_ref[...], mean_t.astype(BF16)).astype(BF16)
    k_ref[0] = _dot_nt(ident_ref[...], compress(kc_ref, wkt_ref)).astype(BF16)
    vt_ref[0] = compress(vc_ref, wvt_ref)


def compress_kv(kc_t, vc_t, pool_mat, wk_bd_t, wv_bd_t, ident):
    b, c, t = kc_t.shape
    n_cmp = pool_mat.shape[1]
    per_b = pl.BlockSpec((1, c, t), lambda i: (i, 0, 0))
    return pl.pallas_call(
        _compress_kernel,
        grid=(b,),
        in_specs=[per_b, per_b, _full(pool_mat.shape), _full(wk_bd_t.shape), _full(wv_bd_t.shape),
                  _full(ident.shape)],
        out_specs=[pl.BlockSpec((1, n_cmp, c), lambda i: (i, 0, 0)), pl.BlockSpec((1, c, n_cmp), lambda i: (i, 0, 0))],
        out_shape=[jax.ShapeDtypeStruct((b, n_cmp, c), BF16), jax.ShapeDtypeStruct((b, c, n_cmp), BF16)],
        compiler_params=_cparams("parallel"),
    )(kc_t, vc_t, pool_mat, wk_bd_t, wv_bd_t, ident)


def _masked_softmax_rows(s, mask):
    s = jnp.where(mask, s, NEG_INF)
    e = jnp.where(mask, jnp.exp(s - jnp.max(s, axis=-1, keepdims=True)), 0.0)
    return e / jnp.maximum(jnp.sum(e, axis=-1, keepdims=True), 1e-30)


def _flash_step(q_pad, k, v_t, mask, carry, tq):
    m, l, acc = carry
    bias = jnp.where(mask, 0.0, NEG_INF)
    s = _dot(k, q_pad) + jnp.concatenate([bias] * HEADS_PER_GROUP, axis=1)
    m_new = jnp.maximum(m, jnp.max(s, axis=0, keepdims=True))
    alpha = jnp.exp(m - m_new)
    e = jnp.exp(s - m_new)
    l = alpha * l + jnp.sum(e, axis=0, keepdims=True)
    acc = alpha * acc + _dot(v_t, e.astype(BF16))
    return m_new, l, acc


def _nsa_prompt_kernel(qt_ref, gatet_ref, kcmp_ref, vcmpt_ref, kpm_ref, vst_ref, vwt_ref,
                       mselt_ref, ident_ref, o_ref, sel_s, *, tq, tk):
    i = pl.program_id(1)
    q0 = i * tq
    n_cmp = kcmp_ref.shape[1]
    n_sel = mselt_ref.shape[0]
    blocks_per_tile = tk // SEL_BLOCK
    tiles_per_q = tq // tk
    qpos_row = q0 + lax.broadcasted_iota(jnp.int32, (1, tq), 1)
    key_col = lax.broadcasted_iota(jnp.int32, (tk, 1), 0)
    cmp_end_col = lax.broadcasted_iota(jnp.int32, (n_cmp, 1), 0) * CMP_STRIDE + (CMP_BLOCK - 1)
    cmp_bias = jnp.where(cmp_end_col <= qpos_row, 0.0, NEG_INF)
    sel_j = lax.broadcasted_iota(jnp.int32, (n_sel, 1), 0)
    cur_row = qpos_row // SEL_BLOCK
    gate_t = gatet_ref[0]
    zero_half = jnp.zeros((HEAD_DIM, HEADS_PER_GROUP * tq), BF16)

    groups = range(NSA_KV_HEADS)
    grows = [slice(g * HEAD_DIM, (g + 1) * HEAD_DIM) for g in groups]
    q_pads, o_cmps = [], []
    for g in groups:
        q_t_g = jnp.concatenate(
            [qt_ref[0, (g * HEADS_PER_GROUP + hh) * HEAD_DIM:(g * HEADS_PER_GROUP + hh + 1) * HEAD_DIM, :]
             for hh in range(HEADS_PER_GROUP)], axis=1)
        q_pad = jnp.concatenate([q_t_g, zero_half] if g == 0 else [zero_half, q_t_g], axis=0)
        q_pads.append(q_pad)

        s_t = _dot(kcmp_ref[0], q_pad) + jnp.concatenate([cmp_bias] * HEADS_PER_GROUP, axis=1)
        e_t = jnp.exp(s_t - jnp.maximum(jnp.max(s_t, axis=0, keepdims=True), M_INIT))
        p_t = e_t / jnp.maximum(jnp.sum(e_t, axis=0, keepdims=True), 1e-30)
        o_cmps.append(_dot(vcmpt_ref[0, grows[g], :], p_t.astype(BF16)))
        psum_t = p_t[:, 0:tq]
        for hh in range(1, HEADS_PER_GROUP):
            psum_t = psum_t + p_t[:, hh * tq:(hh + 1) * tq]

        imp_t = None
        for part in _split3(psum_t):
            d = _dot(mselt_ref[...], part)
            imp_t = d if imp_t is None else imp_t + d
        valid = sel_j * SEL_BLOCK <= qpos_row
        forced = (sel_j == 0) | (sel_j == cur_row) | (sel_j == cur_row - 1)
        score = jnp.where(forced, FORCE_SCORE, jnp.where(valid, imp_t, NEG_INF))
        rank = jnp.zeros((n_sel, tq), F32)
        for jp in range(n_sel):
            sj = score[jp:jp + 1, :]
            beats = (sj > score) | ((sj == score) & (sel_j > jp))
            rank = rank + jnp.where(beats, 1.0, 0.0)
        sel_s[g] = jnp.where(rank < SEL_TOPK, 1.0, 0.0)

    init = (jnp.full((1, HEADS_PER_GROUP * tq), M_INIT, F32),
            jnp.zeros((1, HEADS_PER_GROUP * tq), F32),
            jnp.zeros((HEAD_DIM, HEADS_PER_GROUP * tq), F32))

    def slc_steps(kt, keys, carries):
        k = kpm_ref[0, keys, 0:KV_COLS]
        causal = kt * tk + key_col <= qpos_row
        out = []
        for g in groups:
            chosen = jnp.concatenate(
                [jnp.broadcast_to(sel_s[g, pl.ds(kt * blocks_per_tile + r, 1), :], (SEL_BLOCK, tq))
                 for r in range(blocks_per_tile)], axis=0)
            v_t = vst_ref[0, grows[g], keys].astype(BF16)
            out.append(_flash_step(q_pads[g], k, v_t, (chosen > 0.5) & causal, carries[g], tq))
        return tuple(out)

    def win_steps(kt, keys, carries):
        k = kpm_ref[0, keys, KV_COLS:2 * KV_COLS]
        dist = qpos_row - (kt * tk + key_col)
        mask = (dist >= 0) & (dist < WINDOW)
        return tuple(_flash_step(q_pads[g], k, vwt_ref[0, grows[g], keys].astype(BF16), mask, carries[g], tq)
                     for g in groups)

    def far_body(kt, slc):
        keys = pl.ds(pl.multiple_of(kt * tk, tk), tk)
        return slc_steps(kt, keys, slc)

    def band_body(kt, carry):
        slc, win = carry
        keys = pl.ds(pl.multiple_of(kt * tk, tk), tk)
        return slc_steps(kt, keys, slc), win_steps(kt, keys, win)

    first_win = jnp.maximum((i * tq - (WINDOW - 1)) // tk, 0)
    slc = lax.fori_loop(0, first_win, far_body, (init,) * NSA_KV_HEADS)
    slc, win = lax.fori_loop(first_win, (i + 1) * tiles_per_q, band_body, (slc, (init,) * NSA_KV_HEADS))

    head_out = [None] * NSA_HEADS
    for g in groups:
        o_slc = slc[g][2] / slc[g][1]
        o_win = win[g][2] / win[g][1]
        for hh in range(HEADS_PER_GROUP):
            h = g * HEADS_PER_GROUP + hh
            lanes = slice(hh * tq, (hh + 1) * tq)
            head_out[h] = (gate_t[3 * h:3 * h + 1, :] * o_cmps[g][:, lanes]
                           + gate_t[3 * h + 1:3 * h + 2, :] * o_slc[:, lanes]
                           + gate_t[3 * h + 2:3 * h + 3, :] * o_win[:, lanes])
    o_t = jnp.concatenate(head_out, axis=0).astype(BF16)
    o_ref[0] = _dot_nt(ident_ref[...], o_t).astype(o_ref.dtype)


def nsa_prompt(q_t, gate_t, kcmp, vcmp_t, kpm, vs_t, vw_t, mselt, ident, tq, tk):
    b, _, t = q_t.shape
    feat_tile = lambda f: pl.BlockSpec((1, f, tq), lambda i, j: (i, 0, j))
    whole = lambda a: pl.BlockSpec((1,) + a.shape[1:], lambda i, j: (i, 0, 0))
    per_batch = [kcmp, vcmp_t, kpm, vs_t, vw_t]
    return pl.pallas_call(
        functools.partial(_nsa_prompt_kernel, tq=tq, tk=tk),
        grid=(b, t // tq),
        in_specs=([feat_tile(NSA_WIDTH), feat_tile(LANE)] + [whole(a) for a in per_batch]
                  + [_full(mselt.shape), _full(ident.shape)]),
        out_specs=pl.BlockSpec((1, tq, NSA_WIDTH), lambda i, j: (i, j, 0)),
        out_shape=jax.ShapeDtypeStruct((b, t, NSA_WIDTH), BF16),
        scratch_shapes=[pltpu.VMEM((NSA_KV_HEADS, mselt.shape[0], tq), F32)],
        compiler_params=_cparams("parallel", "arbitrary"),
    )(q_t, gate_t, *per_batch, mselt, ident)


def _stack_heads(q, g):
    return jnp.concatenate(
        [q[:, (g * HEADS_PER_GROUP + hh) * HEAD_DIM:(g * HEADS_PER_GROUP + hh + 1) * HEAD_DIM]
         for hh in range(HEADS_PER_GROUP)], axis=0)


def _sample_cmp_kernel(pt_ref, *refs, n_valid, past_len):
    del pt_ref
    p = PAGES_PER_STEP
    tp = SAMPLE_T_PAD
    k_pages, v_pages = refs[:p], refs[p:2 * p]
    (kn_ref, vn_ref, q_ref, wkt_ref, wvt_ref, msel_ref, subpool_ref, ident_ref,
     ocmp_ref, sel_ref, sub_k, sub_v) = refs[2 * p:]
    c = pl.program_id(1)
    n_sub = past_len // CMP_STRIDE
    steps = past_len // (p * PAGE_SIZE)

    for pages, sub in ((k_pages, sub_k), (v_pages, sub_v)):
        x = jnp.concatenate([r[...] for r in pages], axis=1)
        hi = x.astype(BF16)
        lo = (x - hi.astype(F32)).astype(BF16)
        sub[c] = _dot(hi, subpool_ref[...]) + _dot(lo, subpool_ref[...])

    @pl.when(c == steps - 1)
    def _():
        t_col = lax.broadcasted_iota(jnp.int32, (tp, 1), 0)
        new_valid = lax.broadcasted_iota(jnp.int32, (1, tp), 1) < n_valid
        first_lane = lax.broadcasted_iota(jnp.int32, (1, SUB_PER_STEP), 1) == 0
        cmp_kv = []
        for new_ref, sub, wt_ref in ((kn_ref, sub_k, wkt_ref), (vn_ref, sub_v, wvt_ref)):
            new_t = None
            for part in _split3(new_ref[0]):
                d = _dot_nt(ident_ref[...], part)
                new_t = d if new_t is None else new_t + d
            new_sum = jnp.sum(jnp.where(new_valid, new_t, 0.0), axis=1, keepdims=True)
            sub[steps] = jnp.where(first_lane, new_sum, 0.0)
            sums = jnp.concatenate([sub[s] for s in range(steps + 1)], axis=1)
            mean_t = (sums[:, 0:n_sub] + sums[:, 1:n_sub + 1]) * (1.0 / CMP_BLOCK)
            cmp_kv.append(_dot(wt_ref[...], mean_t.astype(BF16)).astype(BF16))
        kcmp_t, vcmp_t = cmp_kv

        rows = HEADS_PER_GROUP * tp
        qpos_col = past_len + lax.broadcasted_iota(jnp.int32, (rows, 1), 0) % tp
        cmp_end_row = lax.broadcasted_iota(jnp.int32, (1, n_sub), 1) * CMP_STRIDE + (CMP_BLOCK - 1)
        cmp_mask = cmp_end_row <= qpos_col

        n_lane = sel_ref.shape[2]
        lane = lax.broadcasted_iota(jnp.int32, (1, n_lane), 1)
        sel_j = (lane // LANE) * SEL_PER_STEP + lane % LANE
        lane_used = lane % LANE < SEL_PER_STEP
        qpos_t = past_len + t_col
        cur = qpos_t // SEL_BLOCK
        valid = lane_used & (sel_j * SEL_BLOCK <= qpos_t)
        forced = lane_used & ((sel_j == 0) | (sel_j == cur) | (sel_j == cur - 1))

        q = q_ref[0]
        o_heads = [None] * NSA_HEADS
        scores = []
        for g in range(NSA_KV_HEADS):
            gcols = slice(g * HEAD_DIM, (g + 1) * HEAD_DIM)
            prob = _masked_softmax_rows(_dot(_stack_heads(q, g), kcmp_t[gcols, :]), cmp_mask)
            o_g = _dot_nt(prob.astype(BF16), vcmp_t[gcols, :])
            psum = prob[0:tp]
            for hh in range(1, HEADS_PER_GROUP):
                psum = psum + prob[hh * tp:(hh + 1) * tp]
                o_heads[g * HEADS_PER_GROUP + hh] = o_g[hh * tp:(hh + 1) * tp]
            o_heads[g * HEADS_PER_GROUP] = o_g[0:tp]
            imp = None
            for part in _split3(psum):
                d = _dot(part, msel_ref[...])
                imp = d if imp is None else imp + d
            scores.append(jnp.where(forced, FORCE_SCORE, jnp.where(valid, imp, NEG_INF)))
        score = jnp.concatenate(scores, axis=0)
        chosen = jnp.zeros(score.shape, F32)
        for _ in range(SEL_TOPK):
            best = jnp.max(score, axis=-1, keepdims=True)
            first = jnp.min(jnp.where(score == best, lane, n_lane), axis=-1, keepdims=True)
            hit = lane == first
            chosen = jnp.where(hit, 1.0, chosen)
            score = jnp.where(hit, REMOVED_SCORE, score)
        sel_ref[0] = chosen
        ocmp_ref[0] = jnp.concatenate(o_heads, axis=-1)


def _page_spec(k, layer, n_layers):
    return pl.BlockSpec((None, KV_COLS, PAGE_SIZE),
                        lambda i, c, pt: (pt[i, c * PAGES_PER_STEP + k] * n_layers + layer, 0, 0))


def sample_cmp(page_table, cache_k, cache_v, kc_new, vc_new, q, wk_bd_t, wv_bd_t, msel, subpool, ident,
               layer, n_layers, n_valid):
    b, n_pages = page_table.shape
    past_len = n_pages * PAGE_SIZE
    steps = n_pages // PAGES_PER_STEP
    n_lane = msel.shape[1]
    per_b = lambda a: pl.BlockSpec((1,) + a.shape[1:], lambda i, c, pt: (i, 0, 0))
    const = lambda a: pl.BlockSpec(a.shape, lambda i, c, pt: (0,) * a.ndim)
    consts = [wk_bd_t, wv_bd_t, msel, subpool, ident]
    grid_spec = pltpu.PrefetchScalarGridSpec(
        num_scalar_prefetch=1,
        grid=(b, steps),
        in_specs=([_page_spec(k, layer, n_layers) for k in range(PAGES_PER_STEP)] * 2
                  + [per_b(kc_new), per_b(vc_new), per_b(q)] + [const(a) for a in consts]),
        out_specs=[pl.BlockSpec((1, SAMPLE_T_PAD, NSA_WIDTH), lambda i, c, pt: (i, 0, 0)),
                   pl.BlockSpec((1, 2 * SAMPLE_T_PAD, n_lane), lambda i, c, pt: (i, 0, 0))],
        scratch_shapes=[pltpu.VMEM((steps + 1, KV_COLS, SUB_PER_STEP), F32)] * 2,
    )
    return pl.pallas_call(
        functools.partial(_sample_cmp_kernel, n_valid=n_valid, past_len=past_len),
        grid_spec=grid_spec,
        out_shape=[jax.ShapeDtypeStruct((b, SAMPLE_T_PAD, NSA_WIDTH), F32),
                   jax.ShapeDtypeStruct((b, 2 * SAMPLE_T_PAD, n_lane), F32)],
        compiler_params=_cparams("parallel", "arbitrary"),
    )(page_table, *([cache_k] * PAGES_PER_STEP), *([cache_v] * PAGES_PER_STEP),
      kc_new, vc_new, q, *consts)


def _sample_slc_kernel(pt_ref, *refs, n_valid, past_len):
    del pt_ref
    p = PAGES_PER_STEP
    tp = SAMPLE_T_PAD
    rows = HEADS_PER_GROUP * tp
    k_pages, v_pages = refs[:p], refs[p:2 * p]
    (q_ref, sel_ref, expand_ref, ksn_ref, vsn_ref, wk_ref, wv_ref, kwn_ref, vwn_ref, ocmp_ref, gate_ref,
     o_ref, m_s, l_s, acc_s) = refs[2 * p:]
    c = pl.program_id(1)
    q = q_ref[0]
    zero = jnp.zeros((rows, HEAD_DIM), BF16)
    q_bd = jnp.concatenate([jnp.concatenate([_stack_heads(q, 0), zero], axis=1),
                            jnp.concatenate([zero, _stack_heads(q, 1)], axis=1)], axis=0)
    t_col = lax.broadcasted_iota(jnp.int32, (NSA_KV_HEADS * rows, 1), 0) % tp
    new_row = lax.broadcasted_iota(jnp.int32, (1, tp), 1)
    new_mask = (new_row <= t_col) & (new_row < n_valid)

    @pl.when(c == 0)
    def _():
        m_s[...] = jnp.full(m_s.shape, NEG_INF, F32)
        l_s[...] = jnp.zeros(l_s.shape, F32)
        acc_s[...] = jnp.zeros(acc_s.shape, F32)

    k_all_t = jnp.concatenate([r[...] for r in k_pages], axis=1).astype(BF16)
    v_all_t = jnp.concatenate([r[...] for r in v_pages], axis=1).astype(BF16)
    chosen = _dot(sel_ref[0].astype(BF16), expand_ref[...])
    keymask = jnp.concatenate([chosen[0:tp]] * HEADS_PER_GROUP + [chosen[tp:2 * tp]] * HEADS_PER_GROUP,
                              axis=0) > 0.5
    s = jnp.where(keymask, _dot(q_bd, k_all_t), NEG_INF)
    m_old = m_s[...]
    m_new = jnp.maximum(m_old, jnp.max(s, axis=-1, keepdims=True))
    alpha = jnp.exp(m_old - m_new)
    e = jnp.where(keymask, jnp.exp(s - m_new), 0.0)
    l_s[...] = alpha * l_s[...] + jnp.sum(e, axis=-1, keepdims=True)
    acc_s[...] = alpha * acc_s[...] + _dot_nt(e.astype(BF16), v_all_t)
    m_s[...] = m_new

    @pl.when(c == pl.num_programs(1) - 1)
    def _():
        n_win = wk_ref.shape[2]
        win_row = lax.broadcasted_iota(jnp.int32, (1, n_win), 1)
        dist = t_col + n_win - win_row
        win_mask = (dist >= 0) & (dist < WINDOW)
        s_n = jnp.where(new_mask, _dot_nt(q_bd, ksn_ref[0].astype(BF16)), NEG_INF)
        m_fin = jnp.maximum(m_new, jnp.max(s_n, axis=-1, keepdims=True))
        a_fin = jnp.exp(m_new - m_fin)
        e_n = jnp.where(new_mask, jnp.exp(s_n - m_fin), 0.0)
        l_fin = a_fin * l_s[...] + jnp.sum(e_n, axis=-1, keepdims=True)
        o_slc = (a_fin * acc_s[...] + _dot(e_n.astype(BF16), vsn_ref[0].astype(BF16))) / l_fin
        s_p = jnp.where(win_mask, _dot(q_bd, wk_ref[0].astype(BF16)), NEG_INF)
        s_w = jnp.where(new_mask, _dot_nt(q_bd, kwn_ref[0].astype(BF16)), NEG_INF)
        m_w = jnp.maximum(jnp.max(s_p, axis=-1, keepdims=True), jnp.max(s_w, axis=-1, keepdims=True))
        e_p = jnp.where(win_mask, jnp.exp(s_p - m_w), 0.0)
        e_w = jnp.where(new_mask, jnp.exp(s_w - m_w), 0.0)
        l_w = jnp.sum(e_p, axis=-1, keepdims=True) + jnp.sum(e_w, axis=-1, keepdims=True)
        o_win = (_dot_nt(e_p.astype(BF16), wv_ref[0].astype(BF16))
                 + _dot(e_w.astype(BF16), vwn_ref[0].astype(BF16))) / l_w
        gate_all = gate_ref[0]
        head_out = []
        for h in range(NSA_HEADS):
            g, hh = divmod(h, HEADS_PER_GROUP)
            r = slice(g * rows + hh * tp, g * rows + (hh + 1) * tp)
            cols = slice(g * HEAD_DIM, (g + 1) * HEAD_DIM)
            gate = gate_all[:, 3 * h:3 * h + 3]
            head_out.append(gate[:, 0:1] * ocmp_ref[0, :, h * HEAD_DIM:(h + 1) * HEAD_DIM]
                            + gate[:, 1:2] * o_slc[r, cols] + gate[:, 2:3] * o_win[r, cols])
        o_ref[0] = jnp.concatenate(head_out, axis=-1).astype(o_ref.dtype)


def sample_slc(page_table, cache_k, cache_v, q, sel, expand, ks_new, vs_new, win_k, win_v, kw_new, vw_new,
               o_cmp, gate, layer, n_layers, n_valid):
    b, n_pages = page_table.shape
    past_len = n_pages * PAGE_SIZE
    steps = n_pages // PAGES_PER_STEP
    rows = HEADS_PER_GROUP * SAMPLE_T_PAD
    per_b = lambda a: pl.BlockSpec((1,) + a.shape[1:], lambda i, c, pt: (i, 0, 0))
    const = lambda a: pl.BlockSpec(a.shape, lambda i, c, pt: (0,) * a.ndim)
    grid_spec = pltpu.PrefetchScalarGridSpec(
        num_scalar_prefetch=1,
        grid=(b, steps),
        in_specs=([_page_spec(k, layer, n_layers) for k in range(PAGES_PER_STEP)] * 2
                  + [per_b(q), pl.BlockSpec((1, 2 * SAMPLE_T_PAD, LANE), lambda i, c, pt: (i, 0, c)),
                     const(expand), per_b(ks_new), per_b(vs_new), per_b(win_k), per_b(win_v),
                     per_b(kw_new), per_b(vw_new), per_b(o_cmp), per_b(gate)]),
        out_specs=pl.BlockSpec((1, SAMPLE_T_PAD, NSA_WIDTH), lambda i, c, pt: (i, 0, 0)),
        scratch_shapes=[pltpu.VMEM((NSA_KV_HEADS * rows, 1), F32), pltpu.VMEM((NSA_KV_HEADS * rows, 1), F32),
                        pltpu.VMEM((NSA_KV_HEADS * rows, KV_COLS), F32)],
    )
    return pl.pallas_call(
        functools.partial(_sample_slc_kernel, n_valid=n_valid, past_len=past_len),
        grid_spec=grid_spec,
        out_shape=jax.ShapeDtypeStruct((b, SAMPLE_T_PAD, NSA_WIDTH), BF16),
        compiler_params=_cparams("parallel", "arbitrary"),
    )(page_table, *([cache_k] * PAGES_PER_STEP), *([cache_v] * PAGES_PER_STEP),
      q, sel, expand, ks_new, vs_new, win_k, win_v, kw_new, vw_new, o_cmp, gate)


def _lru_kernel(xb_ref, yb_ref, cpast_ref, h0_ref, cw_ref, cb_ref, wa_ref, wx_ref, ba_ref, bx_ref, lam_ref,
                o_ref, tail_ref, xbuf, h_carry, a_s, d_s, *, tt, first_pos_zero):
    j = pl.program_id(1)
    halo = 8

    @pl.when(j == 0)
    def _():
        xbuf[0:halo, :] = cpast_ref[0]
        h_carry[...] = h0_ref[0]

    xbuf[halo:halo + tt, :] = xb_ref[0]
    xc = xbuf[halo - 3:halo - 3 + tt, :] * cw_ref[0:1, :]
    for k in range(1, CONV_WIDTH):
        xc = xc + xbuf[halo - 3 + k:halo - 3 + k + tt, :] * cw_ref[k:k + 1, :]
    xf = xc + cb_ref[...]
    xf_b = xf.astype(BF16)
    r_parts, i_parts = [], []
    for n in range(LRU_BLOCKS):
        cols = slice(n * LRU_BLOCK, (n + 1) * LRU_BLOCK)
        r_parts.append(_dot(xf_b[:, cols], wa_ref[n]))
        i_parts.append(_dot(xf_b[:, cols], wx_ref[n]))
    r = jax.nn.sigmoid(jnp.concatenate(r_parts, axis=-1) + ba_ref[...])
    gate_i = jax.nn.sigmoid(jnp.concatenate(i_parts, axis=-1) + bx_ref[...])
    neg_lam = -lam_ref[...]
    softplus = jnp.maximum(neg_lam, 0.0) + jnp.log1p(jnp.exp(-jnp.abs(neg_lam)))
    log_a = -LRU_C * r * softplus
    a = jnp.exp(log_a)
    mult = jnp.sqrt(-jnp.tanh(log_a) * (a * a + 1.0))
    if first_pos_zero:
        row = lax.broadcasted_iota(jnp.int32, (tt, 1), 0)
        mult = jnp.where((row == 0) & (j == 0), 1.0, mult)
    a_s[...] = a
    d_s[...] = mult * gate_i * xf

    sub = lax.broadcasted_iota(jnp.int32, (8, LRU_WIDTH), 0)

    def body(blk, h):
        rows = pl.ds(pl.multiple_of(blk * 8, 8), 8)
        a8 = a_s[rows, :]
        d8 = d_s[rows, :]
        for s in (1, 2, 4):
            keep = sub >= s
            d8 = jnp.where(keep, a8 * pltpu.roll(d8, s, 0) + d8, d8)
            a8 = jnp.where(keep, a8 * pltpu.roll(a8, s, 0), a8)
        hs = a8 * h + d8
        d_s[rows, :] = hs
        return hs[7:8, :]

    h_carry[...] = lax.fori_loop(0, tt // 8, body, h_carry[...])
    hs = d_s[...]
    o_ref[0] = (hs * jax.nn.gelu(yb_ref[0])).astype(o_ref.dtype)
    tail_ref[0] = d_s[tt - 8:tt, :]
    xbuf[0:halo, :] = xbuf[tt:tt + halo, :]


def lru(xb, yb, conv_past, h0, cw, cb, wa, wx, ba, bx, lam, tt, first_pos_zero):
    b, t, w = xb.shape
    tile = pl.BlockSpec((1, tt, w), lambda i, j: (i, j, 0))
    per_b = lambda a: pl.BlockSpec((1,) + a.shape[1:], lambda i, j: (i, 0, 0))
    consts = [cw, cb, wa, wx, ba, bx, lam]
    return pl.pallas_call(
        functools.partial(_lru_kernel, tt=tt, first_pos_zero=first_pos_zero),
        grid=(b, t // tt),
        in_specs=[tile, tile, per_b(conv_past), per_b(h0)] + [_full(a.shape) for a in consts],
        out_specs=[tile, pl.BlockSpec((1, 8, w), lambda i, j: (i, 0, 0))],
        out_shape=[jax.ShapeDtypeStruct((b, t, w), BF16), jax.ShapeDtypeStruct((b, 8, w), F32)],
        scratch_shapes=[pltpu.VMEM((8 + tt, w), F32), pltpu.VMEM((1, w), F32),
                        pltpu.VMEM((tt, w), F32), pltpu.VMEM((tt, w), F32)],
        compiler_params=_cparams("parallel", "arbitrary"),
    )(xb, yb, conv_past, h0, *consts)


def _block_mean_matrix(t):
    n_cmp = t // CMP_STRIDE - CMP_BLOCK // CMP_STRIDE + 1
    m = np.zeros((t // CMP_STRIDE, t), np.float32)
    for i in range(n_cmp):
        m[i, i * CMP_STRIDE:i * CMP_STRIDE + CMP_BLOCK] = 1.0 / CMP_BLOCK
    return m


def _sel_from_cmp(n_cmp_rows, n_cmp, n_sel):
    m = np.zeros((n_cmp_rows, n_sel), np.float32)
    for j in range(n_sel):
        for c in range(4 * j - 1, 4 * j + 4):
            if 0 <= c < n_cmp:
                m[c, j] = 1.0
    return m


def _block_diag(w):
    z = jnp.zeros((HEAD_DIM, HEAD_DIM), w.dtype)
    return jnp.concatenate([jnp.concatenate([w[0], z], axis=1), jnp.concatenate([z, w[1]], axis=1)], axis=0)


def _sigmoid(v):
    return jax.nn.sigmoid(v)


def _scale_q(v):
    return v * (HEAD_DIM ** -0.5)


_EVEN_GROUPS = (
    (0, POOL_WIDTH, ((0, POOL_WIDTH, None),)),
    (POOL_WIDTH, NSA_WIDTH, ((0, NSA_WIDTH, _scale_q),)),
    (POOL_WIDTH + NSA_WIDTH, 6 * KV_COLS, tuple((k * KV_COLS, KV_COLS, None) for k in range(6))),
    (POOL_WIDTH + NSA_WIDTH + 6 * KV_COLS, LANE, ((0, LANE, _sigmoid),)),
)
_EVEN_DTYPES = (F32, BF16) + (F32,) * 6 + (F32,)


def _even_in_proj(x2d, g, w_in_pad, tm):
    return norm_matmul(x2d, g, w_in_pad, _EVEN_GROUPS, _EVEN_DTYPES, tm)


def _xattn_block(x2d, b, t, mk, mv, g_pre, g_post, wq, wo, tm, tq):
    q, = norm_matmul(x2d, g_pre, wq, ((0, D_MODEL, ((0, D_MODEL, lambda v: v * (XATTN_HEAD_DIM ** -0.5)),)),),
                     (BF16,), tm)
    o = xattn(q.reshape(b, t, D_MODEL), mk, mv, tq).reshape(b * t, D_MODEL)
    return proj_norm_res([o], [wo], g_post, x2d, tm)


def kernel(x_prompt, mem_prompt, x_sample, state_pool, cache_cmp_k, cache_cmp_v, cache_slc_k, cache_slc_v, cache_win_k, cache_win_v, state_lru_h, state_lru_conv, cache_mem_k, cache_mem_v, page_table, norm_gain, mem_norm_gain, w_in_even, pool_w, pool_scale, w_cmp_k, w_cmp_v, w_out_even, w_in_odd, conv_w, conv_b, lru_wa, lru_ba, lru_wx, lru_bx, lru_lambda, w_out_odd, w_xq, w_xk, w_xv, w_xo, w_ffn_gate, w_ffn_up, w_ffn_down):
    bp, tp, d = x_prompt.shape
    bs, ts, _ = x_sample.shape
    depth = norm_gain.shape[0]
    n_even = w_in_even.shape[0]
    n_pages = page_table.shape[1]
    past_len = n_pages * PAGE_SIZE
    n_phys = cache_cmp_k.shape[0]
    tsp = SAMPLE_T_PAD
    np_tok, ns_tok = bp * tp, bs * tsp
    tm_p, tm_s = 512, ns_tok
    tq, tk = 256, 128

    xp = x_prompt.reshape(np_tok, d)
    xs = jnp.pad(x_sample, ((0, 0), (0, tsp - ts), (0, 0))).reshape(ns_tok, d)

    n_sub_p = tp // CMP_STRIDE
    n_cmp_p = n_sub_p - 1
    n_sel_p = tp // SEL_BLOCK
    pool_mat = jnp.asarray(_block_mean_matrix(tp).T, BF16)
    mselt = jnp.asarray(_sel_from_cmp(n_sub_p, n_cmp_p, n_sel_p).T, BF16)
    ident = jnp.asarray(np.eye(tq, dtype=np.float32), BF16)
    ident_kv = jnp.asarray(np.eye(KV_COLS, dtype=np.float32), BF16)
    subpool = jnp.asarray(np.repeat(np.eye(SUB_PER_STEP, dtype=np.float32), CMP_STRIDE, axis=0), BF16)
    steps = n_pages // PAGES_PER_STEP
    n_sub_s = past_len // CMP_STRIDE
    n_sel_s = -(-(past_len + ts) // SEL_BLOCK)
    n_lane_s = (steps + 1) * LANE
    msel_np = np.zeros((n_sub_s, n_lane_s), np.float32)
    dense = _sel_from_cmp(n_sub_s, n_sub_s, n_sel_s)
    for j in range(n_sel_s):
        msel_np[:, (j // SEL_PER_STEP) * LANE + j % SEL_PER_STEP] = dense[:, j]
    msel_s = jnp.asarray(msel_np, BF16)
    expand_np = np.zeros((LANE, PAGES_PER_STEP * PAGE_SIZE), np.float32)
    expand_np[:SEL_PER_STEP] = np.repeat(np.eye(SEL_PER_STEP, dtype=np.float32), SEL_BLOCK, axis=1)
    expand_s = jnp.asarray(expand_np, BF16)

    row = lambda v: v.reshape(1, -1)
    out_ev_p, out_ev_s, out_lru_p, out_lru_s, mem_k_p, mem_v_p = [], [], [], [], [], []

    for li in range(depth):
        gn = norm_gain[li]
        if li % 2 == 0:
            e = li // 2
            w_in = jnp.pad(w_in_even[e], ((0, 0), (0, EVEN_IN_PAD - EVEN_IN))).astype(BF16)
            pw = pool_w[e].astype(BF16)
            ps = row(pool_scale[e])
            wk_bd_t = _block_diag(w_cmp_k[e]).T.astype(BF16)
            wv_bd_t = _block_diag(w_cmp_v[e]).T.astype(BF16)
            w_out = w_out_even[e].astype(BF16)
            kv0 = POOL_WIDTH + NSA_WIDTH
            kcol = lambda k: slice(kv0 + k * KV_COLS, kv0 + (k + 1) * KV_COLS)
            w_u = w_in[:, :POOL_WIDTH]
            w_kpm = jnp.concatenate([w_in[:, kcol(2)], w_in[:, kcol(4)]], axis=1)
            w_t = w_in[:, POOL_WIDTH:kv0 + 7 * KV_COLS].T

            u, kpm, q_t, gate_t, kc, vc, ks, vs, kw, vw = even_in_prompt(xp, row(gn[0]), w_u, w_kpm, w_t,
                                                                         bp, tp, tm_p)
            u3 = u.reshape(bp, tp, POOL_WIDTH)
            a_out = pool_mix(u3, jnp.zeros((bp, POOL_HALO, POOL_WIDTH), F32), pw, ps, 0)
            kcmp, vcmp_t = compress_kv(kc, vc, pool_mat, wk_bd_t, wv_bd_t, ident_kv)
            o_nsa = nsa_prompt(q_t, gate_t, kcmp, vcmp_t, kpm.reshape(bp, tp, 2 * KV_COLS), vs, vw,
                               mselt, ident, tq, tk)
            xp = proj_norm_res([a_out.reshape(np_tok, POOL_WIDTH), o_nsa.reshape(np_tok, NSA_WIDTH)],
                               [w_out[:POOL_WIDTH], w_out[POOL_WIDTH:]], row(gn[1]), xp, tm_p)
            kv5 = lambda a: a.reshape(bp, NSA_KV_HEADS, HEAD_DIM, a.shape[-1]).transpose(0, 3, 1, 2)
            n_keep = min(WINDOW, tp)
            out_ev_p.append((u3[:, -POOL_STATE:], kv5(kc), kv5(vc), kv5(ks), kv5(vs),
                             kv5(kw[:, :, -n_keep:]), kv5(vw[:, :, -n_keep:])))

            u, q, kc, vc, ks, vs, kw, vw, gate = _even_in_proj(xs, row(gn[0]), w_in, tm_s)
            as3 = lambda a: a.reshape(bs, tsp, a.shape[-1])
            u3 = as3(u)
            past_pool = jnp.pad(state_pool[e], ((0, 0), (POOL_HALO - POOL_STATE, 0), (0, 0)))
            a_out = pool_mix(u3, past_pool, pw, ps, past_len)
            paged = lambda c: c.transpose(0, 1, 3, 4, 2).reshape(n_phys * n_even, KV_COLS, PAGE_SIZE)
            o_cmp, sel = sample_cmp(page_table, paged(cache_cmp_k), paged(cache_cmp_v), as3(kc), as3(vc),
                                    as3(q), wk_bd_t, wv_bd_t, msel_s, subpool, ident_kv, e, n_even, ts)
            n_win = cache_win_k.shape[2]
            win_k = cache_win_k[e].transpose(0, 2, 3, 1).reshape(bs, KV_COLS, n_win)
            win_v = cache_win_v[e].transpose(0, 2, 3, 1).reshape(bs, KV_COLS, n_win)
            o_nsa = sample_slc(page_table, paged(cache_slc_k), paged(cache_slc_v), as3(q), sel, expand_s,
                               as3(ks), as3(vs), win_k, win_v, as3(kw), as3(vw), o_cmp, as3(gate),
                               e, n_even, ts)
            xs = proj_norm_res([a_out.reshape(ns_tok, POOL_WIDTH), o_nsa.reshape(ns_tok, NSA_WIDTH)],
                               [w_out[:POOL_WIDTH], w_out[POOL_WIDTH:]], row(gn[1]), xs, tm_s)
            new4 = lambda a: as3(a)[:, :ts].reshape(bs, ts, NSA_KV_HEADS, HEAD_DIM)
            out_ev_s.append((jnp.concatenate([state_pool[e], u3[:, :ts]], axis=1)[:, -POOL_STATE:],
                             new4(kc), new4(vc), new4(ks), new4(vs),
                             jnp.concatenate([cache_win_k[e], new4(kw)], axis=1)[:, -n_win:],
                             jnp.concatenate([cache_win_v[e], new4(vw)], axis=1)[:, -n_win:]))
        else:
            o = li // 2
            w_in = w_in_odd[o].astype(BF16)
            cw = jnp.pad(conv_w[o], ((0, 8 - CONV_WIDTH), (0, 0)))
            consts = (cw, row(conv_b[o]), lru_wa[o].astype(BF16), lru_wx[o].astype(BF16),
                      row(lru_ba[o]), row(lru_bx[o]), row(lru_lambda[o]))
            w_out = w_out_odd[o].astype(BF16)
            groups = ((0, LRU_WIDTH, ((0, LRU_WIDTH, None),)), (LRU_WIDTH, LRU_WIDTH, ((0, LRU_WIDTH, None),)))

            xb, yb = norm_matmul(xp, row(gn[0]), w_in, groups, (F32, F32), tm_p)
            xb3, yb3 = xb.reshape(bp, tp, LRU_WIDTH), yb.reshape(bp, tp, LRU_WIDTH)
            gated, tail = lru(xb3, yb3, jnp.zeros((bp, 8, LRU_WIDTH), F32), jnp.zeros((bp, 1, LRU_WIDTH), F32),
                              *consts, tt=256, first_pos_zero=True)
            xp = proj_norm_res([gated.reshape(np_tok, LRU_WIDTH)], [w_out], row(gn[1]), xp, tm_p)
            out_lru_p.append((xb3[:, -(CONV_WIDTH - 1):], tail[:, 7]))

            xb, yb = norm_matmul(xs, row(gn[0]), w_in, groups, (F32, F32), tm_s)
            xb3, yb3 = xb.reshape(bs, tsp, LRU_WIDTH), yb.reshape(bs, tsp, LRU_WIDTH)
            conv_past = jnp.pad(state_lru_conv[o], ((0, 0), (8 - (CONV_WIDTH - 1), 0), (0, 0)))
            gated, tail = lru(xb3, yb3, conv_past, state_lru_h[o][:, None, :], *consts, tt=tsp,
                              first_pos_zero=False)
            xs = proj_norm_res([gated.reshape(ns_tok, LRU_WIDTH)], [w_out], row(gn[1]), xs, tm_s)
            conv_new = jnp.concatenate([state_lru_conv[o], xb3[:, :ts]], axis=1)[:, -(CONV_WIDTH - 1):]
            out_lru_s.append((conv_new, tail[:, ts - 1]))

        wq, wo = w_xq[li].astype(BF16), w_xo[li].astype(BF16)
        wkv = jnp.concatenate([w_xk[li], w_xv[li]], axis=1).astype(BF16)
        mem2d = mem_prompt.reshape(bp * N_MEM, d)
        kv_groups = ((0, D_MODEL, ((0, D_MODEL, None),)), (D_MODEL, D_MODEL, ((0, D_MODEL, None),)))
        mk, mv = norm_matmul(mem2d, row(mem_norm_gain[li]), wkv, kv_groups, (F32, F32), min(tm_p, bp * N_MEM))
        mk3, mv3 = mk.reshape(bp, N_MEM, d), mv.reshape(bp, N_MEM, d)
        mem_k_p.append(mk3.reshape(bp, N_MEM, XATTN_HEADS, XATTN_HEAD_DIM))
        mem_v_p.append(mv3.reshape(bp, N_MEM, XATTN_HEADS, XATTN_HEAD_DIM))
        xp = _xattn_block(xp, bp, tp, mk3, mv3, row(gn[2]), row(gn[3]), wq, wo, tm_p, 512)
        xs = _xattn_block(xs, bs, tsp, cache_mem_k[li].reshape(bs, N_MEM, d), cache_mem_v[li].reshape(bs, N_MEM, d),
                          row(gn[2]), row(gn[3]), wq, wo, tm_s, tsp)

        wg, wu, wd = w_ffn_gate[li].astype(BF16), w_ffn_up[li].astype(BF16), w_ffn_down[li].astype(BF16)
        xp = ffn(xp, row(gn[4]), row(gn[5]), wg, wu, wd, tm_p)
        xs = ffn(xs, row(gn[4]), row(gn[5]), wg, wu, wd, tm_s)

    stack = lambda items, k, axis=0: jnp.stack([s[k] for s in items], axis=axis)
    y_prompt = xp.reshape(bp, tp, d)
    y_sample = xs.reshape(bs, tsp, d)[:, :ts]
    return (y_prompt, y_sample,
            stack(out_ev_p, 0), stack(out_ev_s, 0),
            stack(out_ev_p, 1, 1), stack(out_ev_s, 1, 1),
            stack(out_ev_p, 2, 1), stack(out_ev_s, 2, 1),
            stack(out_ev_p, 3, 1), stack(out_ev_s, 3, 1),
            stack(out_ev_p, 4, 1), stack(out_ev_s, 4, 1),
            stack(out_ev_p, 5), stack(out_ev_s, 5),
            stack(out_ev_p, 6), stack(out_ev_s, 6),
            stack(out_lru_p, 1), stack(out_lru_s, 1),
            stack(out_lru_p, 0), stack(out_lru_s, 0),
            jnp.stack(mem_k_p), jnp.stack(mem_v_p))
```

```python
import functools

import numpy as np
import jax
import jax.numpy as jnp
from jax import lax
from jax.experimental import pallas as pl
from jax.experimental.pallas import tpu as pltpu

F32 = jnp.float32
BF16 = jnp.bfloat16

D_MODEL = 1024
RMS_EPS = 1e-6
NEG_INF = -1e30
FORCE_SCORE = 1e30
REMOVED_SCORE = -3e38
M_INIT = -1e29

POOL_WIDTH = 512
POOL_WINDOWS = (2, 4, 8, 16)
POOL_GROUP = 128
POOL_STATE = 15
POOL_HALO = 16

HEAD_DIM = 64
NSA_HEADS = 8
NSA_KV_HEADS = 2
HEADS_PER_GROUP = NSA_HEADS // NSA_KV_HEADS
NSA_WIDTH = 512
KV_COLS = 128
CMP_BLOCK = 32
CMP_STRIDE = 16
SEL_BLOCK = 64
SEL_TOPK = 16
WINDOW = 512
PAGE_SIZE = 128
EVEN_IN = POOL_WIDTH + NSA_WIDTH + 6 * KV_COLS + 3 * NSA_HEADS
EVEN_IN_PAD = 1920

LRU_WIDTH = 1024
LRU_BLOCKS = 4
LRU_BLOCK = 256
CONV_WIDTH = 4
LRU_C = 8.0

N_MEM = 256
XATTN_HEADS = 4
XATTN_HEAD_DIM = 256
D_FF = 2816
FF_CHUNK = 256

SAMPLE_T_PAD = 8
PAGES_PER_STEP = 16
SEL_PER_STEP = PAGES_PER_STEP * PAGE_SIZE // SEL_BLOCK
SUB_PER_STEP = PAGES_PER_STEP * PAGE_SIZE // CMP_STRIDE
LANE = 128

VMEM_LIMIT_BYTES = 56 * 1024 * 1024


def _cparams(*sem):
    return pltpu.CompilerParams(dimension_semantics=sem, vmem_limit_bytes=VMEM_LIMIT_BYTES)


def _rms(x, g):
    return x * lax.rsqrt(jnp.mean(x * x, axis=-1, keepdims=True) + RMS_EPS) * g


def _dot(a, b):
    return jnp.dot(a, b, preferred_element_type=F32)


def _dot_nt(a, b):
    return lax.dot_general(a, b, (((1,), (1,)), ((), ())), preferred_element_type=F32)


def _split3(x):
    p1 = x.astype(BF16)
    r1 = x - p1.astype(F32)
    p2 = r1.astype(BF16)
    p3 = (r1 - p2.astype(F32)).astype(BF16)
    return p1, p2, p3


def _full(shape):
    n = len(shape)
    return pl.BlockSpec(shape, lambda *_: (0,) * n)


def _norm_matmul_kernel(x_ref, g_ref, w_ref, *o_refs, groups):
    h = _rms(x_ref[...], g_ref[...]).astype(BF16)
    k = 0
    for start, width, outs in groups:
        z = _dot(h, w_ref[:, start:start + width])
        for off, w, post in outs:
            v = z[:, off:off + w]
            if post is not None:
                v = post(v)
            o_refs[k][...] = v.astype(o_refs[k].dtype)
            k += 1


def norm_matmul(x, g, w, groups, out_dtypes, tm):
    n, d = x.shape
    widths = [w_ for _, _, outs in groups for _, w_, _ in outs]
    return pl.pallas_call(
        functools.partial(_norm_matmul_kernel, groups=groups),
        grid=(n // tm,),
        in_specs=[pl.BlockSpec((tm, d), lambda i: (i, 0)), _full(g.shape), _full(w.shape)],
        out_specs=[pl.BlockSpec((tm, wd), lambda i: (i, 0)) for wd in widths],
        out_shape=[jax.ShapeDtypeStruct((n, wd), dt) for wd, dt in zip(widths, out_dtypes)],
        compiler_params=_cparams("parallel"),
    )(x, g, w)


def _even_in_prompt_kernel(x_ref, g_ref, wu_ref, wk_ref, wt_ref, u_ref, kpm_ref, qt_ref, gatet_ref, *kvt_refs):
    h = _rms(x_ref[...], g_ref[...]).astype(BF16)
    u_ref[...] = _dot(h, wu_ref[...])
    kpm_ref[...] = _dot(h, wk_ref[...]).astype(kpm_ref.dtype)
    z_t = _dot_nt(wt_ref[...], h)
    qt_ref[0] = (z_t[0:NSA_WIDTH] * (HEAD_DIM ** -0.5)).astype(qt_ref.dtype)
    for k, ref in enumerate(kvt_refs):
        ref[0] = z_t[NSA_WIDTH + k * KV_COLS:NSA_WIDTH + (k + 1) * KV_COLS, :]
    gatet_ref[0] = jax.nn.sigmoid(z_t[NSA_WIDTH + 6 * KV_COLS:, :])


def even_in_prompt(x, g, w_u, w_kpm, w_t, b, t, tm):
    n, d = x.shape
    per_seq = t // tm
    row_tile = lambda w: pl.BlockSpec((tm, w), lambda i: (i, 0))
    feat_tile = lambda f: pl.BlockSpec((1, f, tm), lambda i: (i // per_seq, 0, i % per_seq))
    n_kv = 6
    return pl.pallas_call(
        _even_in_prompt_kernel,
        grid=(n // tm,),
        in_specs=[row_tile(d), _full(g.shape), _full(w_u.shape), _full(w_kpm.shape), _full(w_t.shape)],
        out_specs=([row_tile(POOL_WIDTH), row_tile(2 * KV_COLS), feat_tile(NSA_WIDTH), feat_tile(LANE)]
                   + [feat_tile(KV_COLS)] * n_kv),
        out_shape=([jax.ShapeDtypeStruct((n, POOL_WIDTH), F32), jax.ShapeDtypeStruct((n, 2 * KV_COLS), BF16),
                    jax.ShapeDtypeStruct((b, NSA_WIDTH, t), BF16), jax.ShapeDtypeStruct((b, LANE, t), F32)]
                   + [jax.ShapeDtypeStruct((b, KV_COLS, t), F32)] * n_kv),
        compiler_params=_cparams("parallel"),
    )(x, g, w_u, w_kpm, w_t)


def _proj_norm_res_kernel(*refs, n_in):
    a_refs, w_refs = refs[:n_in], refs[n_in:2 * n_in]
    g_ref, x_ref, o_ref = refs[2 * n_in:]
    acc = _dot(a_refs[0][...], w_refs[0][...])
    for a_ref, w_ref in zip(a_refs[1:], w_refs[1:]):
        acc = acc + _dot(a_ref[...], w_ref[...])
    o_ref[...] = x_ref[...] + _rms(acc, g_ref[...])


def proj_norm_res(a_list, w_list, g, x, tm):
    n, d = x.shape
    n_in = len(a_list)
    return pl.pallas_call(
        functools.partial(_proj_norm_res_kernel, n_in=n_in),
        grid=(n // tm,),
        in_specs=([pl.BlockSpec((tm, a.shape[1]), lambda i: (i, 0)) for a in a_list]
                  + [_full(w.shape) for w in w_list]
                  + [_full(g.shape), pl.BlockSpec((tm, d), lambda i: (i, 0))]),
        out_specs=pl.BlockSpec((tm, d), lambda i: (i, 0)),
        out_shape=jax.ShapeDtypeStruct((n, d), F32),
        compiler_params=_cparams("parallel"),
    )(*a_list, *w_list, g, x)


def _ffn_block(x, g_in, g_out, wg_ref, wu_ref, wd_ref):
    h = _rms(x, g_in).astype(BF16)
    acc = None
    for c in range(D_FF // FF_CHUNK):
        cols = slice(c * FF_CHUNK, (c + 1) * FF_CHUNK)
        gate = _dot(h, wg_ref[:, cols])
        up = _dot(h, wu_ref[:, cols])
        act = (jax.nn.silu(gate) * up).astype(BF16)
        part = _dot(act, wd_ref[cols, :])
        acc = part if acc is None else acc + part
    return x + _rms(acc, g_out)


def _ffn_kernel(x_ref, g_in_ref, g_out_ref, wg_ref, wu_ref, wd_ref, o_ref):
    o_ref[...] = _ffn_block(x_ref[...], g_in_ref[...], g_out_ref[...], wg_ref, wu_ref, wd_ref)


def ffn(x, g_in, g_out, wg, wu, wd, tm):
    n, d = x.shape
    resident = functools.partial(pl.BlockSpec, pipeline_mode=pl.Buffered(1))
    return pl.pallas_call(
        _ffn_kernel,
        grid=(n // tm,),
        in_specs=[pl.BlockSpec((tm, d), lambda i: (i, 0)), _full(g_in.shape), _full(g_out.shape),
                  resident(wg.shape, lambda i: (0, 0)), resident(wu.shape, lambda i: (0, 0)),
                  resident(wd.shape, lambda i: (0, 0))],
        out_specs=pl.BlockSpec((tm, d), lambda i: (i, 0)),
        out_shape=jax.ShapeDtypeStruct((n, d), F32),
        compiler_params=_cparams("parallel"),
    )(x, g_in, g_out, wg, wu, wd)


def _xattn_heads(q, mk_ref, mv_ref):
    outs = []
    for hd in range(XATTN_HEADS):
        cols = slice(hd * XATTN_HEAD_DIM, (hd + 1) * XATTN_HEAD_DIM)
        k = mk_ref[0, :, cols].astype(BF16)
        v = mv_ref[0, :, cols].astype(BF16)
        s = _dot_nt(q[:, cols], k)
        e = jnp.exp(s - jnp.max(s, axis=-1, keepdims=True))
        p = e / jnp.sum(e, axis=-1, keepdims=True)
        outs.append(_dot(p.astype(BF16), v).astype(BF16))
    return jnp.concatenate(outs, axis=-1)


def _xattn_kernel(q_ref, mk_ref, mv_ref, o_ref):
    o_ref[0] = _xattn_heads(q_ref[0], mk_ref, mv_ref)


def xattn(q, mk, mv, tq):
    b, t, d = q.shape
    return pl.pallas_call(
        _xattn_kernel,
        grid=(b, t // tq),
        in_specs=[pl.BlockSpec((1, tq, d), lambda i, j: (i, j, 0)),
                  pl.BlockSpec((1, N_MEM, d), lambda i, j: (i, 0, 0)),
                  pl.BlockSpec((1, N_MEM, d), lambda i, j: (i, 0, 0))],
        out_specs=pl.BlockSpec((1, tq, d), lambda i, j: (i, j, 0)),
        out_shape=jax.ShapeDtypeStruct((b, t, d), BF16),
        compiler_params=_cparams("parallel", "parallel"),
    )(q, mk, mv)


def _post_mixer_kernel(*refs, n_in):
    a_refs, w_refs = refs[:n_in], refs[n_in:2 * n_in]
    x_ref, gn_ref, mk_ref, mv_ref, wq_ref, wo_ref, wg_ref, wu_ref, wd_ref, o_ref = refs[2 * n_in:]
    gain = lambda k: gn_ref[k:k + 1, :]
    mix = _dot(a_refs[0][0], w_refs[0][...])
    for a_ref, w_ref in zip(a_refs[1:], w_refs[1:]):
        mix = mix + _dot(a_ref[0], w_ref[...])
    x1 = x_ref[0] + _rms(mix, gain(1))
    q = (_dot(_rms(x1, gain(2)).astype(BF16), wq_ref[...]) * (XATTN_HEAD_DIM ** -0.5)).astype(BF16)
    attn = _xattn_heads(q, mk_ref, mv_ref)
    x2 = x1 + _rms(_dot(attn, wo_ref[...]), gain(3))
    o_ref[0] = _ffn_block(x2, gain(4), gain(5), wg_ref, wu_ref, wd_ref)


def post_mixer(a_list, w_list, x, gn, mk, mv, wq, wo, wg, wu, wd, tm):
    b, t, d = x.shape
    n_in = len(a_list)
    resident = lambda a: pl.BlockSpec(a.shape, lambda i, j: (0,) * a.ndim, pipeline_mode=pl.Buffered(1))
    tile = lambda w: pl.BlockSpec((1, tm, w), lambda i, j: (i, j, 0))
    per_b = lambda a: pl.BlockSpec((1,) + a.shape[1:], lambda i, j: (i, 0, 0))
    return pl.pallas_call(
        functools.partial(_post_mixer_kernel, n_in=n_in),
        grid=(b, t // tm),
        in_specs=([tile(a.shape[2]) for a in a_list] + [resident(w) for w in w_list]
                  + [tile(d), resident(gn), per_b(mk), per_b(mv)]
                  + [resident(w) for w in (wq, wo, wg, wu, wd)]),
        out_specs=tile(d),
        out_shape=jax.ShapeDtypeStruct((b, t, d), F32),
        compiler_params=_cparams("parallel", "parallel"),
    )(*a_list, *w_list, x, gn, mk, mv, wq, wo, wg, wu, wd)


def _pool_kernel(u_ref, past_ref, w_ref, scale_ref, o_ref, buf, *, t, chunk, start_pos):
    buf[0:POOL_HALO, :] = past_ref[0]
    buf[POOL_HALO:POOL_HALO + t, :] = u_ref[0]
    for c in range(t // chunk):
        base = POOL_HALO + c * chunk
        pos = start_pos + c * chunk + lax.broadcasted_iota(jnp.int32, (chunk, 1), 0)
        for gi, win in enumerate(POOL_WINDOWS):
            cols = slice(gi * POOL_GROUP, (gi + 1) * POOL_GROUP)
            cur = buf[base:base + chunk, cols]
            win_sum = cur
            for k in range(1, win):
                win_sum = win_sum + buf[base - k:base - k + chunk, cols]
            cnt = jnp.minimum(pos + 1, win).astype(F32)
            diff = win_sum / cnt - cur
            y = _dot(diff.astype(BF16), w_ref[gi]) * scale_ref[:, cols]
            o_ref[0, c * chunk:(c + 1) * chunk, cols] = y.astype(o_ref.dtype)


def pool_mix(u, past, w, scale, start_pos):
    b, t, c = u.shape
    chunk = min(t, 256)
    return pl.pallas_call(
        functools.partial(_pool_kernel, t=t, chunk=chunk, start_pos=start_pos),
        grid=(b,),
        in_specs=[pl.BlockSpec((1, t, c), lambda i: (i, 0, 0)),
                  pl.BlockSpec((1, POOL_HALO, c), lambda i: (i, 0, 0)),
                  _full(w.shape), _full(scale.shape)],
        out_specs=pl.BlockSpec((1, t, c), lambda i: (i, 0, 0)),
        out_shape=jax.ShapeDtypeStruct((b, t, c), BF16),
        scratch_shapes=[pltpu.VMEM((POOL_HALO + t, c), F32)],
        compiler_params=_cparams("parallel"),
    )(u, past, w, scale)


def _compress_kernel(kc_ref, vc_ref, pool_ref, wkt_ref, wvt_ref, ident_ref, k_ref, vt_ref):
    def compress(src, wt_ref):
        x = src[0]
        hi = x.astype(BF16)
        lo = (x - hi.astype(F32)).astype(BF16)
        mean_t = _dot(hi, pool_ref[...]) + _dot(lo, pool_ref[...])
        return _dot(wt_ref[...], mean_t.astype(BF16)).astype(BF16)
    k_ref[0] = _dot_nt(ident_ref[...], compress(kc_ref, wkt_ref)).astype(BF16)
    vt_ref[0] = compress(vc_ref, wvt_ref)


def compress_kv(kc_t, vc_t, pool_mat, wk_bd_t, wv_bd_t, ident):
    b, c, t = kc_t.shape
    n_cmp = pool_mat.shape[1]
    per_b = pl.BlockSpec((1, c, t), lambda i: (i, 0, 0))
    return pl.pallas_call(
        _compress_kernel,
        grid=(b,),
        in_specs=[per_b, per_b, _full(pool_mat.shape), _full(wk_bd_t.shape), _full(wv_bd_t.shape),
                  _full(ident.shape)],
        out_specs=[pl.BlockSpec((1, n_cmp, c), lambda i: (i, 0, 0)), pl.BlockSpec((1, c, n_cmp), lambda i: (i, 0, 0))],
        out_shape=[jax.ShapeDtypeStruct((b, n_cmp, c), BF16), jax.ShapeDtypeStruct((b, c, n_cmp), BF16)],
        compiler_params=_cparams("parallel"),
    )(kc_t, vc_t, pool_mat, wk_bd_t, wv_bd_t, ident)


def _masked_softmax_rows(s, mask):
    s = jnp.where(mask, s, NEG_INF)
    e = jnp.where(mask, jnp.exp(s - jnp.max(s, axis=-1, keepdims=True)), 0.0)
    return e / jnp.maximum(jnp.sum(e, axis=-1, keepdims=True), 1e-30)


def _flash_step(q_pad, k, v_t, mask, carry, tq):
    m, l, acc = carry
    bias = jnp.where(mask, 0.0, NEG_INF)
    s = _dot(k, q_pad) + jnp.concatenate([bias] * HEADS_PER_GROUP, axis=1)
    m_new = jnp.maximum(m, jnp.max(s, axis=0, keepdims=True))
    alpha = jnp.exp(m - m_new)
    e = jnp.exp(s - m_new)
    l = alpha * l + jnp.sum(e, axis=0, keepdims=True)
    acc = alpha * acc + _dot(v_t, e.astype(BF16))
    return m_new, l, acc


def _nsa_prompt_kernel(qt_ref, gatet_ref, kcmp_ref, vcmpt_ref, kpm_ref, vst_ref, vwt_ref,
                       mselt_ref, ident_ref, o_ref, sel_s, *, tq, tk):
    i = pl.program_id(1)
    q0 = i * tq
    n_cmp = kcmp_ref.shape[1]
    n_sel = mselt_ref.shape[0]
    blocks_per_tile = tk // SEL_BLOCK
    tiles_per_q = tq // tk
    qpos_row = q0 + lax.broadcasted_iota(jnp.int32, (1, tq), 1)
    key_col = lax.broadcasted_iota(jnp.int32, (tk, 1), 0)
    cmp_end_col = lax.broadcasted_iota(jnp.int32, (n_cmp, 1), 0) * CMP_STRIDE + (CMP_BLOCK - 1)
    cmp_bias = jnp.where(cmp_end_col <= qpos_row, 0.0, NEG_INF)
    sel_j = lax.broadcasted_iota(jnp.int32, (n_sel, 1), 0)
    cur_row = qpos_row // SEL_BLOCK
    gate_t = gatet_ref[0]
    zero_half = jnp.zeros((HEAD_DIM, HEADS_PER_GROUP * tq), BF16)

    groups = range(NSA_KV_HEADS)
    grows = [slice(g * HEAD_DIM, (g + 1) * HEAD_DIM) for g in groups]
    q_pads, o_cmps = [], []
    for g in groups:
        q_t_g = jnp.concatenate(
            [qt_ref[0, (g * HEADS_PER_GROUP + hh) * HEAD_DIM:(g * HEADS_PER_GROUP + hh + 1) * HEAD_DIM, :]
             for hh in range(HEADS_PER_GROUP)], axis=1)
        q_pad = jnp.concatenate([q_t_g, zero_half] if g == 0 else [zero_half, q_t_g], axis=0)
        q_pads.append(q_pad)

        s_t = _dot(kcmp_ref[0], q_pad) + jnp.concatenate([cmp_bias] * HEADS_PER_GROUP, axis=1)
        e_t = jnp.exp(s_t - jnp.maximum(jnp.max(s_t, axis=0, keepdims=True), M_INIT))
        p_t = e_t / jnp.maximum(jnp.sum(e_t, axis=0, keepdims=True), 1e-30)
        o_cmps.append(_dot(vcmpt_ref[0, grows[g], :], p_t.astype(BF16)))
        psum_t = p_t[:, 0:tq]
        for hh in range(1, HEADS_PER_GROUP):
            psum_t = psum_t + p_t[:, hh * tq:(hh + 1) * tq]

        imp_t = None
        for part in _split3(psum_t):
            d = _dot(mselt_ref[...], part)
            imp_t = d if imp_t is None else imp_t + d
        valid = sel_j * SEL_BLOCK <= qpos_row
        forced = (sel_j == 0) | (sel_j == cur_row) | (sel_j == cur_row - 1)
        score = jnp.where(forced, FORCE_SCORE, jnp.where(valid, imp_t, NEG_INF))
        rank = jnp.zeros((n_sel, tq), F32)
        for jp in range(n_sel):
            sj = score[jp:jp + 1, :]
            beats = (sj > score) | ((sj == score) & (sel_j > jp))
            rank = rank + jnp.where(beats, 1.0, 0.0)
        sel_s[g] = jnp.where(rank < SEL_TOPK, 1.0, 0.0)

    init = (jnp.full((1, HEADS_PER_GROUP * tq), M_INIT, F32),
            jnp.zeros((1, HEADS_PER_GROUP * tq), F32),
            jnp.zeros((HEAD_DIM, HEADS_PER_GROUP * tq), F32))

    def slc_steps(kt, keys, carries):
        k = kpm_ref[0, keys, 0:KV_COLS]
        causal = kt * tk + key_col <= qpos_row
        out = []
        for g in groups:
            chosen = jnp.concatenate(
                [jnp.broadcast_to(sel_s[g, pl.ds(kt * blocks_per_tile + r, 1), :], (SEL_BLOCK, tq))
                 for r in range(blocks_per_tile)], axis=0)
            v_t = vst_ref[0, grows[g], keys].astype(BF16)
            out.append(_flash_step(q_pads[g], k, v_t, (chosen > 0.5) & causal, carries[g], tq))
        return tuple(out)

    def win_steps(kt, keys, carries):
        k = kpm_ref[0, keys, KV_COLS:2 * KV_COLS]
        dist = qpos_row - (kt * tk + key_col)
        mask = (dist >= 0) & (dist < WINDOW)
        return tuple(_flash_step(q_pads[g], k, vwt_ref[0, grows[g], keys].astype(BF16), mask, carries[g], tq)
                     for g in groups)

    def far_body(kt, slc):
        keys = pl.ds(pl.multiple_of(kt * tk, tk), tk)
        return slc_steps(kt, keys, slc)

    def band_body(kt, carry):
        slc, win = carry
        keys = pl.ds(pl.multiple_of(kt * tk, tk), tk)
        return slc_steps(kt, keys, slc), win_steps(kt, keys, win)

    first_win = jnp.maximum((i * tq - (WINDOW - 1)) // tk, 0)
    slc = lax.fori_loop(0, first_win, far_body, (init,) * NSA_KV_HEADS)
    slc, win = lax.fori_loop(first_win, (i + 1) * tiles_per_q, band_body, (slc, (init,) * NSA_KV_HEADS))

    head_out = [None] * NSA_HEADS
    for g in groups:
        o_slc = slc[g][2] / slc[g][1]
        o_win = win[g][2] / win[g][1]
        for hh in range(HEADS_PER_GROUP):
            h = g * HEADS_PER_GROUP + hh
            lanes = slice(hh * tq, (hh + 1) * tq)
            head_out[h] = (gate_t[3 * h:3 * h + 1, :] * o_cmps[g][:, lanes]
                           + gate_t[3 * h + 1:3 * h + 2, :] * o_slc[:, lanes]
                           + gate_t[3 * h + 2:3 * h + 3, :] * o_win[:, lanes])
    o_t = jnp.concatenate(head_out, axis=0).astype(BF16)
    o_ref[0] = _dot_nt(ident_ref[...], o_t).astype(o_ref.dtype)


def nsa_prompt(q_t, gate_t, kcmp, vcmp_t, kpm, vs_t, vw_t, mselt, ident, tq, tk):
    b, _, t = q_t.shape
    feat_tile = lambda f: pl.BlockSpec((1, f, tq), lambda i, j: (i, 0, j))
    whole = lambda a: pl.BlockSpec((1,) + a.shape[1:], lambda i, j: (i, 0, 0))
    per_batch = [kcmp, vcmp_t, kpm, vs_t, vw_t]
    return pl.pallas_call(
        functools.partial(_nsa_prompt_kernel, tq=tq, tk=tk),
        grid=(b, t // tq),
        in_specs=([feat_tile(NSA_WIDTH), feat_tile(LANE)] + [whole(a) for a in per_batch]
                  + [_full(mselt.shape), _full(ident.shape)]),
        out_specs=pl.BlockSpec((1, tq, NSA_WIDTH), lambda i, j: (i, j, 0)),
        out_shape=jax.ShapeDtypeStruct((b, t, NSA_WIDTH), BF16),
        scratch_shapes=[pltpu.VMEM((NSA_KV_HEADS, mselt.shape[0], tq), F32)],
        compiler_params=_cparams("parallel", "arbitrary"),
    )(q_t, gate_t, *per_batch, mselt, ident)


def _stack_heads(q, g):
    return jnp.concatenate(
        [q[:, (g * HEADS_PER_GROUP + hh) * HEAD_DIM:(g * HEADS_PER_GROUP + hh + 1) * HEAD_DIM]
         for hh in range(HEADS_PER_GROUP)], axis=0)


def _sample_cmp_kernel(pt_ref, *refs, n_valid, past_len):
    del pt_ref
    p = PAGES_PER_STEP
    tp = SAMPLE_T_PAD
    k_pages, v_pages = refs[:p], refs[p:2 * p]
    (kn_ref, vn_ref, q_ref, wkt_ref, wvt_ref, msel_ref, subpool_ref, ident_ref,
     ocmp_ref, sel_ref, sub_k, sub_v) = refs[2 * p:]
    c = pl.program_id(1)
    n_sub = past_len // CMP_STRIDE
    steps = past_len // (p * PAGE_SIZE)

    for pages, sub in ((k_pages, sub_k), (v_pages, sub_v)):
        x = jnp.concatenate([r[...] for r in pages], axis=1)
        hi = x.astype(BF16)
        lo = (x - hi.astype(F32)).astype(BF16)
        sub[c] = _dot(hi, subpool_ref[...]) + _dot(lo, subpool_ref[...])

    @pl.when(c == steps - 1)
    def _():
        t_col = lax.broadcasted_iota(jnp.int32, (tp, 1), 0)
        new_valid = lax.broadcasted_iota(jnp.int32, (1, tp), 1) < n_valid
        first_lane = lax.broadcasted_iota(jnp.int32, (1, SUB_PER_STEP), 1) == 0
        cmp_kv = []
        for new_ref, sub, wt_ref in ((kn_ref, sub_k, wkt_ref), (vn_ref, sub_v, wvt_ref)):
            new_t = None
            for part in _split3(new_ref[0]):
                d = _dot_nt(ident_ref[...], part)
                new_t = d if new_t is None else new_t + d
            new_sum = jnp.sum(jnp.where(new_valid, new_t, 0.0), axis=1, keepdims=True)
            sub[steps] = jnp.where(first_lane, new_sum, 0.0)
            sums = jnp.concatenate([sub[s] for s in range(steps + 1)], axis=1)
            mean_t = (sums[:, 0:n_sub] + sums[:, 1:n_sub + 1]) * (1.0 / CMP_BLOCK)
            cmp_kv.append(_dot(wt_ref[...], mean_t.astype(BF16)).astype(BF16))
        kcmp_t, vcmp_t = cmp_kv

        rows = HEADS_PER_GROUP * tp
        qpos_col = past_len + lax.broadcasted_iota(jnp.int32, (rows, 1), 0) % tp
        cmp_end_row = lax.broadcasted_iota(jnp.int32, (1, n_sub), 1) * CMP_STRIDE + (CMP_BLOCK - 1)
        cmp_mask = cmp_end_row <= qpos_col

        n_lane = sel_ref.shape[2]
        lane = lax.broadcasted_iota(jnp.int32, (1, n_lane), 1)
        sel_j = (lane // LANE) * SEL_PER_STEP + lane % LANE
        lane_used = lane % LANE < SEL_PER_STEP
        qpos_t = past_len + t_col
        cur = qpos_t // SEL_BLOCK
        valid = lane_used & (sel_j * SEL_BLOCK <= qpos_t)
        forced = lane_used & ((sel_j == 0) | (sel_j == cur) | (sel_j == cur - 1))

        q = q_ref[0]
        o_heads = [None] * NSA_HEADS
        scores = []
        for g in range(NSA_KV_HEADS):
            gcols = slice(g * HEAD_DIM, (g + 1) * HEAD_DIM)
            prob = _masked_softmax_rows(_dot(_stack_heads(q, g), kcmp_t[gcols, :]), cmp_mask)
            o_g = _dot_nt(prob.astype(BF16), vcmp_t[gcols, :])
            psum = prob[0:tp]
            for hh in range(1, HEADS_PER_GROUP):
                psum = psum + prob[hh * tp:(hh + 1) * tp]
                o_heads[g * HEADS_PER_GROUP + hh] = o_g[hh * tp:(hh + 1) * tp]
            o_heads[g * HEADS_PER_GROUP] = o_g[0:tp]
            imp = None
            for part in _split3(psum):
                d = _dot(part, msel_ref[...])
                imp = d if imp is None else imp + d
            scores.append(jnp.where(forced, FORCE_SCORE, jnp.where(valid, imp, NEG_INF)))
        score = jnp.concatenate(scores, axis=0)
        chosen = jnp.zeros(score.shape, F32)
        for _ in range(SEL_TOPK):
            best = jnp.max(score, axis=-1, keepdims=True)
            first = jnp.min(jnp.where(score == best, lane, n_lane), axis=-1, keepdims=True)
            hit = lane == first
            chosen = jnp.where(hit, 1.0, chosen)
            score = jnp.where(hit, REMOVED_SCORE, score)
        sel_ref[0] = chosen
        ocmp_ref[0] = jnp.concatenate(o_heads, axis=-1)


def _page_spec(k, layer, n_layers):
    return pl.BlockSpec((None, KV_COLS, PAGE_SIZE),
                        lambda i, c, pt: (pt[i, c * PAGES_PER_STEP + k] * n_layers + layer, 0, 0))


def sample_cmp(page_table, cache_k, cache_v, kc_new, vc_new, q, wk_bd_t, wv_bd_t, msel, subpool, ident,
               layer, n_layers, n_valid):
    b, n_pages = page_table.shape
    past_len = n_pages * PAGE_SIZE
    steps = n_pages // PAGES_PER_STEP
    n_lane = msel.shape[1]
    per_b = lambda a: pl.BlockSpec((1,) + a.shape[1:], lambda i, c, pt: (i, 0, 0))
    const = lambda a: pl.BlockSpec(a.shape, lambda i, c, pt: (0,) * a.ndim)
    consts = [wk_bd_t, wv_bd_t, msel, subpool, ident]
    grid_spec = pltpu.PrefetchScalarGridSpec(
        num_scalar_prefetch=1,
        grid=(b, steps),
        in_specs=([_page_spec(k, layer, n_layers) for k in range(PAGES_PER_STEP)] * 2
                  + [per_b(kc_new), per_b(vc_new), per_b(q)] + [const(a) for a in consts]),
        out_specs=[pl.BlockSpec((1, SAMPLE_T_PAD, NSA_WIDTH), lambda i, c, pt: (i, 0, 0)),
                   pl.BlockSpec((1, 2 * SAMPLE_T_PAD, n_lane), lambda i, c, pt: (i, 0, 0))],
        scratch_shapes=[pltpu.VMEM((steps + 1, KV_COLS, SUB_PER_STEP), F32)] * 2,
    )
    return pl.pallas_call(
        functools.partial(_sample_cmp_kernel, n_valid=n_valid, past_len=past_len),
        grid_spec=grid_spec,
        out_shape=[jax.ShapeDtypeStruct((b, SAMPLE_T_PAD, NSA_WIDTH), F32),
                   jax.ShapeDtypeStruct((b, 2 * SAMPLE_T_PAD, n_lane), F32)],
        compiler_params=_cparams("parallel", "arbitrary"),
    )(page_table, *([cache_k] * PAGES_PER_STEP), *([cache_v] * PAGES_PER_STEP),
      kc_new, vc_new, q, *consts)


def _sample_slc_kernel(pt_ref, *refs, n_valid, past_len):
    del pt_ref
    p = PAGES_PER_STEP
    tp = SAMPLE_T_PAD
    rows = HEADS_PER_GROUP * tp
    k_pages, v_pages = refs[:p], refs[p:2 * p]
    (q_ref, sel_ref, expand_ref, ksn_ref, vsn_ref, wk_ref, wv_ref, kwn_ref, vwn_ref, ocmp_ref, gate_ref,
     o_ref, m_s, l_s, acc_s) = refs[2 * p:]
    c = pl.program_id(1)
    q = q_ref[0]
    zero = jnp.zeros((rows, HEAD_DIM), BF16)
    q_bd = jnp.concatenate([jnp.concatenate([_stack_heads(q, 0), zero], axis=1),
                            jnp.concatenate([zero, _stack_heads(q, 1)], axis=1)], axis=0)
    t_col = lax.broadcasted_iota(jnp.int32, (NSA_KV_HEADS * rows, 1), 0) % tp
    new_row = lax.broadcasted_iota(jnp.int32, (1, tp), 1)
    new_mask = (new_row <= t_col) & (new_row < n_valid)

    @pl.when(c == 0)
    def _():
        m_s[...] = jnp.full(m_s.shape, NEG_INF, F32)
        l_s[...] = jnp.zeros(l_s.shape, F32)
        acc_s[...] = jnp.zeros(acc_s.shape, F32)

    k_all_t = jnp.concatenate([r[...] for r in k_pages], axis=1).astype(BF16)
    v_all_t = jnp.concatenate([r[...] for r in v_pages], axis=1).astype(BF16)
    chosen = _dot(sel_ref[0].astype(BF16), expand_ref[...])
    keymask = jnp.concatenate([chosen[0:tp]] * HEADS_PER_GROUP + [chosen[tp:2 * tp]] * HEADS_PER_GROUP,
                              axis=0) > 0.5
    s = jnp.where(keymask, _dot(q_bd, k_all_t), NEG_INF)
    m_old = m_s[...]
    m_new = jnp.maximum(m_old, jnp.max(s, axis=-1, keepdims=True))
    alpha = jnp.exp(m_old - m_new)
    e = jnp.where(keymask, jnp.exp(s - m_new), 0.0)
    l_s[...] = alpha * l_s[...] + jnp.sum(e, axis=-1, keepdims=True)
    acc_s[...] = alpha * acc_s[...] + _dot_nt(e.astype(BF16), v_all_t)
    m_s[...] = m_new

    @pl.when(c == pl.num_programs(1) - 1)
    def _():
        n_win = wk_ref.shape[2]
        win_row = lax.broadcasted_iota(jnp.int32, (1, n_win), 1)
        dist = t_col + n_win - win_row
        win_mask = (dist >= 0) & (dist < WINDOW)
        s_n = jnp.where(new_mask, _dot_nt(q_bd, ksn_ref[0].astype(BF16)), NEG_INF)
        m_fin = jnp.maximum(m_new, jnp.max(s_n, axis=-1, keepdims=True))
        a_fin = jnp.exp(m_new - m_fin)
        e_n = jnp.where(new_mask, jnp.exp(s_n - m_fin), 0.0)
        l_fin = a_fin * l_s[...] + jnp.sum(e_n, axis=-1, keepdims=True)
        o_slc = (a_fin * acc_s[...] + _dot(e_n.astype(BF16), vsn_ref[0].astype(BF16))) / l_fin
        s_p = jnp.where(win_mask, _dot(q_bd, wk_ref[0].astype(BF16)), NEG_INF)
        s_w = jnp.where(new_mask, _dot_nt(q_bd, kwn_ref[0].astype(BF16)), NEG_INF)
        m_w = jnp.maximum(jnp.max(s_p, axis=-1, keepdims=True), jnp.max(s_w, axis=-1, keepdims=True))
        e_p = jnp.where(win_mask, jnp.exp(s_p - m_w), 0.0)
        e_w = jnp.where(new_mask, jnp.exp(s_w - m_w), 0.0)
        l_w = jnp.sum(e_p, axis=-1, keepdims=True) + jnp.sum(e_w, axis=-1, keepdims=True)
        o_win = (_dot_nt(e_p.astype(BF16), wv_ref[0].astype(BF16))
                 + _dot(e_w.astype(BF16), vwn_ref[0].astype(BF16))) / l_w
        gate_all = gate_ref[0]
        head_out = []
        for h in range(NSA_HEADS):
            g, hh = divmod(h, HEADS_PER_GROUP)
            r = slice(g * rows + hh * tp, g * rows + (hh + 1) * tp)
            cols = slice(g * HEAD_DIM, (g + 1) * HEAD_DIM)
            gate = gate_all[:, 3 * h:3 * h + 3]
            head_out.append(gate[:, 0:1] * ocmp_ref[0, :, h * HEAD_DIM:(h + 1) * HEAD_DIM]
                            + gate[:, 1:2] * o_slc[r, cols] + gate[:, 2:3] * o_win[r, cols])
        o_ref[0] = jnp.concatenate(head_out, axis=-1).astype(o_ref.dtype)


def sample_slc(page_table, cache_k, cache_v, q, sel, expand, ks_new, vs_new, win_k, win_v, kw_new, vw_new,
               o_cmp, gate, layer, n_layers, n_valid):
    b, n_pages = page_table.shape
    past_len = n_pages * PAGE_SIZE
    steps = n_pages // PAGES_PER_STEP
    rows = HEADS_PER_GROUP * SAMPLE_T_PAD
    per_b = lambda a: pl.BlockSpec((1,) + a.shape[1:], lambda i, c, pt: (i, 0, 0))
    const = lambda a: pl.BlockSpec(a.shape, lambda i, c, pt: (0,) * a.ndim)
    grid_spec = pltpu.PrefetchScalarGridSpec(
        num_scalar_prefetch=1,
        grid=(b, steps),
        in_specs=([_page_spec(k, layer, n_layers) for k in range(PAGES_PER_STEP)] * 2
                  + [per_b(q), pl.BlockSpec((1, 2 * SAMPLE_T_PAD, LANE), lambda i, c, pt: (i, 0, c)),
                     const(expand), per_b(ks_new), per_b(vs_new), per_b(win_k), per_b(win_v),
                     per_b(kw_new), per_b(vw_new), per_b(o_cmp), per_b(gate)]),
        out_specs=pl.BlockSpec((1, SAMPLE_T_PAD, NSA_WIDTH), lambda i, c, pt: (i, 0, 0)),
        scratch_shapes=[pltpu.VMEM((NSA_KV_HEADS * rows, 1), F32), pltpu.VMEM((NSA_KV_HEADS * rows, 1), F32),
                        pltpu.VMEM((NSA_KV_HEADS * rows, KV_COLS), F32)],
    )
    return pl.pallas_call(
        functools.partial(_sample_slc_kernel, n_valid=n_valid, past_len=past_len),
        grid_spec=grid_spec,
        out_shape=jax.ShapeDtypeStruct((b, SAMPLE_T_PAD, NSA_WIDTH), BF16),
        compiler_params=_cparams("parallel", "arbitrary"),
    )(page_table, *([cache_k] * PAGES_PER_STEP), *([cache_v] * PAGES_PER_STEP),
      q, sel, expand, ks_new, vs_new, win_k, win_v, kw_new, vw_new, o_cmp, gate)


def _lru_kernel(xb_ref, gy_ref, cpast_ref, h0_ref, cw_ref, cb_ref, wa_ref, wx_ref, ba_ref, bx_ref, lam_ref,
                o_ref, tail_ref, xbuf, h_carry, a_s, d_s, *, tt, first_pos_zero):
    j = pl.program_id(1)

    @pl.when(j == 0)
    def _():
        xbuf[...] = cpast_ref[0]
        h_carry[...] = h0_ref[0]

    sub = lax.broadcasted_iota(jnp.int32, (8, LRU_WIDTH), 0)
    x_cur = xb_ref[0]
    prev8 = xbuf[...]

    def delayed(k):
        if k == 0:
            return x_cur
        rolled = pltpu.roll(x_cur, k, 0)
        head = jnp.where(sub < k, pltpu.roll(prev8, k, 0), rolled[0:8])
        return jnp.concatenate([head, rolled[8:]], axis=0) if tt > 8 else head

    xc = delayed(CONV_WIDTH - 1) * cw_ref[0:1, :]
    for k in range(1, CONV_WIDTH):
        xc = xc + delayed(CONV_WIDTH - 1 - k) * cw_ref[k:k + 1, :]
    xf = xc + cb_ref[...]
    xf_b = xf.astype(BF16)
    r_parts, i_parts = [], []
    for n in range(LRU_BLOCKS):
        cols = slice(n * LRU_BLOCK, (n + 1) * LRU_BLOCK)
        r_parts.append(_dot(xf_b[:, cols], wa_ref[n]))
        i_parts.append(_dot(xf_b[:, cols], wx_ref[n]))
    r = jax.nn.sigmoid(jnp.concatenate(r_parts, axis=-1) + ba_ref[...])
    gate_i = jax.nn.sigmoid(jnp.concatenate(i_parts, axis=-1) + bx_ref[...])
    neg_lam = -lam_ref[...]
    softplus = jnp.maximum(neg_lam, 0.0) + jnp.log1p(jnp.exp(-jnp.abs(neg_lam)))
    log_a = -LRU_C * r * softplus
    a = jnp.exp(log_a)
    mult = jnp.sqrt(-jnp.tanh(log_a) * (a * a + 1.0))
    if first_pos_zero:
        row = lax.broadcasted_iota(jnp.int32, (tt, 1), 0)
        mult = jnp.where((row == 0) & (j == 0), 1.0, mult)
    a_s[...] = a
    d_s[...] = mult * gate_i * xf

    def body(blk, h):
        rows = pl.ds(pl.multiple_of(blk * 8, 8), 8)
        a8 = a_s[rows, :]
        d8 = d_s[rows, :]
        for s in (1, 2, 4):
            keep = sub >= s
            d8 = jnp.where(keep, a8 * pltpu.roll(d8, s, 0) + d8, d8)
            a8 = jnp.where(keep, a8 * pltpu.roll(a8, s, 0), a8)
        hs = a8 * h + d8
        d_s[rows, :] = hs
        return hs[7:8, :]

    n_blk = tt // 8
    h_carry[...] = lax.fori_loop(0, n_blk, body, h_carry[...], unroll=min(4, n_blk))
    hs = d_s[...]
    o_ref[0] = (hs * gy_ref[0]).astype(o_ref.dtype)
    tail_ref[0] = d_s[tt - 8:tt, :]
    xbuf[...] = x_cur[tt - 8:tt, :]


def lru(xb, gy, conv_past, h0, cw, cb, wa, wx, ba, bx, lam, tt, first_pos_zero):
    b, t, w = xb.shape
    tile = pl.BlockSpec((1, tt, w), lambda i, j: (i, j, 0))
    per_b = lambda a: pl.BlockSpec((1,) + a.shape[1:], lambda i, j: (i, 0, 0))
    consts = [cw, cb, wa, wx, ba, bx, lam]
    return pl.pallas_call(
        functools.partial(_lru_kernel, tt=tt, first_pos_zero=first_pos_zero),
        grid=(b, t // tt),
        in_specs=[tile, tile, per_b(conv_past), per_b(h0)] + [_full(a.shape) for a in consts],
        out_specs=[tile, pl.BlockSpec((1, 8, w), lambda i, j: (i, 0, 0))],
        out_shape=[jax.ShapeDtypeStruct((b, t, w), BF16), jax.ShapeDtypeStruct((b, 8, w), F32)],
        scratch_shapes=[pltpu.VMEM((8, w), F32), pltpu.VMEM((1, w), F32),
                        pltpu.VMEM((tt, w), F32), pltpu.VMEM((tt, w), F32)],
        compiler_params=_cparams("parallel", "arbitrary"),
    )(xb, gy, conv_past, h0, *consts)


def _block_mean_matrix(t):
    n_cmp = t // CMP_STRIDE - CMP_BLOCK // CMP_STRIDE + 1
    m = np.zeros((t // CMP_STRIDE, t), np.float32)
    for i in range(n_cmp):
        m[i, i * CMP_STRIDE:i * CMP_STRIDE + CMP_BLOCK] = 1.0 / CMP_BLOCK
    return m


def _sel_from_cmp(n_cmp_rows, n_cmp, n_sel):
    m = np.zeros((n_cmp_rows, n_sel), np.float32)
    for j in range(n_sel):
        for c in range(4 * j - 1, 4 * j + 4):
            if 0 <= c < n_cmp:
                m[c, j] = 1.0
    return m


def _block_diag(w):
    z = jnp.zeros((HEAD_DIM, HEAD_DIM), w.dtype)
    return jnp.concatenate([jnp.concatenate([w[0], z], axis=1), jnp.concatenate([z, w[1]], axis=1)], axis=0)


def _sigmoid(v):
    return jax.nn.sigmoid(v)


def _scale_q(v):
    return v * (HEAD_DIM ** -0.5)


_EVEN_GROUPS = (
    (0, POOL_WIDTH, ((0, POOL_WIDTH, None),)),
    (POOL_WIDTH, NSA_WIDTH, ((0, NSA_WIDTH, _scale_q),)),
    (POOL_WIDTH + NSA_WIDTH, 6 * KV_COLS, tuple((k * KV_COLS, KV_COLS, None) for k in range(6))),
    (POOL_WIDTH + NSA_WIDTH + 6 * KV_COLS, LANE, ((0, LANE, _sigmoid),)),
)
_EVEN_DTYPES = (F32, BF16) + (F32,) * 6 + (F32,)


def _even_in_proj(x2d, g, w_in_pad, tm):
    return norm_matmul(x2d, g, w_in_pad, _EVEN_GROUPS, _EVEN_DTYPES, tm)


def _xattn_block(x2d, b, t, mk, mv, g_pre, g_post, wq, wo, tm, tq):
    q, = norm_matmul(x2d, g_pre, wq, ((0, D_MODEL, ((0, D_MODEL, lambda v: v * (XATTN_HEAD_DIM ** -0.5)),)),),
                     (BF16,), tm)
    o = xattn(q.reshape(b, t, D_MODEL), mk, mv, tq).reshape(b * t, D_MODEL)
    return proj_norm_res([o], [wo], g_post, x2d, tm)


def kernel(x_prompt, mem_prompt, x_sample, state_pool, cache_cmp_k, cache_cmp_v, cache_slc_k, cache_slc_v, cache_win_k, cache_win_v, state_lru_h, state_lru_conv, cache_mem_k, cache_mem_v, page_table, norm_gain, mem_norm_gain, w_in_even, pool_w, pool_scale, w_cmp_k, w_cmp_v, w_out_even, w_in_odd, conv_w, conv_b, lru_wa, lru_ba, lru_wx, lru_bx, lru_lambda, w_out_odd, w_xq, w_xk, w_xv, w_xo, w_ffn_gate, w_ffn_up, w_ffn_down):
    bp, tp, d = x_prompt.shape
    bs, ts, _ = x_sample.shape
    depth = norm_gain.shape[0]
    n_even = w_in_even.shape[0]
    n_pages = page_table.shape[1]
    past_len = n_pages * PAGE_SIZE
    n_phys = cache_cmp_k.shape[0]
    tsp = SAMPLE_T_PAD
    np_tok, ns_tok = bp * tp, bs * tsp
    tm_p, tm_s = 512, ns_tok
    tq, tk = 256, 128

    xp = x_prompt.reshape(np_tok, d)
    xs = jnp.pad(x_sample, ((0, 0), (0, tsp - ts), (0, 0))).reshape(ns_tok, d)

    n_sub_p = tp // CMP_STRIDE
    n_cmp_p = n_sub_p - 1
    n_sel_p = tp // SEL_BLOCK
    pool_mat = jnp.asarray(_block_mean_matrix(tp).T, BF16)
    mselt = jnp.asarray(_sel_from_cmp(n_sub_p, n_cmp_p, n_sel_p).T, BF16)
    ident = jnp.asarray(np.eye(tq, dtype=np.float32), BF16)
    ident_kv = jnp.asarray(np.eye(KV_COLS, dtype=np.float32), BF16)
    subpool = jnp.asarray(np.repeat(np.eye(SUB_PER_STEP, dtype=np.float32), CMP_STRIDE, axis=0), BF16)
    steps = n_pages // PAGES_PER_STEP
    n_sub_s = past_len // CMP_STRIDE
    n_sel_s = -(-(past_len + ts) // SEL_BLOCK)
    n_lane_s = (steps + 1) * LANE
    msel_np = np.zeros((n_sub_s, n_lane_s), np.float32)
    dense = _sel_from_cmp(n_sub_s, n_sub_s, n_sel_s)
    for j in range(n_sel_s):
        msel_np[:, (j // SEL_PER_STEP) * LANE + j % SEL_PER_STEP] = dense[:, j]
    msel_s = jnp.asarray(msel_np, BF16)
    expand_np = np.zeros((LANE, PAGES_PER_STEP * PAGE_SIZE), np.float32)
    expand_np[:SEL_PER_STEP] = np.repeat(np.eye(SEL_PER_STEP, dtype=np.float32), SEL_BLOCK, axis=1)
    expand_s = jnp.asarray(expand_np, BF16)

    row = lambda v: v.reshape(1, -1)
    out_ev_p, out_ev_s, out_lru_p, out_lru_s, mem_k_p, mem_v_p = [], [], [], [], [], []

    for li in range(depth):
        gn = norm_gain[li]
        if li % 2 == 0:
            e = li // 2
            w_in = jnp.pad(w_in_even[e], ((0, 0), (0, EVEN_IN_PAD - EVEN_IN))).astype(BF16)
            pw = pool_w[e].astype(BF16)
            ps = row(pool_scale[e])
            wk_bd_t = _block_diag(w_cmp_k[e]).T.astype(BF16)
            wv_bd_t = _block_diag(w_cmp_v[e]).T.astype(BF16)
            w_out = w_out_even[e].astype(BF16)
            kv0 = POOL_WIDTH + NSA_WIDTH
            kcol = lambda k: slice(kv0 + k * KV_COLS, kv0 + (k + 1) * KV_COLS)
            w_u = w_in[:, :POOL_WIDTH]
            w_kpm = jnp.concatenate([w_in[:, kcol(2)], w_in[:, kcol(4)]], axis=1)
            w_t = w_in[:, POOL_WIDTH:kv0 + 7 * KV_COLS].T

            u, kpm, q_t, gate_t, kc, vc, ks, vs, kw, vw = even_in_prompt(xp, row(gn[0]), w_u, w_kpm, w_t,
                                                                         bp, tp, tm_p)
            u3 = u.reshape(bp, tp, POOL_WIDTH)
            a_out = pool_mix(u3, jnp.zeros((bp, POOL_HALO, POOL_WIDTH), F32), pw, ps, 0)
            kcmp, vcmp_t = compress_kv(kc, vc, pool_mat, wk_bd_t, wv_bd_t, ident_kv)
            o_nsa = nsa_prompt(q_t, gate_t, kcmp, vcmp_t, kpm.reshape(bp, tp, 2 * KV_COLS), vs, vw,
                               mselt, ident, tq, tk)
            mix_p = ([a_out, o_nsa], [w_out[:POOL_WIDTH], w_out[POOL_WIDTH:]])
            kv5 = lambda a: a.reshape(bp, NSA_KV_HEADS, HEAD_DIM, a.shape[-1]).transpose(0, 3, 1, 2)
            n_keep = min(WINDOW, tp)
            out_ev_p.append((u3[:, -POOL_STATE:], kv5(kc), kv5(vc), kv5(ks), kv5(vs),
                             kv5(kw[:, :, -n_keep:]), kv5(vw[:, :, -n_keep:])))

            u, q, kc, vc, ks, vs, kw, vw, gate = _even_in_proj(xs, row(gn[0]), w_in, tm_s)
            as3 = lambda a: a.reshape(bs, tsp, a.shape[-1])
            u3 = as3(u)
            past_pool = jnp.pad(state_pool[e], ((0, 0), (POOL_HALO - POOL_STATE, 0), (0, 0)))
            a_out = pool_mix(u3, past_pool, pw, ps, past_len)
            paged = lambda c: c.transpose(0, 1, 3, 4, 2).reshape(n_phys * n_even, KV_COLS, PAGE_SIZE)
            o_cmp, sel = sample_cmp(page_table, paged(cache_cmp_k), paged(cache_cmp_v), as3(kc), as3(vc),
                                    as3(q), wk_bd_t, wv_bd_t, msel_s, subpool, ident_kv, e, n_even, ts)
            n_win = cache_win_k.shape[2]
            win_k = cache_win_k[e].transpose(0, 2, 3, 1).reshape(bs, KV_COLS, n_win)
            win_v = cache_win_v[e].transpose(0, 2, 3, 1).reshape(bs, KV_COLS, n_win)
            o_nsa = sample_slc(page_table, paged(cache_slc_k), paged(cache_slc_v), as3(q), sel, expand_s,
                               as3(ks), as3(vs), win_k, win_v, as3(kw), as3(vw), o_cmp, as3(gate),
                               e, n_even, ts)
            xs = proj_norm_res([a_out.reshape(ns_tok, POOL_WIDTH), o_nsa.reshape(ns_tok, NSA_WIDTH)],
                               [w_out[:POOL_WIDTH], w_out[POOL_WIDTH:]], row(gn[1]), xs, tm_s)
            new4 = lambda a: as3(a)[:, :ts].reshape(bs, ts, NSA_KV_HEADS, HEAD_DIM)
            out_ev_s.append((jnp.concatenate([state_pool[e], u3[:, :ts]], axis=1)[:, -POOL_STATE:],
                             new4(kc), new4(vc), new4(ks), new4(vs),
                             jnp.concatenate([cache_win_k[e], new4(kw)], axis=1)[:, -n_win:],
                             jnp.concatenate([cache_win_v[e], new4(vw)], axis=1)[:, -n_win:]))
        else:
            o = li // 2
            w_in = w_in_odd[o].astype(BF16)
            cw = jnp.pad(conv_w[o], ((0, 8 - CONV_WIDTH), (0, 0)))
            consts = (cw, row(conv_b[o]), lru_wa[o].astype(BF16), lru_wx[o].astype(BF16),
                      row(lru_ba[o]), row(lru_bx[o]), row(lru_lambda[o]))
            w_out = w_out_odd[o].astype(BF16)
            groups = ((0, LRU_WIDTH, ((0, LRU_WIDTH, None),)),
                      (LRU_WIDTH, LRU_WIDTH, ((0, LRU_WIDTH, jax.nn.gelu),)))

            xb, yb = norm_matmul(xp, row(gn[0]), w_in, groups, (F32, F32), tm_p)
            xb3, yb3 = xb.reshape(bp, tp, LRU_WIDTH), yb.reshape(bp, tp, LRU_WIDTH)
            gated, tail = lru(xb3, yb3, jnp.zeros((bp, 8, LRU_WIDTH), F32), jnp.zeros((bp, 1, LRU_WIDTH), F32),
                              *consts, tt=256, first_pos_zero=True)
            mix_p = ([gated], [w_out])
            out_lru_p.append((xb3[:, -(CONV_WIDTH - 1):], tail[:, 7]))

            xb, yb = norm_matmul(xs, row(gn[0]), w_in, groups, (F32, F32), tm_s)
            xb3, yb3 = xb.reshape(bs, tsp, LRU_WIDTH), yb.reshape(bs, tsp, LRU_WIDTH)
            conv_past = jnp.pad(state_lru_conv[o], ((0, 0), (8 - (CONV_WIDTH - 1), 0), (0, 0)))
            gated, tail = lru(xb3, yb3, conv_past, state_lru_h[o][:, None, :], *consts, tt=tsp,
                              first_pos_zero=False)
            xs = proj_norm_res([gated.reshape(ns_tok, LRU_WIDTH)], [w_out], row(gn[1]), xs, tm_s)
            conv_new = jnp.concatenate([state_lru_conv[o], xb3[:, :ts]], axis=1)[:, -(CONV_WIDTH - 1):]
            out_lru_s.append((conv_new, tail[:, ts - 1]))

        wq, wo = w_xq[li].astype(BF16), w_xo[li].astype(BF16)
        wkv = jnp.concatenate([w_xk[li], w_xv[li]], axis=1).astype(BF16)
        mem2d = mem_prompt.reshape(bp * N_MEM, d)
        kv_groups = ((0, D_MODEL, ((0, D_MODEL, None),)), (D_MODEL, D_MODEL, ((0, D_MODEL, None),)))
        mk, mv = norm_matmul(mem2d, row(mem_norm_gain[li]), wkv, kv_groups, (F32, F32), min(tm_p, bp * N_MEM))
        mk3, mv3 = mk.reshape(bp, N_MEM, d), mv.reshape(bp, N_MEM, d)
        mem_k_p.append(mk3.reshape(bp, N_MEM, XATTN_HEADS, XATTN_HEAD_DIM))
        mem_v_p.append(mv3.reshape(bp, N_MEM, XATTN_HEADS, XATTN_HEAD_DIM))
        wg, wu, wd = w_ffn_gate[li].astype(BF16), w_ffn_up[li].astype(BF16), w_ffn_down[li].astype(BF16)
        xp = post_mixer(mix_p[0], mix_p[1], xp.reshape(bp, tp, d), gn, mk3, mv3, wq, wo, wg, wu, wd,
                        tm_p).reshape(np_tok, d)
        xs = _xattn_block(xs, bs, tsp, cache_mem_k[li].reshape(bs, N_MEM, d), cache_mem_v[li].reshape(bs, N_MEM, d),
                          row(gn[2]), row(gn[3]), wq, wo, tm_s, tsp)
        xs = ffn(xs, row(gn[4]), row(gn[5]), wg, wu, wd, tm_s)

    stack = lambda items, k, axis=0: jnp.stack([s[k] for s in items], axis=axis)
    y_prompt = xp.reshape(bp, tp, d)
    y_sample = xs.reshape(bs, tsp, d)[:, :ts]
    return (y_prompt, y_sample,
            stack(out_ev_p, 0), stack(out_ev_s, 0),
            stack(out_ev_p, 1, 1), stack(out_ev_s, 1, 1),
            stack(out_ev_p, 2, 1), stack(out_ev_s, 2, 1),
            stack(out_ev_p, 3, 1), stack(out_ev_s, 3, 1),
            stack(out_ev_p, 4, 1), stack(out_ev_s, 4, 1),
            stack(out_ev_p, 5), stack(out_ev_s, 5),
            stack(out_ev_p, 6), stack(out_ev_s, 6),
            stack(out_lru_p, 1), stack(out_lru_s, 1),
            stack(out_lru_p, 0), stack(out_lru_s, 0),
            jnp.stack(mem_k_p), jnp.stack(mem_v_p))
```

```python
import functools

import numpy as np
import jax
import jax.numpy as jnp
from jax import lax
from jax.experimental import pallas as pl
from jax.experimental.pallas import tpu as pltpu

F32 = jnp.float32
BF16 = jnp.bfloat16

D_MODEL = 1024
RMS_EPS = 1e-6
NEG_INF = -1e30
FORCE_SCORE = 1e30
REMOVED_SCORE = -3e38
LOG2E = 1.4426950408889634
M_INIT = -1e29

POOL_WIDTH = 512
POOL_WINDOWS = (2, 4, 8, 16)
POOL_GROUP = 128
POOL_STATE = 15
POOL_HALO = 16

HEAD_DIM = 64
NSA_HEADS = 8
NSA_KV_HEADS = 2
HEADS_PER_GROUP = NSA_HEADS // NSA_KV_HEADS
NSA_WIDTH = 512
KV_COLS = 128
CMP_BLOCK = 32
CMP_STRIDE = 16
SEL_BLOCK = 64
SEL_TOPK = 16
WINDOW = 512
PAGE_SIZE = 128
EVEN_IN = POOL_WIDTH + NSA_WIDTH + 6 * KV_COLS + 3 * NSA_HEADS
EVEN_IN_PAD = 1920

LRU_WIDTH = 1024
LRU_BLOCKS = 4
LRU_BLOCK = 256
CONV_WIDTH = 4
LRU_C = 8.0

N_MEM = 256
XATTN_HEADS = 4
XATTN_HEAD_DIM = 256
D_FF = 2816
FF_CHUNK = 256

SAMPLE_T_PAD = 8
PAGES_PER_STEP = 16
SEL_PER_STEP = PAGES_PER_STEP * PAGE_SIZE // SEL_BLOCK
SUB_PER_STEP = PAGES_PER_STEP * PAGE_SIZE // CMP_STRIDE
LANE = 128

VMEM_LIMIT_BYTES = 56 * 1024 * 1024


def _cparams(*sem):
    return pltpu.CompilerParams(dimension_semantics=sem, vmem_limit_bytes=VMEM_LIMIT_BYTES)


def _rms(x, g):
    return x * lax.rsqrt(jnp.mean(x * x, axis=-1, keepdims=True) + RMS_EPS) * g


def _dot(a, b):
    return jnp.dot(a, b, preferred_element_type=F32)


def _dot_nt(a, b):
    return lax.dot_general(a, b, (((1,), (1,)), ((), ())), preferred_element_type=F32)


def _split3(x):
    p1 = x.astype(BF16)
    r1 = x - p1.astype(F32)
    p2 = r1.astype(BF16)
    p3 = (r1 - p2.astype(F32)).astype(BF16)
    return p1, p2, p3


def _full(shape):
    n = len(shape)
    return pl.BlockSpec(shape, lambda *_: (0,) * n)


def _norm_matmul_kernel(x_ref, g_ref, w_ref, *o_refs, groups):
    h = _rms(x_ref[...], g_ref[...]).astype(BF16)
    k = 0
    for start, width, outs in groups:
        z = _dot(h, w_ref[:, start:start + width])
        for off, w, post in outs:
            v = z[:, off:off + w]
            if post is not None:
                v = post(v)
            o_refs[k][...] = v.astype(o_refs[k].dtype)
            k += 1


def norm_matmul(x, g, w, groups, out_dtypes, tm):
    n, d = x.shape
    widths = [w_ for _, _, outs in groups for _, w_, _ in outs]
    return pl.pallas_call(
        functools.partial(_norm_matmul_kernel, groups=groups),
        grid=(n // tm,),
        in_specs=[pl.BlockSpec((tm, d), lambda i: (i, 0)), _full(g.shape), _full(w.shape)],
        out_specs=[pl.BlockSpec((tm, wd), lambda i: (i, 0)) for wd in widths],
        out_shape=[jax.ShapeDtypeStruct((n, wd), dt) for wd, dt in zip(widths, out_dtypes)],
        compiler_params=_cparams("parallel"),
    )(x, g, w)


def _even_in_prompt_kernel(x_ref, g_ref, wu_ref, wk_ref, wt_ref, u_ref, kpm_ref, qt_ref, gatet_ref, *kvt_refs):
    h = _rms(x_ref[...], g_ref[...]).astype(BF16)
    u_ref[...] = _dot(h, wu_ref[...])
    kpm_ref[...] = _dot(h, wk_ref[...]).astype(kpm_ref.dtype)
    z_t = _dot_nt(wt_ref[...], h)
    qt_ref[0] = (z_t[0:NSA_WIDTH] * (HEAD_DIM ** -0.5 * LOG2E)).astype(qt_ref.dtype)
    for k, ref in enumerate(kvt_refs):
        ref[0] = z_t[NSA_WIDTH + k * KV_COLS:NSA_WIDTH + (k + 1) * KV_COLS, :]
    gatet_ref[0] = jax.nn.sigmoid(z_t[NSA_WIDTH + 6 * KV_COLS:, :])


def even_in_prompt(x, g, w_u, w_kpm, w_t, b, t, tm):
    n, d = x.shape
    per_seq = t // tm
    row_tile = lambda w: pl.BlockSpec((tm, w), lambda i: (i, 0))
    feat_tile = lambda f: pl.BlockSpec((1, f, tm), lambda i: (i // per_seq, 0, i % per_seq))
    n_kv = 6
    return pl.pallas_call(
        _even_in_prompt_kernel,
        grid=(n // tm,),
        in_specs=[row_tile(d), _full(g.shape), _full(w_u.shape), _full(w_kpm.shape), _full(w_t.shape)],
        out_specs=([row_tile(POOL_WIDTH), row_tile(2 * KV_COLS), feat_tile(NSA_WIDTH), feat_tile(LANE)]
                   + [feat_tile(KV_COLS)] * n_kv),
        out_shape=([jax.ShapeDtypeStruct((n, POOL_WIDTH), F32), jax.ShapeDtypeStruct((n, 2 * KV_COLS), BF16),
                    jax.ShapeDtypeStruct((b, NSA_WIDTH, t), BF16), jax.ShapeDtypeStruct((b, LANE, t), F32)]
                   + [jax.ShapeDtypeStruct((b, KV_COLS, t), F32)] * n_kv),
        compiler_params=_cparams("parallel"),
    )(x, g, w_u, w_kpm, w_t)


def _proj_norm_res_kernel(*refs, n_in):
    a_refs, w_refs = refs[:n_in], refs[n_in:2 * n_in]
    g_ref, x_ref, o_ref = refs[2 * n_in:]
    acc = _dot(a_refs[0][...], w_refs[0][...])
    for a_ref, w_ref in zip(a_refs[1:], w_refs[1:]):
        acc = acc + _dot(a_ref[...], w_ref[...])
    o_ref[...] = x_ref[...] + _rms(acc, g_ref[...])


def proj_norm_res(a_list, w_list, g, x, tm):
    n, d = x.shape
    n_in = len(a_list)
    return pl.pallas_call(
        functools.partial(_proj_norm_res_kernel, n_in=n_in),
        grid=(n // tm,),
        in_specs=([pl.BlockSpec((tm, a.shape[1]), lambda i: (i, 0)) for a in a_list]
                  + [_full(w.shape) for w in w_list]
                  + [_full(g.shape), pl.BlockSpec((tm, d), lambda i: (i, 0))]),
        out_specs=pl.BlockSpec((tm, d), lambda i: (i, 0)),
        out_shape=jax.ShapeDtypeStruct((n, d), F32),
        compiler_params=_cparams("parallel"),
    )(*a_list, *w_list, g, x)


def _ffn_block(x, g_in, g_out, wg_ref, wu_ref, wd_ref):
    h = _rms(x, g_in).astype(BF16)
    acc = None
    for c in range(D_FF // FF_CHUNK):
        cols = slice(c * FF_CHUNK, (c + 1) * FF_CHUNK)
        gate = _dot(h, wg_ref[:, cols])
        up = _dot(h, wu_ref[:, cols])
        act = (jax.nn.silu(gate) * up).astype(BF16)
        part = _dot(act, wd_ref[cols, :])
        acc = part if acc is None else acc + part
    return x + _rms(acc, g_out)


def _ffn_kernel(x_ref, g_in_ref, g_out_ref, wg_ref, wu_ref, wd_ref, o_ref):
    o_ref[...] = _ffn_block(x_ref[...], g_in_ref[...], g_out_ref[...], wg_ref, wu_ref, wd_ref)


def ffn(x, g_in, g_out, wg, wu, wd, tm):
    n, d = x.shape
    resident = functools.partial(pl.BlockSpec, pipeline_mode=pl.Buffered(1))
    return pl.pallas_call(
        _ffn_kernel,
        grid=(n // tm,),
        in_specs=[pl.BlockSpec((tm, d), lambda i: (i, 0)), _full(g_in.shape), _full(g_out.shape),
                  resident(wg.shape, lambda i: (0, 0)), resident(wu.shape, lambda i: (0, 0)),
                  resident(wd.shape, lambda i: (0, 0))],
        out_specs=pl.BlockSpec((tm, d), lambda i: (i, 0)),
        out_shape=jax.ShapeDtypeStruct((n, d), F32),
        compiler_params=_cparams("parallel"),
    )(x, g_in, g_out, wg, wu, wd)


def _xattn_heads(q, mk_ref, mv_ref):
    head_split = len(mk_ref.shape) == 4
    outs = []
    for hd in range(XATTN_HEADS):
        cols = slice(hd * XATTN_HEAD_DIM, (hd + 1) * XATTN_HEAD_DIM)
        k = (mk_ref[0, :, hd, :] if head_split else mk_ref[0, :, cols]).astype(BF16)
        v = (mv_ref[0, :, hd, :] if head_split else mv_ref[0, :, cols]).astype(BF16)
        s = _dot_nt(q[:, cols], k)
        e = jnp.exp(s - jnp.max(s, axis=-1, keepdims=True))
        p = e / jnp.sum(e, axis=-1, keepdims=True)
        outs.append(_dot(p.astype(BF16), v).astype(BF16))
    return jnp.concatenate(outs, axis=-1)


def _xattn_kernel(q_ref, mk_ref, mv_ref, o_ref):
    o_ref[0] = _xattn_heads(q_ref[0], mk_ref, mv_ref)


def xattn(q, mk, mv, layer, tq):
    b, t, d = q.shape
    mem_spec = pl.BlockSpec((None, 1) + mk.shape[2:], lambda i, j: (layer, i, 0, 0, 0))
    return pl.pallas_call(
        _xattn_kernel,
        grid=(b, t // tq),
        in_specs=[pl.BlockSpec((1, tq, d), lambda i, j: (i, j, 0)), mem_spec, mem_spec],
        out_specs=pl.BlockSpec((1, tq, d), lambda i, j: (i, j, 0)),
        out_shape=jax.ShapeDtypeStruct((b, t, d), BF16),
        compiler_params=_cparams("parallel", "parallel"),
    )(q, mk, mv)


def _post_mixer_kernel(*refs, n_in):
    a_refs, w_refs = refs[:n_in], refs[n_in:2 * n_in]
    x_ref, gn_ref, mk_ref, mv_ref, wq_ref, wo_ref, wg_ref, wu_ref, wd_ref, o_ref = refs[2 * n_in:]
    gain = lambda k: gn_ref[k:k + 1, :]
    mix = _dot(a_refs[0][0], w_refs[0][...])
    for a_ref, w_ref in zip(a_refs[1:], w_refs[1:]):
        mix = mix + _dot(a_ref[0], w_ref[...])
    x1 = x_ref[0] + _rms(mix, gain(1))
    q = (_dot(_rms(x1, gain(2)).astype(BF16), wq_ref[...]) * (XATTN_HEAD_DIM ** -0.5)).astype(BF16)
    attn = _xattn_heads(q, mk_ref, mv_ref)
    x2 = x1 + _rms(_dot(attn, wo_ref[...]), gain(3))
    o_ref[0] = _ffn_block(x2, gain(4), gain(5), wg_ref, wu_ref, wd_ref)


def post_mixer(a_list, w_list, x, gn, mk, mv, wq, wo, wg, wu, wd, tm):
    b, t, d = x.shape
    n_in = len(a_list)
    resident = lambda a: pl.BlockSpec(a.shape, lambda i, j: (0,) * a.ndim, pipeline_mode=pl.Buffered(1))
    tile = lambda w: pl.BlockSpec((1, tm, w), lambda i, j: (i, j, 0))
    per_b = lambda a: pl.BlockSpec((1,) + a.shape[1:], lambda i, j: (i, 0, 0))
    return pl.pallas_call(
        functools.partial(_post_mixer_kernel, n_in=n_in),
        grid=(b, t // tm),
        in_specs=([tile(a.shape[2]) for a in a_list] + [resident(w) for w in w_list]
                  + [tile(d), resident(gn), per_b(mk), per_b(mv)]
                  + [resident(w) for w in (wq, wo, wg, wu, wd)]),
        out_specs=tile(d),
        out_shape=jax.ShapeDtypeStruct((b, t, d), F32),
        compiler_params=_cparams("parallel", "parallel"),
    )(*a_list, *w_list, x, gn, mk, mv, wq, wo, wg, wu, wd)


def _pool_kernel(u_ref, past_ref, w_ref, scale_ref, o_ref, buf, *, t, chunk, start_pos):
    buf[0:POOL_HALO, :] = past_ref[0]
    buf[POOL_HALO:POOL_HALO + t, :] = u_ref[0]
    for c in range(t // chunk):
        base = POOL_HALO + c * chunk
        pos = start_pos + c * chunk + lax.broadcasted_iota(jnp.int32, (chunk, 1), 0)
        for gi, win in enumerate(POOL_WINDOWS):
            cols = slice(gi * POOL_GROUP, (gi + 1) * POOL_GROUP)
            cur = buf[base:base + chunk, cols]
            win_sum = cur
            for k in range(1, win):
                win_sum = win_sum + buf[base - k:base - k + chunk, cols]
            cnt = jnp.minimum(pos + 1, win).astype(F32)
            diff = win_sum / cnt - cur
            y = _dot(diff.astype(BF16), w_ref[gi]) * scale_ref[:, cols]
            o_ref[0, c * chunk:(c + 1) * chunk, cols] = y.astype(o_ref.dtype)


def pool_mix(u, past, w, scale, start_pos):
    b, t, c = u.shape
    chunk = min(t, 256)
    return pl.pallas_call(
        functools.partial(_pool_kernel, t=t, chunk=chunk, start_pos=start_pos),
        grid=(b,),
        in_specs=[pl.BlockSpec((1, t, c), lambda i: (i, 0, 0)),
                  pl.BlockSpec((1, POOL_HALO, c), lambda i: (i, 0, 0)),
                  _full(w.shape), _full(scale.shape)],
        out_specs=pl.BlockSpec((1, t, c), lambda i: (i, 0, 0)),
        out_shape=jax.ShapeDtypeStruct((b, t, c), BF16),
        scratch_shapes=[pltpu.VMEM((POOL_HALO + t, c), F32)],
        compiler_params=_cparams("parallel"),
    )(u, past, w, scale)


def _compress_kernel(kc_ref, vc_ref, pool_ref, wkt_ref, wvt_ref, ident_ref, k_ref, vt_ref):
    def compress(src, wt_ref):
        x = src[0]
        hi = x.astype(BF16)
        lo = (x - hi.astype(F32)).astype(BF16)
        mean_t = _dot(hi, pool_ref[...]) + _dot(lo, pool_ref[...])
        return _dot(wt_ref[...], mean_t.astype(BF16)).astype(BF16)
    k_ref[0] = _dot_nt(ident_ref[...], compress(kc_ref, wkt_ref)).astype(BF16)
    vt_ref[0] = compress(vc_ref, wvt_ref)


def compress_kv(kc_t, vc_t, pool_mat, wk_bd_t, wv_bd_t, ident):
    b, c, t = kc_t.shape
    n_cmp = pool_mat.shape[1]
    per_b = pl.BlockSpec((1, c, t), lambda i: (i, 0, 0))
    return pl.pallas_call(
        _compress_kernel,
        grid=(b,),
        in_specs=[per_b, per_b, _full(pool_mat.shape), _full(wk_bd_t.shape), _full(wv_bd_t.shape),
                  _full(ident.shape)],
        out_specs=[pl.BlockSpec((1, n_cmp, c), lambda i: (i, 0, 0)), pl.BlockSpec((1, c, n_cmp), lambda i: (i, 0, 0))],
        out_shape=[jax.ShapeDtypeStruct((b, n_cmp, c), BF16), jax.ShapeDtypeStruct((b, c, n_cmp), BF16)],
        compiler_params=_cparams("parallel"),
    )(kc_t, vc_t, pool_mat, wk_bd_t, wv_bd_t, ident)


def _masked_softmax_rows(s, mask):
    s = jnp.where(mask, s, NEG_INF)
    e = jnp.where(mask, jnp.exp(s - jnp.max(s, axis=-1, keepdims=True)), 0.0)
    return e / jnp.maximum(jnp.sum(e, axis=-1, keepdims=True), 1e-30)


def _flash_step(raw_ref, v_t, mask, m_ref, l_ref, acc_ref):
    tq = mask.shape[1]
    bias = jnp.where(mask, 0.0, NEG_INF)
    for hh in range(HEADS_PER_GROUP):
        lanes = slice(hh * tq, (hh + 1) * tq)
        s = raw_ref[:, lanes] + bias
        m_old = m_ref[:, lanes]
        m_new = jnp.maximum(m_old, jnp.max(s, axis=0, keepdims=True))
        alpha = jnp.exp2(m_old - m_new)
        e = jnp.exp2(s - m_new)
        m_ref[:, lanes] = m_new
        l_ref[:, lanes] = alpha * l_ref[:, lanes] + jnp.sum(e, axis=0, keepdims=True)
        acc_ref[:, lanes] = alpha * acc_ref[:, lanes] + _dot(v_t, e.astype(BF16))


def _nsa_prompt_kernel(qt_ref, gatet_ref, kcmp_ref, vcmpt_ref, kpm_ref, vst_ref, vwt_ref,
                       mselt_ref, ident_ref, o_ref, sel_s, raw_s, m_s, l_s, acc_s, *, tq, tk):
    i = pl.program_id(1)
    q0 = i * tq
    n_cmp = kcmp_ref.shape[1]
    n_sel = mselt_ref.shape[0]
    blocks_per_tile = tk // SEL_BLOCK
    tiles_per_q = tq // tk
    qpos_row = q0 + lax.broadcasted_iota(jnp.int32, (1, tq), 1)
    key_col = lax.broadcasted_iota(jnp.int32, (tk, 1), 0)
    cmp_end_col = lax.broadcasted_iota(jnp.int32, (n_cmp, 1), 0) * CMP_STRIDE + (CMP_BLOCK - 1)
    cmp_bias = jnp.where(cmp_end_col <= qpos_row, 0.0, NEG_INF)
    sel_j = lax.broadcasted_iota(jnp.int32, (n_sel, 1), 0)
    cur_row = qpos_row // SEL_BLOCK
    gate_t = gatet_ref[0]
    zero_half = jnp.zeros((HEAD_DIM, HEADS_PER_GROUP * tq), BF16)

    groups = range(NSA_KV_HEADS)
    grows = [slice(g * HEAD_DIM, (g + 1) * HEAD_DIM) for g in groups]
    q_pads, o_cmps = [], []
    for g in groups:
        q_t_g = jnp.concatenate(
            [qt_ref[0, (g * HEADS_PER_GROUP + hh) * HEAD_DIM:(g * HEADS_PER_GROUP + hh + 1) * HEAD_DIM, :]
             for hh in range(HEADS_PER_GROUP)], axis=1)
        q_pad = jnp.concatenate([q_t_g, zero_half] if g == 0 else [zero_half, q_t_g], axis=0)
        q_pads.append(q_pad)

        s_t = _dot(kcmp_ref[0], q_pad) + jnp.concatenate([cmp_bias] * HEADS_PER_GROUP, axis=1)
        e_t = jnp.exp2(s_t - jnp.maximum(jnp.max(s_t, axis=0, keepdims=True), M_INIT))
        p_t = e_t / jnp.maximum(jnp.sum(e_t, axis=0, keepdims=True), 1e-30)
        o_cmps.append(_dot(vcmpt_ref[0, grows[g], :], p_t.astype(BF16)))
        psum_t = p_t[:, 0:tq]
        for hh in range(1, HEADS_PER_GROUP):
            psum_t = psum_t + p_t[:, hh * tq:(hh + 1) * tq]

        imp_t = None
        for part in _split3(psum_t):
            d = _dot(mselt_ref[...], part)
            imp_t = d if imp_t is None else imp_t + d
        valid = sel_j * SEL_BLOCK <= qpos_row
        forced = (sel_j == 0) | (sel_j == cur_row) | (sel_j == cur_row - 1)
        score = jnp.where(forced, FORCE_SCORE, jnp.where(valid, imp_t, NEG_INF))
        rank = jnp.zeros((n_sel, tq), F32)
        for jp in range(n_sel):
            sj = score[jp:jp + 1, :]
            beats = (sj > score) | ((sj == score) & (sel_j > jp))
            rank = rank + jnp.where(beats, 1.0, 0.0)
        sel_s[g] = jnp.where(rank < SEL_TOPK, 1.0, 0.0)

    m_s[...] = jnp.full(m_s.shape, M_INIT, F32)
    l_s[...] = jnp.zeros(l_s.shape, F32)
    acc_s[...] = jnp.zeros(acc_s.shape, F32)
    n_tiles = (i + 1) * tiles_per_q
    first_win = jnp.maximum((i * tq - (WINDOW - 1)) // tk, 0)
    key_rows = lambda kt: pl.ds(pl.multiple_of(kt * tk, tk), tk)

    def issue_scores(kt, slot, branch):
        k = kpm_ref[0, key_rows(kt), branch * KV_COLS:(branch + 1) * KV_COLS]
        for g in groups:
            raw_s[slot, NSA_KV_HEADS * branch + g] = _dot(k, q_pads[g])

    def slc_update(kt, slot):
        causal = kt * tk + key_col <= qpos_row
        for g in groups:
            chosen = jnp.concatenate(
                [jnp.broadcast_to(sel_s[g, pl.ds(kt * blocks_per_tile + r, 1), :], (SEL_BLOCK, tq))
                 for r in range(blocks_per_tile)], axis=0)
            v_t = vst_ref[0, grows[g], key_rows(kt)].astype(BF16)
            _flash_step(raw_s.at[slot, g], v_t, (chosen > 0.5) & causal, m_s.at[g], l_s.at[g], acc_s.at[g])

    def win_update(kt, slot):
        dist = qpos_row - (kt * tk + key_col)
        mask = (dist >= 0) & (dist < WINDOW)
        for g in groups:
            c = NSA_KV_HEADS + g
            v_t = vwt_ref[0, grows[g], key_rows(kt)].astype(BF16)
            _flash_step(raw_s.at[slot, c], v_t, mask, m_s.at[c], l_s.at[c], acc_s.at[c])

    def far_body(pair, _):
        for slot in (0, 1):
            kt = 2 * pair + slot
            issue_scores(kt + 1, 1 - slot, 0)
            slc_update(kt, slot)
        return 0

    def band_body(pair, _):
        for slot in (0, 1):
            kt = 2 * pair + slot
            nxt = jnp.minimum(kt + 1, n_tiles - 1)
            issue_scores(nxt, 1 - slot, 0)
            issue_scores(nxt, 1 - slot, 1)
            slc_update(kt, slot)
            win_update(kt, slot)
        return 0

    issue_scores(0, 0, 0)
    lax.fori_loop(0, first_win // 2, far_body, 0)
    issue_scores(first_win, 0, 1)
    lax.fori_loop(first_win // 2, n_tiles // 2, band_body, 0)

    head_out = [None] * NSA_HEADS
    for g in groups:
        o_slc = acc_s[g] / l_s[g]
        o_win = acc_s[NSA_KV_HEADS + g] / l_s[NSA_KV_HEADS + g]
        for hh in range(HEADS_PER_GROUP):
            h = g * HEADS_PER_GROUP + hh
            lanes = slice(hh * tq, (hh + 1) * tq)
            head_out[h] = (gate_t[3 * h:3 * h + 1, :] * o_cmps[g][:, lanes]
                           + gate_t[3 * h + 1:3 * h + 2, :] * o_slc[:, lanes]
                           + gate_t[3 * h + 2:3 * h + 3, :] * o_win[:, lanes])
    o_t = jnp.concatenate(head_out, axis=0).astype(BF16)
    o_ref[0] = _dot_nt(ident_ref[...], o_t).astype(o_ref.dtype)


def nsa_prompt(q_t, gate_t, kcmp, vcmp_t, kpm, vs_t, vw_t, mselt, ident, tq, tk):
    b, _, t = q_t.shape
    feat_tile = lambda f: pl.BlockSpec((1, f, tq), lambda i, j: (i, 0, j))
    whole = lambda a: pl.BlockSpec((1,) + a.shape[1:], lambda i, j: (i, 0, 0))
    per_batch = [kcmp, vcmp_t, kpm, vs_t, vw_t]
    n_chain = 2 * NSA_KV_HEADS
    assert tq % (2 * tk) == 0 and WINDOW % (2 * tk) == 0 and tk % SEL_BLOCK == 0
    return pl.pallas_call(
        functools.partial(_nsa_prompt_kernel, tq=tq, tk=tk),
        grid=(b, t // tq),
        in_specs=([feat_tile(NSA_WIDTH), feat_tile(LANE)] + [whole(a) for a in per_batch]
                  + [_full(mselt.shape), _full(ident.shape)]),
        out_specs=pl.BlockSpec((1, tq, NSA_WIDTH), lambda i, j: (i, j, 0)),
        out_shape=jax.ShapeDtypeStruct((b, t, NSA_WIDTH), BF16),
        scratch_shapes=[pltpu.VMEM((NSA_KV_HEADS, mselt.shape[0], tq), F32),
                        pltpu.VMEM((2, n_chain, tk, HEADS_PER_GROUP * tq), F32),
                        pltpu.VMEM((n_chain, 1, HEADS_PER_GROUP * tq), F32),
                        pltpu.VMEM((n_chain, 1, HEADS_PER_GROUP * tq), F32),
                        pltpu.VMEM((n_chain, HEAD_DIM, HEADS_PER_GROUP * tq), F32)],
        compiler_params=_cparams("parallel", "arbitrary"),
    )(q_t, gate_t, *per_batch, mselt, ident)


def _stack_heads(q, g):
    return jnp.concatenate(
        [q[:, (g * HEADS_PER_GROUP + hh) * HEAD_DIM:(g * HEADS_PER_GROUP + hh + 1) * HEAD_DIM]
         for hh in range(HEADS_PER_GROUP)], axis=0)


def _sample_cmp_kernel(pt_ref, *refs, n_valid, past_len):
    del pt_ref
    p = PAGES_PER_STEP
    tp = SAMPLE_T_PAD
    k_pages, v_pages = refs[:p], refs[p:2 * p]
    (kn_ref, vn_ref, q_ref, wkt_ref, wvt_ref, msel_ref, subpool_ref, ident_ref,
     ocmp_ref, sel_ref, sub_k, sub_v) = refs[2 * p:]
    c = pl.program_id(1)
    n_sub = past_len // CMP_STRIDE
    steps = past_len // (p * PAGE_SIZE)

    for pages, sub in ((k_pages, sub_k), (v_pages, sub_v)):
        x = jnp.concatenate([r[...] for r in pages], axis=1)
        hi = x.astype(BF16)
        lo = (x - hi.astype(F32)).astype(BF16)
        sub[c] = _dot(hi, subpool_ref[...]) + _dot(lo, subpool_ref[...])

    @pl.when(c == steps - 1)
    def _():
        t_col = lax.broadcasted_iota(jnp.int32, (tp, 1), 0)
        new_valid = lax.broadcasted_iota(jnp.int32, (1, tp), 1) < n_valid
        first_lane = lax.broadcasted_iota(jnp.int32, (1, SUB_PER_STEP), 1) == 0
        cmp_kv = []
        for new_ref, sub, wt_ref in ((kn_ref, sub_k, wkt_ref), (vn_ref, sub_v, wvt_ref)):
            new_t = None
            for part in _split3(new_ref[0]):
                d = _dot_nt(ident_ref[...], part)
                new_t = d if new_t is None else new_t + d
            new_sum = jnp.sum(jnp.where(new_valid, new_t, 0.0), axis=1, keepdims=True)
            sub[steps] = jnp.where(first_lane, new_sum, 0.0)
            sums = jnp.concatenate([sub[s] for s in range(steps + 1)], axis=1)
            mean_t = (sums[:, 0:n_sub] + sums[:, 1:n_sub + 1]) * (1.0 / CMP_BLOCK)
            cmp_kv.append(_dot(wt_ref[...], mean_t.astype(BF16)).astype(BF16))
        kcmp_t, vcmp_t = cmp_kv

        rows = HEADS_PER_GROUP * tp
        qpos_col = past_len + lax.broadcasted_iota(jnp.int32, (rows, 1), 0) % tp
        cmp_end_row = lax.broadcasted_iota(jnp.int32, (1, n_sub), 1) * CMP_STRIDE + (CMP_BLOCK - 1)
        cmp_mask = cmp_end_row <= qpos_col

        n_lane = sel_ref.shape[2]
        lane = lax.broadcasted_iota(jnp.int32, (1, n_lane), 1)
        sel_j = (lane // LANE) * SEL_PER_STEP + lane % LANE
        lane_used = lane % LANE < SEL_PER_STEP
        qpos_t = past_len + t_col
        cur = qpos_t // SEL_BLOCK
        valid = lane_used & (sel_j * SEL_BLOCK <= qpos_t)
        forced = lane_used & ((sel_j == 0) | (sel_j == cur) | (sel_j == cur - 1))

        q = q_ref[0]
        o_heads = [None] * NSA_HEADS
        scores = []
        for g in range(NSA_KV_HEADS):
            gcols = slice(g * HEAD_DIM, (g + 1) * HEAD_DIM)
            prob = _masked_softmax_rows(_dot(_stack_heads(q, g), kcmp_t[gcols, :]), cmp_mask)
            o_g = _dot_nt(prob.astype(BF16), vcmp_t[gcols, :])
            psum = prob[0:tp]
            for hh in range(1, HEADS_PER_GROUP):
                psum = psum + prob[hh * tp:(hh + 1) * tp]
                o_heads[g * HEADS_PER_GROUP + hh] = o_g[hh * tp:(hh + 1) * tp]
            o_heads[g * HEADS_PER_GROUP] = o_g[0:tp]
            imp = None
            for part in _split3(psum):
                d = _dot(part, msel_ref[...])
                imp = d if imp is None else imp + d
            scores.append(jnp.where(forced, FORCE_SCORE, jnp.where(valid, imp, NEG_INF)))
        score = jnp.concatenate(scores, axis=0)
        chosen = jnp.zeros(score.shape, F32)
        for _ in range(SEL_TOPK):
            best = jnp.max(score, axis=-1, keepdims=True)
            first = jnp.min(jnp.where(score == best, lane, n_lane), axis=-1, keepdims=True)
            hit = lane == first
            chosen = jnp.where(hit, 1.0, chosen)
            score = jnp.where(hit, REMOVED_SCORE, score)
        sel_ref[0] = chosen
        ocmp_ref[0] = jnp.concatenate(o_heads, axis=-1)


def _page_spec(k, layer, n_layers):
    return pl.BlockSpec((None, KV_COLS, PAGE_SIZE),
                        lambda i, c, pt: (pt[i, c * PAGES_PER_STEP + k] * n_layers + layer, 0, 0))


def sample_cmp(page_table, cache_k, cache_v, kc_new, vc_new, q, wk_bd_t, wv_bd_t, msel, subpool, ident,
               layer, n_layers, n_valid):
    b, n_pages = page_table.shape
    past_len = n_pages * PAGE_SIZE
    steps = n_pages // PAGES_PER_STEP
    n_lane = msel.shape[1]
    per_b = lambda a: pl.BlockSpec((1,) + a.shape[1:], lambda i, c, pt: (i, 0, 0))
    const = lambda a: pl.BlockSpec(a.shape, lambda i, c, pt: (0,) * a.ndim)
    consts = [wk_bd_t, wv_bd_t, msel, subpool, ident]
    grid_spec = pltpu.PrefetchScalarGridSpec(
        num_scalar_prefetch=1,
        grid=(b, steps),
        in_specs=([_page_spec(k, layer, n_layers) for k in range(PAGES_PER_STEP)] * 2
                  + [per_b(kc_new), per_b(vc_new), per_b(q)] + [const(a) for a in consts]),
        out_specs=[pl.BlockSpec((1, SAMPLE_T_PAD, NSA_WIDTH), lambda i, c, pt: (i, 0, 0)),
                   pl.BlockSpec((1, 2 * SAMPLE_T_PAD, n_lane), lambda i, c, pt: (i, 0, 0))],
        scratch_shapes=[pltpu.VMEM((steps + 1, KV_COLS, SUB_PER_STEP), F32)] * 2,
    )
    return pl.pallas_call(
        functools.partial(_sample_cmp_kernel, n_valid=n_valid, past_len=past_len),
        grid_spec=grid_spec,
        out_shape=[jax.ShapeDtypeStruct((b, SAMPLE_T_PAD, NSA_WIDTH), F32),
                   jax.ShapeDtypeStruct((b, 2 * SAMPLE_T_PAD, n_lane), F32)],
        compiler_params=_cparams("parallel", "arbitrary"),
    )(page_table, *([cache_k] * PAGES_PER_STEP), *([cache_v] * PAGES_PER_STEP),
      kc_new, vc_new, q, *consts)


def _sample_slc_kernel(pt_ref, *refs, n_valid, past_len):
    del pt_ref
    p = PAGES_PER_STEP
    tp = SAMPLE_T_PAD
    rows = HEADS_PER_GROUP * tp
    k_pages, v_pages = refs[:p], refs[p:2 * p]
    (q_ref, sel_ref, expand_ref, ksn_ref, vsn_ref, wk_ref, wv_ref, kwn_ref, vwn_ref, ocmp_ref, gate_ref,
     o_ref, m_s, l_s, acc_s) = refs[2 * p:]
    c = pl.program_id(1)
    q = q_ref[0]
    zero = jnp.zeros((rows, HEAD_DIM), BF16)
    q_bd = jnp.concatenate([jnp.concatenate([_stack_heads(q, 0), zero], axis=1),
                            jnp.concatenate([zero, _stack_heads(q, 1)], axis=1)], axis=0)
    t_col = lax.broadcasted_iota(jnp.int32, (NSA_KV_HEADS * rows, 1), 0) % tp
    new_row = lax.broadcasted_iota(jnp.int32, (1, tp), 1)
    new_mask = (new_row <= t_col) & (new_row < n_valid)

    @pl.when(c == 0)
    def _():
        m_s[...] = jnp.full(m_s.shape, NEG_INF, F32)
        l_s[...] = jnp.zeros(l_s.shape, F32)
        acc_s[...] = jnp.zeros(acc_s.shape, F32)

    k_all_t = jnp.concatenate([r[...] for r in k_pages], axis=1).astype(BF16)
    v_all_t = jnp.concatenate([r[...] for r in v_pages], axis=1).astype(BF16)
    chosen = _dot(sel_ref[0].astype(BF16), expand_ref[...])
    keymask = jnp.concatenate([chosen[0:tp]] * HEADS_PER_GROUP + [chosen[tp:2 * tp]] * HEADS_PER_GROUP,
                              axis=0) > 0.5
    s = jnp.where(keymask, _dot(q_bd, k_all_t), NEG_INF)
    m_old = m_s[...]
    m_new = jnp.maximum(m_old, jnp.max(s, axis=-1, keepdims=True))
    alpha = jnp.exp(m_old - m_new)
    e = jnp.where(keymask, jnp.exp(s - m_new), 0.0)
    l_s[...] = alpha * l_s[...] + jnp.sum(e, axis=-1, keepdims=True)
    acc_s[...] = alpha * acc_s[...] + _dot_nt(e.astype(BF16), v_all_t)
    m_s[...] = m_new

    @pl.when(c == pl.num_programs(1) - 1)
    def _():
        n_win = wk_ref.shape[2]
        win_row = lax.broadcasted_iota(jnp.int32, (1, n_win), 1)
        dist = t_col + n_win - win_row
        win_mask = (dist >= 0) & (dist < WINDOW)
        s_n = jnp.where(new_mask, _dot_nt(q_bd, ksn_ref[0].astype(BF16)), NEG_INF)
        m_fin = jnp.maximum(m_new, jnp.max(s_n, axis=-1, keepdims=True))
        a_fin = jnp.exp(m_new - m_fin)
        e_n = jnp.where(new_mask, jnp.exp(s_n - m_fin), 0.0)
        l_fin = a_fin * l_s[...] + jnp.sum(e_n, axis=-1, keepdims=True)
        o_slc = (a_fin * acc_s[...] + _dot(e_n.astype(BF16), vsn_ref[0].astype(BF16))) / l_fin
        s_p = jnp.where(win_mask, _dot(q_bd, wk_ref[0].astype(BF16)), NEG_INF)
        s_w = jnp.where(new_mask, _dot_nt(q_bd, kwn_ref[0].astype(BF16)), NEG_INF)
        m_w = jnp.maximum(jnp.max(s_p, axis=-1, keepdims=True), jnp.max(s_w, axis=-1, keepdims=True))
        e_p = jnp.where(win_mask, jnp.exp(s_p - m_w), 0.0)
        e_w = jnp.where(new_mask, jnp.exp(s_w - m_w), 0.0)
        l_w = jnp.sum(e_p, axis=-1, keepdims=True) + jnp.sum(e_w, axis=-1, keepdims=True)
        o_win = (_dot_nt(e_p.astype(BF16), wv_ref[0].astype(BF16))
                 + _dot(e_w.astype(BF16), vwn_ref[0].astype(BF16))) / l_w
        gate_all = gate_ref[0]
        head_out = []
        for h in range(NSA_HEADS):
            g, hh = divmod(h, HEADS_PER_GROUP)
            r = slice(g * rows + hh * tp, g * rows + (hh + 1) * tp)
            cols = slice(g * HEAD_DIM, (g + 1) * HEAD_DIM)
            gate = gate_all[:, 3 * h:3 * h + 3]
            head_out.append(gate[:, 0:1] * ocmp_ref[0, :, h * HEAD_DIM:(h + 1) * HEAD_DIM]
                            + gate[:, 1:2] * o_slc[r, cols] + gate[:, 2:3] * o_win[r, cols])
        o_ref[0] = jnp.concatenate(head_out, axis=-1).astype(o_ref.dtype)


def sample_slc(page_table, cache_k, cache_v, q, sel, expand, ks_new, vs_new, win_k, win_v, kw_new, vw_new,
               o_cmp, gate, layer, n_layers, n_valid):
    b, n_pages = page_table.shape
    past_len = n_pages * PAGE_SIZE
    steps = n_pages // PAGES_PER_STEP
    rows = HEADS_PER_GROUP * SAMPLE_T_PAD
    per_b = lambda a: pl.BlockSpec((1,) + a.shape[1:], lambda i, c, pt: (i, 0, 0))
    const = lambda a: pl.BlockSpec(a.shape, lambda i, c, pt: (0,) * a.ndim)
    grid_spec = pltpu.PrefetchScalarGridSpec(
        num_scalar_prefetch=1,
        grid=(b, steps),
        in_specs=([_page_spec(k, layer, n_layers) for k in range(PAGES_PER_STEP)] * 2
                  + [per_b(q), pl.BlockSpec((1, 2 * SAMPLE_T_PAD, LANE), lambda i, c, pt: (i, 0, c)),
                     const(expand), per_b(ks_new), per_b(vs_new), per_b(win_k), per_b(win_v),
                     per_b(kw_new), per_b(vw_new), per_b(o_cmp), per_b(gate)]),
        out_specs=pl.BlockSpec((1, SAMPLE_T_PAD, NSA_WIDTH), lambda i, c, pt: (i, 0, 0)),
        scratch_shapes=[pltpu.VMEM((NSA_KV_HEADS * rows, 1), F32), pltpu.VMEM((NSA_KV_HEADS * rows, 1), F32),
                        pltpu.VMEM((NSA_KV_HEADS * rows, KV_COLS), F32)],
    )
    return pl.pallas_call(
        functools.partial(_sample_slc_kernel, n_valid=n_valid, past_len=past_len),
        grid_spec=grid_spec,
        out_shape=jax.ShapeDtypeStruct((b, SAMPLE_T_PAD, NSA_WIDTH), BF16),
        compiler_params=_cparams("parallel", "arbitrary"),
    )(page_table, *([cache_k] * PAGES_PER_STEP), *([cache_v] * PAGES_PER_STEP),
      q, sel, expand, ks_new, vs_new, win_k, win_v, kw_new, vw_new, o_cmp, gate)


def _lru_kernel(xb_ref, gy_ref, cpast_ref, h0_ref, cw_ref, cb_ref, wa_ref, wx_ref, ba_ref, bx_ref, lam_ref,
                o_ref, tail_ref, xbuf, h_carry, a_s, d_s, *, tt, first_pos_zero):
    j = pl.program_id(1)

    @pl.when(j == 0)
    def _():
        xbuf[...] = cpast_ref[0]
        h_carry[...] = h0_ref[0]

    sub = lax.broadcasted_iota(jnp.int32, (8, LRU_WIDTH), 0)
    x_cur = xb_ref[0]
    prev8 = xbuf[...]

    def delayed(k):
        if k == 0:
            return x_cur
        rolled = pltpu.roll(x_cur, k, 0)
        head = jnp.where(sub < k, pltpu.roll(prev8, k, 0), rolled[0:8])
        return jnp.concatenate([head, rolled[8:]], axis=0) if tt > 8 else head

    xc = delayed(CONV_WIDTH - 1) * cw_ref[0:1, :]
    for k in range(1, CONV_WIDTH):
        xc = xc + delayed(CONV_WIDTH - 1 - k) * cw_ref[k:k + 1, :]
    xf = xc + cb_ref[...]
    xf_b = xf.astype(BF16)
    r_parts, i_parts = [], []
    for n in range(LRU_BLOCKS):
        cols = slice(n * LRU_BLOCK, (n + 1) * LRU_BLOCK)
        r_parts.append(_dot(xf_b[:, cols], wa_ref[n]))
        i_parts.append(_dot(xf_b[:, cols], wx_ref[n]))
    r = jax.nn.sigmoid(jnp.concatenate(r_parts, axis=-1) + ba_ref[...])
    gate_i = jax.nn.sigmoid(jnp.concatenate(i_parts, axis=-1) + bx_ref[...])
    neg_lam = -lam_ref[...]
    softplus = jnp.maximum(neg_lam, 0.0) + jnp.log1p(jnp.exp(-jnp.abs(neg_lam)))
    log_a = -LRU_C * r * softplus
    a = jnp.exp(log_a)
    mult = jnp.sqrt(-jnp.tanh(log_a) * (a * a + 1.0))
    if first_pos_zero:
        row = lax.broadcasted_iota(jnp.int32, (tt, 1), 0)
        mult = jnp.where((row == 0) & (j == 0), 1.0, mult)
    a_s[...] = a
    d_s[...] = mult * gate_i * xf

    def body(blk, h):
        rows = pl.ds(pl.multiple_of(blk * 8, 8), 8)
        a8 = a_s[rows, :]
        d8 = d_s[rows, :]
        for s in (1, 2, 4):
            keep = sub >= s
            d8 = jnp.where(keep, a8 * pltpu.roll(d8, s, 0) + d8, d8)
            a8 = jnp.where(keep, a8 * pltpu.roll(a8, s, 0), a8)
        hs = a8 * h + d8
        d_s[rows, :] = hs
        return hs[7:8, :]

    n_blk = tt // 8
    h_carry[...] = lax.fori_loop(0, n_blk, body, h_carry[...], unroll=min(4, n_blk))
    hs = d_s[...]
    o_ref[0] = (hs * gy_ref[0]).astype(o_ref.dtype)
    tail_ref[0] = d_s[tt - 8:tt, :]
    xbuf[...] = x_cur[tt - 8:tt, :]


def lru(xb, gy, conv_past, h0, cw, cb, wa, wx, ba, bx, lam, tt, first_pos_zero):
    b, t, w = xb.shape
    tile = pl.BlockSpec((1, tt, w), lambda i, j: (i, j, 0))
    per_b = lambda a: pl.BlockSpec((1,) + a.shape[1:], lambda i, j: (i, 0, 0))
    consts = [cw, cb, wa, wx, ba, bx, lam]
    return pl.pallas_call(
        functools.partial(_lru_kernel, tt=tt, first_pos_zero=first_pos_zero),
        grid=(b, t // tt),
        in_specs=[tile, tile, per_b(conv_past), per_b(h0)] + [_full(a.shape) for a in consts],
        out_specs=[tile, pl.BlockSpec((1, 8, w), lambda i, j: (i, 0, 0))],
        out_shape=[jax.ShapeDtypeStruct((b, t, w), BF16), jax.ShapeDtypeStruct((b, 8, w), F32)],
        scratch_shapes=[pltpu.VMEM((8, w), F32), pltpu.VMEM((1, w), F32),
                        pltpu.VMEM((tt, w), F32), pltpu.VMEM((tt, w), F32)],
        compiler_params=_cparams("parallel", "arbitrary"),
    )(xb, gy, conv_past, h0, *consts)


def _block_mean_matrix(t):
    n_cmp = t // CMP_STRIDE - CMP_BLOCK // CMP_STRIDE + 1
    m = np.zeros((t // CMP_STRIDE, t), np.float32)
    for i in range(n_cmp):
        m[i, i * CMP_STRIDE:i * CMP_STRIDE + CMP_BLOCK] = 1.0 / CMP_BLOCK
    return m


def _sel_from_cmp(n_cmp_rows, n_cmp, n_sel):
    m = np.zeros((n_cmp_rows, n_sel), np.float32)
    for j in range(n_sel):
        for c in range(4 * j - 1, 4 * j + 4):
            if 0 <= c < n_cmp:
                m[c, j] = 1.0
    return m


def _block_diag(w):
    z = jnp.zeros((HEAD_DIM, HEAD_DIM), w.dtype)
    return jnp.concatenate([jnp.concatenate([w[0], z], axis=1), jnp.concatenate([z, w[1]], axis=1)], axis=0)


def _sigmoid(v):
    return jax.nn.sigmoid(v)


def _scale_q(v):
    return v * (HEAD_DIM ** -0.5)


_EVEN_GROUPS = (
    (0, POOL_WIDTH, ((0, POOL_WIDTH, None),)),
    (POOL_WIDTH, NSA_WIDTH, ((0, NSA_WIDTH, _scale_q),)),
    (POOL_WIDTH + NSA_WIDTH, 6 * KV_COLS, tuple((k * KV_COLS, KV_COLS, None) for k in range(6))),
    (POOL_WIDTH + NSA_WIDTH + 6 * KV_COLS, LANE, ((0, LANE, _sigmoid),)),
)
_EVEN_DTYPES = (F32, BF16) + (F32,) * 6 + (F32,)


def _even_in_proj(x2d, g, w_in_pad, tm):
    return norm_matmul(x2d, g, w_in_pad, _EVEN_GROUPS, _EVEN_DTYPES, tm)


def _xattn_block(x2d, b, t, mk, mv, layer, g_pre, g_post, wq, wo, tm, tq):
    q, = norm_matmul(x2d, g_pre, wq, ((0, D_MODEL, ((0, D_MODEL, lambda v: v * (XATTN_HEAD_DIM ** -0.5)),)),),
                     (BF16,), tm)
    o = xattn(q.reshape(b, t, D_MODEL), mk, mv, layer, tq).reshape(b * t, D_MODEL)
    return proj_norm_res([o], [wo], g_post, x2d, tm)


def kernel(x_prompt, mem_prompt, x_sample, state_pool, cache_cmp_k, cache_cmp_v, cache_slc_k, cache_slc_v, cache_win_k, cache_win_v, state_lru_h, state_lru_conv, cache_mem_k, cache_mem_v, page_table, norm_gain, mem_norm_gain, w_in_even, pool_w, pool_scale, w_cmp_k, w_cmp_v, w_out_even, w_in_odd, conv_w, conv_b, lru_wa, lru_ba, lru_wx, lru_bx, lru_lambda, w_out_odd, w_xq, w_xk, w_xv, w_xo, w_ffn_gate, w_ffn_up, w_ffn_down):
    bp, tp, d = x_prompt.shape
    bs, ts, _ = x_sample.shape
    depth = norm_gain.shape[0]
    n_even = w_in_even.shape[0]
    n_pages = page_table.shape[1]
    past_len = n_pages * PAGE_SIZE
    n_phys = cache_cmp_k.shape[0]
    tsp = SAMPLE_T_PAD
    np_tok, ns_tok = bp * tp, bs * tsp
    tm_p, tm_s = 512, ns_tok
    tq, tk = 256, 128

    xp = x_prompt.reshape(np_tok, d)
    xs = jnp.pad(x_sample, ((0, 0), (0, tsp - ts), (0, 0))).reshape(ns_tok, d)

    n_sub_p = tp // CMP_STRIDE
    n_cmp_p = n_sub_p - 1
    n_sel_p = tp // SEL_BLOCK
    pool_mat = jnp.asarray(_block_mean_matrix(tp).T, BF16)
    mselt = jnp.asarray(_sel_from_cmp(n_sub_p, n_cmp_p, n_sel_p).T, BF16)
    ident = jnp.asarray(np.eye(tq, dtype=np.float32), BF16)
    ident_kv = jnp.asarray(np.eye(KV_COLS, dtype=np.float32), BF16)
    subpool = jnp.asarray(np.repeat(np.eye(SUB_PER_STEP, dtype=np.float32), CMP_STRIDE, axis=0), BF16)
    steps = n_pages // PAGES_PER_STEP
    n_sub_s = past_len // CMP_STRIDE
    n_sel_s = -(-(past_len + ts) // SEL_BLOCK)
    n_lane_s = (steps + 1) * LANE
    msel_np = np.zeros((n_sub_s, n_lane_s), np.float32)
    dense = _sel_from_cmp(n_sub_s, n_sub_s, n_sel_s)
    for j in range(n_sel_s):
        msel_np[:, (j // SEL_PER_STEP) * LANE + j % SEL_PER_STEP] = dense[:, j]
    msel_s = jnp.asarray(msel_np, BF16)
    expand_np = np.zeros((LANE, PAGES_PER_STEP * PAGE_SIZE), np.float32)
    expand_np[:SEL_PER_STEP] = np.repeat(np.eye(SEL_PER_STEP, dtype=np.float32), SEL_BLOCK, axis=1)
    expand_s = jnp.asarray(expand_np, BF16)

    row = lambda v: v.reshape(1, -1)
    out_ev_p, out_ev_s, out_lru_p, out_lru_s, mem_k_p, mem_v_p = [], [], [], [], [], []

    for li in range(depth):
        gn = norm_gain[li]
        if li % 2 == 0:
            e = li // 2
            w_in = jnp.pad(w_in_even[e], ((0, 0), (0, EVEN_IN_PAD - EVEN_IN))).astype(BF16)
            pw = pool_w[e].astype(BF16)
            ps = row(pool_scale[e])
            wk_bd_t = _block_diag(w_cmp_k[e]).T.astype(BF16)
            wv_bd_t = _block_diag(w_cmp_v[e]).T.astype(BF16)
            w_out = w_out_even[e].astype(BF16)
            kv0 = POOL_WIDTH + NSA_WIDTH
            kcol = lambda k: slice(kv0 + k * KV_COLS, kv0 + (k + 1) * KV_COLS)
            w_u = w_in[:, :POOL_WIDTH]
            w_kpm = jnp.concatenate([w_in[:, kcol(2)], w_in[:, kcol(4)]], axis=1)
            w_t = w_in[:, POOL_WIDTH:kv0 + 7 * KV_COLS].T

            u, kpm, q_t, gate_t, kc, vc, ks, vs, kw, vw = even_in_prompt(xp, row(gn[0]), w_u, w_kpm, w_t,
                                                                         bp, tp, tm_p)
            u3 = u.reshape(bp, tp, POOL_WIDTH)
            a_out = pool_mix(u3, jnp.zeros((bp, POOL_HALO, POOL_WIDTH), F32), pw, ps, 0)
            kcmp, vcmp_t = compress_kv(kc, vc, pool_mat, wk_bd_t, wv_bd_t, ident_kv)
            o_nsa = nsa_prompt(q_t, gate_t, kcmp, vcmp_t, kpm.reshape(bp, tp, 2 * KV_COLS), vs, vw,
                               mselt, ident, tq, tk)
            mix_p = ([a_out, o_nsa], [w_out[:POOL_WIDTH], w_out[POOL_WIDTH:]])
            kv5 = lambda a: a.reshape(bp, NSA_KV_HEADS, HEAD_DIM, a.shape[-1]).transpose(0, 3, 1, 2)
            n_keep = min(WINDOW, tp)
            out_ev_p.append((u3[:, -POOL_STATE:], kv5(kc), kv5(vc), kv5(ks), kv5(vs),
                             kv5(kw[:, :, -n_keep:]), kv5(vw[:, :, -n_keep:])))

            u, q, kc, vc, ks, vs, kw, vw, gate = _even_in_proj(xs, row(gn[0]), w_in, tm_s)
            as3 = lambda a: a.reshape(bs, tsp, a.shape[-1])
            u3 = as3(u)
            past_pool = jnp.pad(state_pool[e], ((0, 0), (POOL_HALO - POOL_STATE, 0), (0, 0)))
            a_out = pool_mix(u3, past_pool, pw, ps, past_len)
            paged = lambda c: c.transpose(0, 1, 3, 4, 2).reshape(n_phys * n_even, KV_COLS, PAGE_SIZE)
            o_cmp, sel = sample_cmp(page_table, paged(cache_cmp_k), paged(cache_cmp_v), as3(kc), as3(vc),
                                    as3(q), wk_bd_t, wv_bd_t, msel_s, subpool, ident_kv, e, n_even, ts)
            n_win = cache_win_k.shape[2]
            win_k = cache_win_k[e].transpose(0, 2, 3, 1).reshape(bs, KV_COLS, n_win)
            win_v = cache_win_v[e].transpose(0, 2, 3, 1).reshape(bs, KV_COLS, n_win)
            o_nsa = sample_slc(page_table, paged(cache_slc_k), paged(cache_slc_v), as3(q), sel, expand_s,
                               as3(ks), as3(vs), win_k, win_v, as3(kw), as3(vw), o_cmp, as3(gate),
                               e, n_even, ts)
            xs = proj_norm_res([a_out.reshape(ns_tok, POOL_WIDTH), o_nsa.reshape(ns_tok, NSA_WIDTH)],
                               [w_out[:POOL_WIDTH], w_out[POOL_WIDTH:]], row(gn[1]), xs, tm_s)
            new4 = lambda a: as3(a)[:, :ts].reshape(bs, ts, NSA_KV_HEADS, HEAD_DIM)
            out_ev_s.append((jnp.concatenate([state_pool[e], u3[:, :ts]], axis=1)[:, -POOL_STATE:],
                             new4(kc), new4(vc), new4(ks), new4(vs),
                             jnp.concatenate([cache_win_k[e], new4(kw)], axis=1)[:, -n_win:],
                             jnp.concatenate([cache_win_v[e], new4(vw)], axis=1)[:, -n_win:]))
        else:
            o = li // 2
            w_in = w_in_odd[o].astype(BF16)
            cw = jnp.pad(conv_w[o], ((0, 8 - CONV_WIDTH), (0, 0)))
            consts = (cw, row(conv_b[o]), lru_wa[o].astype(BF16), lru_wx[o].astype(BF16),
                      row(lru_ba[o]), row(lru_bx[o]), row(lru_lambda[o]))
            w_out = w_out_odd[o].astype(BF16)
            groups = ((0, LRU_WIDTH, ((0, LRU_WIDTH, None),)),
                      (LRU_WIDTH, LRU_WIDTH, ((0, LRU_WIDTH, jax.nn.gelu),)))

            xb, yb = norm_matmul(xp, row(gn[0]), w_in, groups, (F32, F32), tm_p)
            xb3, yb3 = xb.reshape(bp, tp, LRU_WIDTH), yb.reshape(bp, tp, LRU_WIDTH)
            gated, tail = lru(xb3, yb3, jnp.zeros((bp, 8, LRU_WIDTH), F32), jnp.zeros((bp, 1, LRU_WIDTH), F32),
                              *consts, tt=256, first_pos_zero=True)
            mix_p = ([gated], [w_out])
            out_lru_p.append((xb3[:, -(CONV_WIDTH - 1):], tail[:, 7]))

            xb, yb = norm_matmul(xs, row(gn[0]), w_in, groups, (F32, F32), tm_s)
            xb3, yb3 = xb.reshape(bs, tsp, LRU_WIDTH), yb.reshape(bs, tsp, LRU_WIDTH)
            conv_past = jnp.pad(state_lru_conv[o], ((0, 0), (8 - (CONV_WIDTH - 1), 0), (0, 0)))
            gated, tail = lru(xb3, yb3, conv_past, state_lru_h[o][:, None, :], *consts, tt=tsp,
                              first_pos_zero=False)
            xs = proj_norm_res([gated.reshape(ns_tok, LRU_WIDTH)], [w_out], row(gn[1]), xs, tm_s)
            conv_new = jnp.concatenate([state_lru_conv[o], xb3[:, :ts]], axis=1)[:, -(CONV_WIDTH - 1):]
            out_lru_s.append((conv_new, tail[:, ts - 1]))

        wq, wo = w_xq[li].astype(BF16), w_xo[li].astype(BF16)
        wkv = jnp.concatenate([w_xk[li], w_xv[li]], axis=1).astype(BF16)
        mem2d = mem_prompt.reshape(bp * N_MEM, d)
        kv_groups = ((0, D_MODEL, ((0, D_MODEL, None),)), (D_MODEL, D_MODEL, ((0, D_MODEL, None),)))
        mk, mv = norm_matmul(mem2d, row(mem_norm_gain[li]), wkv, kv_groups, (F32, F32), min(tm_p, bp * N_MEM))
        mk3, mv3 = mk.reshape(bp, N_MEM, d), mv.reshape(bp, N_MEM, d)
        mem_k_p.append(mk3.reshape(bp, N_MEM, XATTN_HEADS, XATTN_HEAD_DIM))
        mem_v_p.append(mv3.reshape(bp, N_MEM, XATTN_HEADS, XATTN_HEAD_DIM))
        wg, wu, wd = w_ffn_gate[li].astype(BF16), w_ffn_up[li].astype(BF16), w_ffn_down[li].astype(BF16)
        xp = post_mixer(mix_p[0], mix_p[1], xp.reshape(bp, tp, d), gn, mk3, mv3, wq, wo, wg, wu, wd,
                        tm_p).reshape(np_tok, d)
        xs = _xattn_block(xs, bs, tsp, cache_mem_k, cache_mem_v, li, row(gn[2]), row(gn[3]), wq, wo, tm_s, tsp)
        xs = ffn(xs, row(gn[4]), row(gn[5]), wg, wu, wd, tm_s)

    stack = lambda items, k, axis=0: jnp.stack([s[k] for s in items], axis=axis)
    y_prompt = xp.reshape(bp, tp, d)
    y_sample = xs.reshape(bs, tsp, d)[:, :ts]
    return (y_prompt, y_sample,
            stack(out_ev_p, 0), stack(out_ev_s, 0),
            stack(out_ev_p, 1, 1), stack(out_ev_s, 1, 1),
            stack(out_ev_p, 2, 1), stack(out_ev_s, 2, 1),
            stack(out_ev_p, 3, 1), stack(out_ev_s, 3, 1),
            stack(out_ev_p, 4, 1), stack(out_ev_s, 4, 1),
            stack(out_ev_p, 5), stack(out_ev_s, 5),
            stack(out_ev_p, 6), stack(out_ev_s, 6),
            stack(out_lru_p, 1), stack(out_lru_s, 1),
            stack(out_lru_p, 0), stack(out_lru_s, 0),
            jnp.stack(mem_k_p), jnp.stack(mem_v_p))
```

```python
import functools

import numpy as np
import jax
import jax.numpy as jnp
from jax import lax
from jax.experimental import pallas as pl
from jax.experimental.pallas import tpu as pltpu

F32 = jnp.float32
BF16 = jnp.bfloat16

D_MODEL = 1024
RMS_EPS = 1e-6
NEG_INF = -1e30
FORCE_SCORE = 1e30
REMOVED_SCORE = -3e38
LOG2E = 1.4426950408889634
M_INIT = -1e29

POOL_WIDTH = 512
POOL_WINDOWS = (2, 4, 8, 16)
POOL_GROUP = 128
POOL_STATE = 15
POOL_HALO = 16

HEAD_DIM = 64
NSA_HEADS = 8
NSA_KV_HEADS = 2
HEADS_PER_GROUP = NSA_HEADS // NSA_KV_HEADS
NSA_WIDTH = 512
KV_COLS = 128
CMP_BLOCK = 32
CMP_STRIDE = 16
SEL_BLOCK = 64
SEL_TOPK = 16
WINDOW = 512
PAGE_SIZE = 128
EVEN_IN = POOL_WIDTH + NSA_WIDTH + 6 * KV_COLS + 3 * NSA_HEADS
EVEN_IN_PAD = 1920

LRU_WIDTH = 1024
LRU_BLOCKS = 4
LRU_BLOCK = 256
CONV_WIDTH = 4
LRU_C = 8.0

N_MEM = 256
XATTN_HEADS = 4
XATTN_HEAD_DIM = 256
D_FF = 2816
FF_CHUNK = 256

SAMPLE_T_PAD = 8
PAGES_PER_STEP = 16
SEL_PER_STEP = PAGES_PER_STEP * PAGE_SIZE // SEL_BLOCK
SUB_PER_STEP = PAGES_PER_STEP * PAGE_SIZE // CMP_STRIDE
SEL_PER_CMP = SEL_BLOCK // CMP_STRIDE
LANE = 128

VMEM_LIMIT_BYTES = 56 * 1024 * 1024


def _cparams(*sem):
    return pltpu.CompilerParams(dimension_semantics=sem, vmem_limit_bytes=VMEM_LIMIT_BYTES)


def _rms(x, g):
    return x * lax.rsqrt(jnp.mean(x * x, axis=-1, keepdims=True) + RMS_EPS) * g


def _dot(a, b):
    return jnp.dot(a, b, preferred_element_type=F32)


def _dot_nt(a, b):
    return lax.dot_general(a, b, (((1,), (1,)), ((), ())), preferred_element_type=F32)


def _split3(x):
    p1 = x.astype(BF16)
    r1 = x - p1.astype(F32)
    p2 = r1.astype(BF16)
    p3 = (r1 - p2.astype(F32)).astype(BF16)
    return p1, p2, p3


def _full(shape):
    n = len(shape)
    return pl.BlockSpec(shape, lambda *_: (0,) * n)


def _norm_matmul_kernel(x_ref, g_ref, w_ref, *o_refs, groups):
    h = _rms(x_ref[...], g_ref[...]).astype(BF16)
    k = 0
    for start, width, outs in groups:
        z = _dot(h, w_ref[:, start:start + width])
        for off, w, post in outs:
            v = z[:, off:off + w]
            if post is not None:
                v = post(v)
            o_refs[k][...] = v.astype(o_refs[k].dtype)
            k += 1


def norm_matmul(x, g, w, groups, out_dtypes, tm):
    n, d = x.shape
    widths = [w_ for _, _, outs in groups for _, w_, _ in outs]
    return pl.pallas_call(
        functools.partial(_norm_matmul_kernel, groups=groups),
        grid=(n // tm,),
        in_specs=[pl.BlockSpec((tm, d), lambda i: (i, 0)), _full(g.shape), _full(w.shape)],
        out_specs=[pl.BlockSpec((tm, wd), lambda i: (i, 0)) for wd in widths],
        out_shape=[jax.ShapeDtypeStruct((n, wd), dt) for wd, dt in zip(widths, out_dtypes)],
        compiler_params=_cparams("parallel"),
    )(x, g, w)


def _even_in_prompt_kernel(x_ref, g_ref, wu_ref, wk_ref, wt_ref, u_ref, kpm_ref, qt_ref, gatet_ref, *kvt_refs):
    h = _rms(x_ref[...], g_ref[...]).astype(BF16)
    u_ref[...] = _dot(h, wu_ref[...])
    kpm_ref[...] = _dot(h, wk_ref[...]).astype(kpm_ref.dtype)
    z_t = _dot_nt(wt_ref[...], h)
    qt_ref[0] = (z_t[0:NSA_WIDTH] * (HEAD_DIM ** -0.5 * LOG2E)).astype(qt_ref.dtype)
    for k, ref in enumerate(kvt_refs):
        ref[0] = z_t[NSA_WIDTH + k * KV_COLS:NSA_WIDTH + (k + 1) * KV_COLS, :]
    gatet_ref[0] = jax.nn.sigmoid(z_t[NSA_WIDTH + 6 * KV_COLS:, :])


def even_in_prompt(x, g, w_u, w_kpm, w_t, b, t, tm):
    n, d = x.shape
    per_seq = t // tm
    row_tile = lambda w: pl.BlockSpec((tm, w), lambda i: (i, 0))
    feat_tile = lambda f: pl.BlockSpec((1, f, tm), lambda i: (i // per_seq, 0, i % per_seq))
    n_kv = 6
    return pl.pallas_call(
        _even_in_prompt_kernel,
        grid=(n // tm,),
        in_specs=[row_tile(d), _full(g.shape), _full(w_u.shape), _full(w_kpm.shape), _full(w_t.shape)],
        out_specs=([row_tile(POOL_WIDTH), row_tile(2 * KV_COLS), feat_tile(NSA_WIDTH), feat_tile(LANE)]
                   + [feat_tile(KV_COLS)] * n_kv),
        out_shape=([jax.ShapeDtypeStruct((n, POOL_WIDTH), F32), jax.ShapeDtypeStruct((n, 2 * KV_COLS), BF16),
                    jax.ShapeDtypeStruct((b, NSA_WIDTH, t), BF16), jax.ShapeDtypeStruct((b, LANE, t), F32)]
                   + [jax.ShapeDtypeStruct((b, KV_COLS, t), F32)] * n_kv),
        compiler_params=_cparams("parallel"),
    )(x, g, w_u, w_kpm, w_t)


def _proj_norm_res_kernel(*refs, n_in):
    a_refs, w_refs = refs[:n_in], refs[n_in:2 * n_in]
    g_ref, x_ref, o_ref = refs[2 * n_in:]
    acc = _dot(a_refs[0][...], w_refs[0][...])
    for a_ref, w_ref in zip(a_refs[1:], w_refs[1:]):
        acc = acc + _dot(a_ref[...], w_ref[...])
    o_ref[...] = x_ref[...] + _rms(acc, g_ref[...])


def proj_norm_res(a_list, w_list, g, x, tm):
    n, d = x.shape
    n_in = len(a_list)
    return pl.pallas_call(
        functools.partial(_proj_norm_res_kernel, n_in=n_in),
        grid=(n // tm,),
        in_specs=([pl.BlockSpec((tm, a.shape[1]), lambda i: (i, 0)) for a in a_list]
                  + [_full(w.shape) for w in w_list]
                  + [_full(g.shape), pl.BlockSpec((tm, d), lambda i: (i, 0))]),
        out_specs=pl.BlockSpec((tm, d), lambda i: (i, 0)),
        out_shape=jax.ShapeDtypeStruct((n, d), F32),
        compiler_params=_cparams("parallel"),
    )(*a_list, *w_list, g, x)


def _ffn_block(x, g_in, g_out, wg_ref, wu_ref, wd_ref):
    h = _rms(x, g_in).astype(BF16)
    acc = None
    for c in range(D_FF // FF_CHUNK):
        cols = slice(c * FF_CHUNK, (c + 1) * FF_CHUNK)
        gate = _dot(h, wg_ref[:, cols])
        up = _dot(h, wu_ref[:, cols])
        act = (jax.nn.silu(gate) * up).astype(BF16)
        part = _dot(act, wd_ref[cols, :])
        acc = part if acc is None else acc + part
    return x + _rms(acc, g_out)


def _ffn_kernel(x_ref, g_in_ref, g_out_ref, wg_ref, wu_ref, wd_ref, o_ref):
    o_ref[...] = _ffn_block(x_ref[...], g_in_ref[...], g_out_ref[...], wg_ref, wu_ref, wd_ref)


def ffn(x, g_in, g_out, wg, wu, wd, tm):
    n, d = x.shape
    resident = functools.partial(pl.BlockSpec, pipeline_mode=pl.Buffered(1))
    return pl.pallas_call(
        _ffn_kernel,
        grid=(n // tm,),
        in_specs=[pl.BlockSpec((tm, d), lambda i: (i, 0)), _full(g_in.shape), _full(g_out.shape),
                  resident(wg.shape, lambda i: (0, 0)), resident(wu.shape, lambda i: (0, 0)),
                  resident(wd.shape, lambda i: (0, 0))],
        out_specs=pl.BlockSpec((tm, d), lambda i: (i, 0)),
        out_shape=jax.ShapeDtypeStruct((n, d), F32),
        compiler_params=_cparams("parallel"),
    )(x, g_in, g_out, wg, wu, wd)


def _xattn_heads(q, mk_ref, mv_ref):
    head_cols = [slice(hd * XATTN_HEAD_DIM, (hd + 1) * XATTN_HEAD_DIM) for hd in range(XATTN_HEADS)]
    if len(mk_ref.shape) == 4:
        rows = q.shape[0]
        n_kv = N_MEM * XATTN_HEADS
        k_all = mk_ref[0].reshape(n_kv, XATTN_HEAD_DIM).astype(BF16)
        v_all = mv_ref[0].reshape(n_kv, XATTN_HEAD_DIM).astype(BF16)
        q_st = jnp.concatenate([q[:, cols] for cols in head_cols], axis=0)
        own = (lax.broadcasted_iota(jnp.int32, (1, n_kv), 1) % XATTN_HEADS
               == lax.broadcasted_iota(jnp.int32, (XATTN_HEADS * rows, 1), 0) // rows)
        p = _masked_softmax_rows(_dot_nt(q_st, k_all), own)
        o_st = _dot(p.astype(BF16), v_all).astype(BF16)
        return jnp.concatenate([o_st[hd * rows:(hd + 1) * rows] for hd in range(XATTN_HEADS)], axis=-1)
    outs = []
    for cols in head_cols:
        k = mk_ref[0, :, cols].astype(BF16)
        v = mv_ref[0, :, cols].astype(BF16)
        s = _dot_nt(q[:, cols], k)
        e = jnp.exp(s - jnp.max(s, axis=-1, keepdims=True))
        p = e / jnp.sum(e, axis=-1, keepdims=True)
        outs.append(_dot(p.astype(BF16), v).astype(BF16))
    return jnp.concatenate(outs, axis=-1)


def _xattn_kernel(q_ref, mk_ref, mv_ref, o_ref):
    o_ref[0] = _xattn_heads(q_ref[0], mk_ref, mv_ref)


def xattn(q, mk, mv, layer, tq):
    b, t, d = q.shape
    mem_spec = pl.BlockSpec((None, 1) + mk.shape[2:], lambda i, j: (layer, i, 0, 0, 0))
    return pl.pallas_call(
        _xattn_kernel,
        grid=(b, t // tq),
        in_specs=[pl.BlockSpec((1, tq, d), lambda i, j: (i, j, 0)), mem_spec, mem_spec],
        out_specs=pl.BlockSpec((1, tq, d), lambda i, j: (i, j, 0)),
        out_shape=jax.ShapeDtypeStruct((b, t, d), BF16),
        compiler_params=_cparams("parallel", "parallel"),
    )(q, mk, mv)


def _post_mixer_kernel(*refs, n_in):
    a_refs, w_refs = refs[:n_in], refs[n_in:2 * n_in]
    x_ref, gn_ref, mk_ref, mv_ref, wq_ref, wo_ref, wg_ref, wu_ref, wd_ref, o_ref = refs[2 * n_in:]
    gain = lambda k: gn_ref[k:k + 1, :]
    mix = _dot(a_refs[0][0], w_refs[0][...])
    for a_ref, w_ref in zip(a_refs[1:], w_refs[1:]):
        mix = mix + _dot(a_ref[0], w_ref[...])
    x1 = x_ref[0] + _rms(mix, gain(1))
    q = (_dot(_rms(x1, gain(2)).astype(BF16), wq_ref[...]) * (XATTN_HEAD_DIM ** -0.5)).astype(BF16)
    attn = _xattn_heads(q, mk_ref, mv_ref)
    x2 = x1 + _rms(_dot(attn, wo_ref[...]), gain(3))
    o_ref[0] = _ffn_block(x2, gain(4), gain(5), wg_ref, wu_ref, wd_ref)


def post_mixer(a_list, w_list, x, gn, mk, mv, wq, wo, wg, wu, wd, tm):
    b, t, d = x.shape
    n_in = len(a_list)
    resident = lambda a: pl.BlockSpec(a.shape, lambda i, j: (0,) * a.ndim, pipeline_mode=pl.Buffered(1))
    tile = lambda w: pl.BlockSpec((1, tm, w), lambda i, j: (i, j, 0))
    per_b = lambda a: pl.BlockSpec((1,) + a.shape[1:], lambda i, j: (i, 0, 0))
    return pl.pallas_call(
        functools.partial(_post_mixer_kernel, n_in=n_in),
        grid=(b, t // tm),
        in_specs=([tile(a.shape[2]) for a in a_list] + [resident(w) for w in w_list]
                  + [tile(d), resident(gn), per_b(mk), per_b(mv)]
                  + [resident(w) for w in (wq, wo, wg, wu, wd)]),
        out_specs=tile(d),
        out_shape=jax.ShapeDtypeStruct((b, t, d), F32),
        compiler_params=_cparams("parallel", "parallel"),
    )(*a_list, *w_list, x, gn, mk, mv, wq, wo, wg, wu, wd)


def _pool_kernel(u_ref, past_ref, w_ref, scale_ref, o_ref, buf, *, t, chunk, start_pos):
    buf[0:POOL_HALO, :] = past_ref[0]
    buf[POOL_HALO:POOL_HALO + t, :] = u_ref[0]
    for c in range(t // chunk):
        base = POOL_HALO + c * chunk
        pos = start_pos + c * chunk + lax.broadcasted_iota(jnp.int32, (chunk, 1), 0)
        for gi, win in enumerate(POOL_WINDOWS):
            cols = slice(gi * POOL_GROUP, (gi + 1) * POOL_GROUP)
            cur = buf[base:base + chunk, cols]
            win_sum = cur
            for k in range(1, win):
                win_sum = win_sum + buf[base - k:base - k + chunk, cols]
            cnt = jnp.minimum(pos + 1, win).astype(F32)
            diff = win_sum / cnt - cur
            y = _dot(diff.astype(BF16), w_ref[gi]) * scale_ref[:, cols]
            o_ref[0, c * chunk:(c + 1) * chunk, cols] = y.astype(o_ref.dtype)


def pool_mix(u, past, w, scale, start_pos):
    b, t, c = u.shape
    chunk = min(t, 256)
    return pl.pallas_call(
        functools.partial(_pool_kernel, t=t, chunk=chunk, start_pos=start_pos),
        grid=(b,),
        in_specs=[pl.BlockSpec((1, t, c), lambda i: (i, 0, 0)),
                  pl.BlockSpec((1, POOL_HALO, c), lambda i: (i, 0, 0)),
                  _full(w.shape), _full(scale.shape)],
        out_specs=pl.BlockSpec((1, t, c), lambda i: (i, 0, 0)),
        out_shape=jax.ShapeDtypeStruct((b, t, c), BF16),
        scratch_shapes=[pltpu.VMEM((POOL_HALO + t, c), F32)],
        compiler_params=_cparams("parallel"),
    )(u, past, w, scale)


def _compress_kernel(kc_ref, vc_ref, pool_ref, wkt_ref, wvt_ref, ident_ref, k_ref, vt_ref):
    def compress(src, wt_ref):
        x = src[0]
        hi = x.astype(BF16)
        lo = (x - hi.astype(F32)).astype(BF16)
        mean_t = _dot(hi, pool_ref[...]) + _dot(lo, pool_ref[...])
        return _dot(wt_ref[...], mean_t.astype(BF16)).astype(BF16)
    k_ref[0] = _dot_nt(ident_ref[...], compress(kc_ref, wkt_ref)).astype(BF16)
    vt_ref[0] = compress(vc_ref, wvt_ref)


def compress_kv(kc_t, vc_t, pool_mat, wk_bd_t, wv_bd_t, ident):
    b, c, t = kc_t.shape
    n_cmp = pool_mat.shape[1]
    per_b = pl.BlockSpec((1, c, t), lambda i: (i, 0, 0))
    return pl.pallas_call(
        _compress_kernel,
        grid=(b,),
        in_specs=[per_b, per_b, _full(pool_mat.shape), _full(wk_bd_t.shape), _full(wv_bd_t.shape),
                  _full(ident.shape)],
        out_specs=[pl.BlockSpec((1, n_cmp, c), lambda i: (i, 0, 0)), pl.BlockSpec((1, c, n_cmp), lambda i: (i, 0, 0))],
        out_shape=[jax.ShapeDtypeStruct((b, n_cmp, c), BF16), jax.ShapeDtypeStruct((b, c, n_cmp), BF16)],
        compiler_params=_cparams("parallel"),
    )(kc_t, vc_t, pool_mat, wk_bd_t, wv_bd_t, ident)


def _masked_softmax_rows(s, mask):
    s = jnp.where(mask, s, NEG_INF)
    e = jnp.where(mask, jnp.exp(s - jnp.max(s, axis=-1, keepdims=True)), 0.0)
    return e / jnp.maximum(jnp.sum(e, axis=-1, keepdims=True), 1e-30)


def _flash_step(raw_ref, v_t, mask, m_ref, l_ref, acc_ref):
    tq = mask.shape[1]
    bias = jnp.where(mask, 0.0, NEG_INF)
    for hh in range(HEADS_PER_GROUP):
        lanes = slice(hh * tq, (hh + 1) * tq)
        s = raw_ref[:, lanes] + bias
        m_old = m_ref[:, lanes]
        m_new = jnp.maximum(m_old, jnp.max(s, axis=0, keepdims=True))
        alpha = jnp.exp2(m_old - m_new)
        e = jnp.exp2(s - m_new)
        m_ref[:, lanes] = m_new
        l_ref[:, lanes] = alpha * l_ref[:, lanes] + jnp.sum(e, axis=0, keepdims=True)
        acc_ref[:, lanes] = alpha * acc_ref[:, lanes] + _dot(v_t, e.astype(BF16))


def _nsa_prompt_kernel(qt_ref, gatet_ref, kcmp_ref, vcmpt_ref, kpm_ref, vst_ref, vwt_ref,
                       mselt_ref, ident_ref, o_ref, sel_s, raw_s, m_s, l_s, acc_s, *, tq, tk):
    i = pl.program_id(1)
    q0 = i * tq
    n_cmp = kcmp_ref.shape[1]
    n_sel = mselt_ref.shape[0]
    blocks_per_tile = tk // SEL_BLOCK
    tiles_per_q = tq // tk
    qpos_row = q0 + lax.broadcasted_iota(jnp.int32, (1, tq), 1)
    key_col = lax.broadcasted_iota(jnp.int32, (tk, 1), 0)
    cmp_end_col = lax.broadcasted_iota(jnp.int32, (n_cmp, 1), 0) * CMP_STRIDE + (CMP_BLOCK - 1)
    cmp_bias = jnp.where(cmp_end_col <= qpos_row, 0.0, NEG_INF)
    sel_j = lax.broadcasted_iota(jnp.int32, (n_sel, 1), 0)
    cur_row = qpos_row // SEL_BLOCK
    gate_t = gatet_ref[0]
    zero_half = jnp.zeros((HEAD_DIM, HEADS_PER_GROUP * tq), BF16)

    groups = range(NSA_KV_HEADS)
    grows = [slice(g * HEAD_DIM, (g + 1) * HEAD_DIM) for g in groups]
    q_pads, o_cmps = [], []
    for g in groups:
        q_t_g = jnp.concatenate(
            [qt_ref[0, (g * HEADS_PER_GROUP + hh) * HEAD_DIM:(g * HEADS_PER_GROUP + hh + 1) * HEAD_DIM, :]
             for hh in range(HEADS_PER_GROUP)], axis=1)
        q_pad = jnp.concatenate([q_t_g, zero_half] if g == 0 else [zero_half, q_t_g], axis=0)
        q_pads.append(q_pad)

        s_t = _dot(kcmp_ref[0], q_pad) + jnp.concatenate([cmp_bias] * HEADS_PER_GROUP, axis=1)
        e_t = jnp.exp2(s_t - jnp.maximum(jnp.max(s_t, axis=0, keepdims=True), M_INIT))
        p_t = e_t / jnp.maximum(jnp.sum(e_t, axis=0, keepdims=True), 1e-30)
        o_cmps.append(_dot(vcmpt_ref[0, grows[g], :], p_t.astype(BF16)))
        psum_t = p_t[:, 0:tq]
        for hh in range(1, HEADS_PER_GROUP):
            psum_t = psum_t + p_t[:, hh * tq:(hh + 1) * tq]

        imp_t = None
        for part in _split3(psum_t):
            d = _dot(mselt_ref[...], part)
            imp_t = d if imp_t is None else imp_t + d
        valid = sel_j * SEL_BLOCK <= qpos_row
        forced = (sel_j == 0) | (sel_j == cur_row) | (sel_j == cur_row - 1)
        score = jnp.where(forced, FORCE_SCORE, jnp.where(valid, imp_t, NEG_INF))
        rank = jnp.zeros((n_sel, tq), F32)
        for jp in range(n_sel):
            sj = score[jp:jp + 1, :]
            beats = (sj > score) | ((sj == score) & (sel_j > jp))
            rank = rank + jnp.where(beats, 1.0, 0.0)
        sel_s[g] = jnp.where(rank < SEL_TOPK, 1.0, 0.0)

    m_s[...] = jnp.full(m_s.shape, M_INIT, F32)
    l_s[...] = jnp.zeros(l_s.shape, F32)
    acc_s[...] = jnp.zeros(acc_s.shape, F32)
    n_tiles = (i + 1) * tiles_per_q
    first_win = jnp.maximum((i * tq - (WINDOW - 1)) // tk, 0)
    key_rows = lambda kt: pl.ds(pl.multiple_of(kt * tk, tk), tk)

    def issue_scores(kt, slot, branch):
        k = kpm_ref[0, key_rows(kt), branch * KV_COLS:(branch + 1) * KV_COLS]
        for g in groups:
            raw_s[slot, NSA_KV_HEADS * branch + g] = _dot(k, q_pads[g])

    def slc_update(kt, slot):
        causal = kt * tk + key_col <= qpos_row
        for g in groups:
            chosen = jnp.concatenate(
                [jnp.broadcast_to(sel_s[g, pl.ds(kt * blocks_per_tile + r, 1), :], (SEL_BLOCK, tq))
                 for r in range(blocks_per_tile)], axis=0)
            v_t = vst_ref[0, grows[g], key_rows(kt)].astype(BF16)
            _flash_step(raw_s.at[slot, g], v_t, (chosen > 0.5) & causal, m_s.at[g], l_s.at[g], acc_s.at[g])

    def win_update(kt, slot):
        dist = qpos_row - (kt * tk + key_col)
        mask = (dist >= 0) & (dist < WINDOW)
        for g in groups:
            c = NSA_KV_HEADS + g
            v_t = vwt_ref[0, grows[g], key_rows(kt)].astype(BF16)
            _flash_step(raw_s.at[slot, c], v_t, mask, m_s.at[c], l_s.at[c], acc_s.at[c])

    def far_body(pair, _):
        for slot in (0, 1):
            kt = 2 * pair + slot
            issue_scores(kt + 1, 1 - slot, 0)
            slc_update(kt, slot)
        return 0

    def band_body(pair, _):
        for slot in (0, 1):
            kt = 2 * pair + slot
            nxt = jnp.minimum(kt + 1, n_tiles - 1)
            issue_scores(nxt, 1 - slot, 0)
            issue_scores(nxt, 1 - slot, 1)
            slc_update(kt, slot)
            win_update(kt, slot)
        return 0

    issue_scores(0, 0, 0)
    lax.fori_loop(0, first_win // 2, far_body, 0)
    issue_scores(first_win, 0, 1)
    lax.fori_loop(first_win // 2, n_tiles // 2, band_body, 0)

    head_out = [None] * NSA_HEADS
    for g in groups:
        o_slc = acc_s[g] / l_s[g]
        o_win = acc_s[NSA_KV_HEADS + g] / l_s[NSA_KV_HEADS + g]
        for hh in range(HEADS_PER_GROUP):
            h = g * HEADS_PER_GROUP + hh
            lanes = slice(hh * tq, (hh + 1) * tq)
            head_out[h] = (gate_t[3 * h:3 * h + 1, :] * o_cmps[g][:, lanes]
                           + gate_t[3 * h + 1:3 * h + 2, :] * o_slc[:, lanes]
                           + gate_t[3 * h + 2:3 * h + 3, :] * o_win[:, lanes])
    o_t = jnp.concatenate(head_out, axis=0).astype(BF16)
    o_ref[0] = _dot_nt(ident_ref[...], o_t).astype(o_ref.dtype)


def nsa_prompt(q_t, gate_t, kcmp, vcmp_t, kpm, vs_t, vw_t, mselt, ident, tq, tk):
    b, _, t = q_t.shape
    feat_tile = lambda f: pl.BlockSpec((1, f, tq), lambda i, j: (i, 0, j))
    whole = lambda a: pl.BlockSpec((1,) + a.shape[1:], lambda i, j: (i, 0, 0))
    per_batch = [kcmp, vcmp_t, kpm, vs_t, vw_t]
    n_chain = 2 * NSA_KV_HEADS
    assert tq % (2 * tk) == 0 and WINDOW % (2 * tk) == 0 and tk % SEL_BLOCK == 0
    return pl.pallas_call(
        functools.partial(_nsa_prompt_kernel, tq=tq, tk=tk),
        grid=(b, t // tq),
        in_specs=([feat_tile(NSA_WIDTH), feat_tile(LANE)] + [whole(a) for a in per_batch]
                  + [_full(mselt.shape), _full(ident.shape)]),
        out_specs=pl.BlockSpec((1, tq, NSA_WIDTH), lambda i, j: (i, j, 0)),
        out_shape=jax.ShapeDtypeStruct((b, t, NSA_WIDTH), BF16),
        scratch_shapes=[pltpu.VMEM((NSA_KV_HEADS, mselt.shape[0], tq), F32),
                        pltpu.VMEM((2, n_chain, tk, HEADS_PER_GROUP * tq), F32),
                        pltpu.VMEM((n_chain, 1, HEADS_PER_GROUP * tq), F32),
                        pltpu.VMEM((n_chain, 1, HEADS_PER_GROUP * tq), F32),
                        pltpu.VMEM((n_chain, HEAD_DIM, HEADS_PER_GROUP * tq), F32)],
        compiler_params=_cparams("parallel", "arbitrary"),
    )(q_t, gate_t, *per_batch, mselt, ident)


def _stack_heads(q, g):
    return jnp.concatenate(
        [q[:, (g * HEADS_PER_GROUP + hh) * HEAD_DIM:(g * HEADS_PER_GROUP + hh + 1) * HEAD_DIM]
         for hh in range(HEADS_PER_GROUP)], axis=0)


def _sample_cmp_kernel(pt_ref, *refs, n_valid, past_len):
    del pt_ref
    p = PAGES_PER_STEP
    tp = SAMPLE_T_PAD
    k_pages, v_pages = refs[:p], refs[p:2 * p]
    (kn_ref, vn_ref, q_ref, wkt_ref, wvt_ref, subpool_ref, ident_ref,
     ocmp_ref, sel_ref, sub_k, sub_v) = refs[2 * p:]
    c = pl.program_id(1)
    n_sub = past_len // CMP_STRIDE
    steps = past_len // (p * PAGE_SIZE)

    for pages, sub in ((k_pages, sub_k), (v_pages, sub_v)):
        x = jnp.concatenate([r[...] for r in pages], axis=1)
        hi = x.astype(BF16)
        lo = (x - hi.astype(F32)).astype(BF16)
        sub[c] = _dot(hi, subpool_ref[...]) + _dot(lo, subpool_ref[...])

    @pl.when(c == steps - 1)
    def _():
        t_col = lax.broadcasted_iota(jnp.int32, (tp, 1), 0)
        new_valid = lax.broadcasted_iota(jnp.int32, (1, tp), 1) < n_valid
        first_lane = lax.broadcasted_iota(jnp.int32, (1, SUB_PER_STEP), 1) == 0
        cmp_kv = []
        for new_ref, sub, wt_ref in ((kn_ref, sub_k, wkt_ref), (vn_ref, sub_v, wvt_ref)):
            new_t = None
            for part in _split3(new_ref[0]):
                d = _dot_nt(ident_ref[...], part)
                new_t = d if new_t is None else new_t + d
            new_sum = jnp.sum(jnp.where(new_valid, new_t, 0.0), axis=1, keepdims=True)
            sub[steps] = jnp.where(first_lane, new_sum, 0.0)
            sums = jnp.concatenate([sub[s] for s in range(steps + 1)], axis=1)
            mean_t = (sums[:, 0:n_sub] + sums[:, 1:n_sub + 1]) * (1.0 / CMP_BLOCK)
            cmp_kv.append(_dot(wt_ref[...], mean_t.astype(BF16)).astype(BF16))
        kcmp_t, vcmp_t = cmp_kv

        rows = HEADS_PER_GROUP * tp
        qpos_col = past_len + lax.broadcasted_iota(jnp.int32, (rows, 1), 0) % tp
        cmp_end_row = lax.broadcasted_iota(jnp.int32, (1, n_sub), 1) * CMP_STRIDE + (CMP_BLOCK - 1)
        cmp_mask = cmp_end_row <= qpos_col

        n_lane = sel_ref.shape[2]
        n_sel = -(-(past_len + n_valid) // SEL_BLOCK)
        lane = lax.broadcasted_iota(jnp.int32, (1, n_lane), 1)
        sel_j = lane // SEL_PER_CMP
        lane_used = (lane % SEL_PER_CMP == 0) & (sel_j < n_sel)
        qpos_t = past_len + t_col
        cur = qpos_t // SEL_BLOCK
        valid = lane_used & (sel_j * SEL_BLOCK <= qpos_t)
        forced = lane_used & ((sel_j == 0) | (sel_j == cur) | (sel_j == cur - 1))
        lane_pad = jnp.zeros((tp, n_lane - n_sub), F32)

        q = q_ref[0]
        o_heads = [None] * NSA_HEADS
        scores = []
        for g in range(NSA_KV_HEADS):
            gcols = slice(g * HEAD_DIM, (g + 1) * HEAD_DIM)
            prob = _masked_softmax_rows(_dot(_stack_heads(q, g), kcmp_t[gcols, :]), cmp_mask)
            o_g = _dot_nt(prob.astype(BF16), vcmp_t[gcols, :])
            psum = prob[0:tp]
            for hh in range(1, HEADS_PER_GROUP):
                psum = psum + prob[hh * tp:(hh + 1) * tp]
                o_heads[g * HEADS_PER_GROUP + hh] = o_g[hh * tp:(hh + 1) * tp]
            o_heads[g * HEADS_PER_GROUP] = o_g[0:tp]
            pz = jnp.concatenate([psum, lane_pad], axis=1)
            imp = pltpu.roll(pz, 1, 1) + pz
            for d in range(1, SEL_PER_CMP):
                imp = imp + pltpu.roll(pz, n_lane - d, 1)
            scores.append(jnp.where(forced, FORCE_SCORE, jnp.where(valid, imp, NEG_INF)))
        score = jnp.concatenate(scores, axis=0)
        chosen = jnp.zeros(score.shape, F32)
        for _ in range(SEL_TOPK):
            first = jnp.argmax(score, axis=-1, keepdims=True)
            hit = lane == first
            chosen = jnp.where(hit, 1.0, chosen)
            score = jnp.where(hit, REMOVED_SCORE, score)
        sel_ref[0] = chosen
        ocmp_ref[0] = jnp.concatenate(o_heads, axis=-1)


def _page_spec(k, layer, n_layers):
    return pl.BlockSpec((None, KV_COLS, PAGE_SIZE),
                        lambda i, c, pt: (pt[i, c * PAGES_PER_STEP + k] * n_layers + layer, 0, 0))


def sample_cmp(page_table, cache_k, cache_v, kc_new, vc_new, q, wk_bd_t, wv_bd_t, subpool, ident,
               layer, n_layers, n_valid):
    b, n_pages = page_table.shape
    past_len = n_pages * PAGE_SIZE
    steps = n_pages // PAGES_PER_STEP
    assert SUB_PER_STEP == LANE
    n_lane = (steps + 1) * LANE
    per_b = lambda a: pl.BlockSpec((1,) + a.shape[1:], lambda i, c, pt: (i, 0, 0))
    const = lambda a: pl.BlockSpec(a.shape, lambda i, c, pt: (0,) * a.ndim)
    consts = [wk_bd_t, wv_bd_t, subpool, ident]
    grid_spec = pltpu.PrefetchScalarGridSpec(
        num_scalar_prefetch=1,
        grid=(b, steps),
        in_specs=([_page_spec(k, layer, n_layers) for k in range(PAGES_PER_STEP)] * 2
                  + [per_b(kc_new), per_b(vc_new), per_b(q)] + [const(a) for a in consts]),
        out_specs=[pl.BlockSpec((1, SAMPLE_T_PAD, NSA_WIDTH), lambda i, c, pt: (i, 0, 0)),
                   pl.BlockSpec((1, 2 * SAMPLE_T_PAD, n_lane), lambda i, c, pt: (i, 0, 0))],
        scratch_shapes=[pltpu.VMEM((steps + 1, KV_COLS, SUB_PER_STEP), F32)] * 2,
    )
    return pl.pallas_call(
        functools.partial(_sample_cmp_kernel, n_valid=n_valid, past_len=past_len),
        grid_spec=grid_spec,
        out_shape=[jax.ShapeDtypeStruct((b, SAMPLE_T_PAD, NSA_WIDTH), F32),
                   jax.ShapeDtypeStruct((b, 2 * SAMPLE_T_PAD, n_lane), F32)],
        compiler_params=_cparams("parallel", "arbitrary"),
    )(page_table, *([cache_k] * PAGES_PER_STEP), *([cache_v] * PAGES_PER_STEP),
      kc_new, vc_new, q, *consts)


def _sample_slc_kernel(pt_ref, *refs, n_valid, past_len):
    del pt_ref
    p = PAGES_PER_STEP
    tp = SAMPLE_T_PAD
    rows = HEADS_PER_GROUP * tp
    k_pages, v_pages = refs[:p], refs[p:2 * p]
    (q_ref, sel_ref, expand_ref, ksn_ref, vsn_ref, wk_ref, wv_ref, kwn_ref, vwn_ref, ocmp_ref, gate_ref,
     o_ref, m_s, l_s, acc_s) = refs[2 * p:]
    c = pl.program_id(1)
    q = q_ref[0]
    zero = jnp.zeros((rows, HEAD_DIM), BF16)
    q_bd = jnp.concatenate([jnp.concatenate([_stack_heads(q, 0), zero], axis=1),
                            jnp.concatenate([zero, _stack_heads(q, 1)], axis=1)], axis=0)
    t_col = lax.broadcasted_iota(jnp.int32, (NSA_KV_HEADS * rows, 1), 0) % tp
    new_row = lax.broadcasted_iota(jnp.int32, (1, tp), 1)
    new_mask = (new_row <= t_col) & (new_row < n_valid)

    @pl.when(c == 0)
    def _():
        m_s[...] = jnp.full(m_s.shape, NEG_INF, F32)
        l_s[...] = jnp.zeros(l_s.shape, F32)
        acc_s[...] = jnp.zeros(acc_s.shape, F32)

    k_all_t = jnp.concatenate([r[...] for r in k_pages], axis=1).astype(BF16)
    v_all_t = jnp.concatenate([r[...] for r in v_pages], axis=1).astype(BF16)
    chosen = _dot(sel_ref[0].astype(BF16), expand_ref[...])
    keymask = jnp.concatenate([chosen[0:tp]] * HEADS_PER_GROUP + [chosen[tp:2 * tp]] * HEADS_PER_GROUP,
                              axis=0) > 0.5
    s = jnp.where(keymask, _dot(q_bd, k_all_t), NEG_INF)
    m_old = m_s[...]
    m_new = jnp.maximum(m_old, jnp.max(s, axis=-1, keepdims=True))
    alpha = jnp.exp(m_old - m_new)
    e = jnp.where(keymask, jnp.exp(s - m_new), 0.0)
    l_s[...] = alpha * l_s[...] + jnp.sum(e, axis=-1, keepdims=True)
    acc_s[...] = alpha * acc_s[...] + _dot_nt(e.astype(BF16), v_all_t)
    m_s[...] = m_new

    @pl.when(c == pl.num_programs(1) - 1)
    def _():
        n_win = wk_ref.shape[2]
        win_row = lax.broadcasted_iota(jnp.int32, (1, n_win), 1)
        dist = t_col + n_win - win_row
        win_mask = (dist >= 0) & (dist < WINDOW)
        s_n = jnp.where(new_mask, _dot_nt(q_bd, ksn_ref[0].astype(BF16)), NEG_INF)
        m_fin = jnp.maximum(m_new, jnp.max(s_n, axis=-1, keepdims=True))
        a_fin = jnp.exp(m_new - m_fin)
        e_n = jnp.where(new_mask, jnp.exp(s_n - m_fin), 0.0)
        l_fin = a_fin * l_s[...] + jnp.sum(e_n, axis=-1, keepdims=True)
        o_slc = (a_fin * acc_s[...] + _dot(e_n.astype(BF16), vsn_ref[0].astype(BF16))) / l_fin
        s_p = jnp.where(win_mask, _dot(q_bd, wk_ref[0].astype(BF16)), NEG_INF)
        s_w = jnp.where(new_mask, _dot_nt(q_bd, kwn_ref[0].astype(BF16)), NEG_INF)
        m_w = jnp.maximum(jnp.max(s_p, axis=-1, keepdims=True), jnp.max(s_w, axis=-1, keepdims=True))
        e_p = jnp.where(win_mask, jnp.exp(s_p - m_w), 0.0)
        e_w = jnp.where(new_mask, jnp.exp(s_w - m_w), 0.0)
        l_w = jnp.sum(e_p, axis=-1, keepdims=True) + jnp.sum(e_w, axis=-1, keepdims=True)
        o_win = (_dot_nt(e_p.astype(BF16), wv_ref[0].astype(BF16))
                 + _dot(e_w.astype(BF16), vwn_ref[0].astype(BF16))) / l_w
        gate_all = gate_ref[0]
        head_out = []
        for h in range(NSA_HEADS):
            g, hh = divmod(h, HEADS_PER_GROUP)
            r = slice(g * rows + hh * tp, g * rows + (hh + 1) * tp)
            cols = slice(g * HEAD_DIM, (g + 1) * HEAD_DIM)
            gate = gate_all[:, 3 * h:3 * h + 3]
            head_out.append(gate[:, 0:1] * ocmp_ref[0, :, h * HEAD_DIM:(h + 1) * HEAD_DIM]
                            + gate[:, 1:2] * o_slc[r, cols] + gate[:, 2:3] * o_win[r, cols])
        o_ref[0] = jnp.concatenate(head_out, axis=-1).astype(o_ref.dtype)


def sample_slc(page_table, cache_k, cache_v, q, sel, expand, ks_new, vs_new, win_k, win_v, kw_new, vw_new,
               o_cmp, gate, layer, n_layers, n_valid):
    b, n_pages = page_table.shape
    past_len = n_pages * PAGE_SIZE
    steps = n_pages // PAGES_PER_STEP
    rows = HEADS_PER_GROUP * SAMPLE_T_PAD
    per_b = lambda a: pl.BlockSpec((1,) + a.shape[1:], lambda i, c, pt: (i, 0, 0))
    const = lambda a: pl.BlockSpec(a.shape, lambda i, c, pt: (0,) * a.ndim)
    grid_spec = pltpu.PrefetchScalarGridSpec(
        num_scalar_prefetch=1,
        grid=(b, steps),
        in_specs=([_page_spec(k, layer, n_layers) for k in range(PAGES_PER_STEP)] * 2
                  + [per_b(q), pl.BlockSpec((1, 2 * SAMPLE_T_PAD, LANE), lambda i, c, pt: (i, 0, c)),
                     const(expand), per_b(ks_new), per_b(vs_new), per_b(win_k), per_b(win_v),
                     per_b(kw_new), per_b(vw_new), per_b(o_cmp), per_b(gate)]),
        out_specs=pl.BlockSpec((1, SAMPLE_T_PAD, NSA_WIDTH), lambda i, c, pt: (i, 0, 0)),
        scratch_shapes=[pltpu.VMEM((NSA_KV_HEADS * rows, 1), F32), pltpu.VMEM((NSA_KV_HEADS * rows, 1), F32),
                        pltpu.VMEM((NSA_KV_HEADS * rows, KV_COLS), F32)],
    )
    return pl.pallas_call(
        functools.partial(_sample_slc_kernel, n_valid=n_valid, past_len=past_len),
        grid_spec=grid_spec,
        out_shape=jax.ShapeDtypeStruct((b, SAMPLE_T_PAD, NSA_WIDTH), BF16),
        compiler_params=_cparams("parallel", "arbitrary"),
    )(page_table, *([cache_k] * PAGES_PER_STEP), *([cache_v] * PAGES_PER_STEP),
      q, sel, expand, ks_new, vs_new, win_k, win_v, kw_new, vw_new, o_cmp, gate)


def _lru_kernel(*refs, tt, first_pos_zero, fused_in_proj):
    n_lead = 3 if fused_in_proj else 2
    (cpast_ref, h0_ref, cw_ref, cb_ref, wa_ref, wx_ref, ba_ref, bx_ref, lam_ref,
     o_ref, tail_ref, xtail_ref, xbuf, h_carry, a_s, d_s) = refs[n_lead:]
    j = pl.program_id(1)

    @pl.when(j == 0)
    def _():
        xbuf[...] = cpast_ref[0]
        h_carry[...] = h0_ref[0]

    sub = lax.broadcasted_iota(jnp.int32, (8, LRU_WIDTH), 0)
    if fused_in_proj:
        x_ref, g_ref, win_ref = refs[:n_lead]
        h_in = _rms(x_ref[0], g_ref[...]).astype(BF16)
        x_cur = _dot(h_in, win_ref[:, 0:LRU_WIDTH])
        gy = jax.nn.gelu(_dot(h_in, win_ref[:, LRU_WIDTH:]))
    else:
        x_cur = refs[0][0]
        gy = refs[1][0]
    prev8 = xbuf[...]

    def delayed(k):
        if k == 0:
            return x_cur
        rolled = pltpu.roll(x_cur, k, 0)
        head = jnp.where(sub < k, pltpu.roll(prev8, k, 0), rolled[0:8])
        return jnp.concatenate([head, rolled[8:]], axis=0) if tt > 8 else head

    xc = delayed(CONV_WIDTH - 1) * cw_ref[0:1, :]
    for k in range(1, CONV_WIDTH):
        xc = xc + delayed(CONV_WIDTH - 1 - k) * cw_ref[k:k + 1, :]
    xf = xc + cb_ref[...]
    xf_b = xf.astype(BF16)
    r_parts, i_parts = [], []
    for n in range(LRU_BLOCKS):
        cols = slice(n * LRU_BLOCK, (n + 1) * LRU_BLOCK)
        r_parts.append(_dot(xf_b[:, cols], wa_ref[n]))
        i_parts.append(_dot(xf_b[:, cols], wx_ref[n]))
    r = jax.nn.sigmoid(jnp.concatenate(r_parts, axis=-1) + ba_ref[...])
    gate_i = jax.nn.sigmoid(jnp.concatenate(i_parts, axis=-1) + bx_ref[...])
    neg_lam = -lam_ref[...]
    softplus = jnp.maximum(neg_lam, 0.0) + jnp.log1p(jnp.exp(-jnp.abs(neg_lam)))
    log_a = -LRU_C * r * softplus
    a = jnp.exp(log_a)
    mult = jnp.sqrt(-jnp.tanh(log_a) * (a * a + 1.0))
    if first_pos_zero:
        row = lax.broadcasted_iota(jnp.int32, (tt, 1), 0)
        mult = jnp.where((row == 0) & (j == 0), 1.0, mult)
    a_s[...] = a
    d_s[...] = mult * gate_i * xf

    def body(blk, h):
        rows = pl.ds(pl.multiple_of(blk * 8, 8), 8)
        a8 = a_s[rows, :]
        d8 = d_s[rows, :]
        for s in (1, 2, 4):
            keep = sub >= s
            d8 = jnp.where(keep, a8 * pltpu.roll(d8, s, 0) + d8, d8)
            a8 = jnp.where(keep, a8 * pltpu.roll(a8, s, 0), a8)
        hs = a8 * h + d8
        d_s[rows, :] = hs
        return hs[7:8, :]

    n_blk = tt // 8
    h_carry[...] = lax.fori_loop(0, n_blk, body, h_carry[...], unroll=min(4, n_blk))
    hs = d_s[...]
    o_ref[0] = (hs * gy).astype(o_ref.dtype)
    tail_ref[0] = d_s[tt - 8:tt, :]
    xtail_ref[0] = x_cur[tt - 8:tt, :]
    xbuf[...] = x_cur[tt - 8:tt, :]


def lru(lead, conv_past, h0, cw, cb, wa, wx, ba, bx, lam, tt, first_pos_zero, fused_in_proj):
    b, t, _ = lead[0].shape
    w = LRU_WIDTH
    tile = lambda a: pl.BlockSpec((1, tt, a.shape[2]), lambda i, j: (i, j, 0))
    per_b = lambda a: pl.BlockSpec((1,) + a.shape[1:], lambda i, j: (i, 0, 0))
    last8 = pl.BlockSpec((1, 8, w), lambda i, j: (i, 0, 0))
    consts = [cw, cb, wa, wx, ba, bx, lam]
    lead_specs = ([tile(lead[0]), _full(lead[1].shape), _full(lead[2].shape)] if fused_in_proj
                  else [tile(lead[0]), tile(lead[1])])
    return pl.pallas_call(
        functools.partial(_lru_kernel, tt=tt, first_pos_zero=first_pos_zero, fused_in_proj=fused_in_proj),
        grid=(b, t // tt),
        in_specs=lead_specs + [per_b(conv_past), per_b(h0)] + [_full(a.shape) for a in consts],
        out_specs=[pl.BlockSpec((1, tt, w), lambda i, j: (i, j, 0)), last8, last8],
        out_shape=[jax.ShapeDtypeStruct((b, t, w), BF16), jax.ShapeDtypeStruct((b, 8, w), F32),
                   jax.ShapeDtypeStruct((b, 8, w), F32)],
        scratch_shapes=[pltpu.VMEM((8, w), F32), pltpu.VMEM((1, w), F32),
                        pltpu.VMEM((tt, w), F32), pltpu.VMEM((tt, w), F32)],
        compiler_params=_cparams("parallel", "arbitrary"),
    )(*lead, conv_past, h0, *consts)


def _block_mean_matrix(t):
    n_cmp = t // CMP_STRIDE - CMP_BLOCK // CMP_STRIDE + 1
    m = np.zeros((t // CMP_STRIDE, t), np.float32)
    for i in range(n_cmp):
        m[i, i * CMP_STRIDE:i * CMP_STRIDE + CMP_BLOCK] = 1.0 / CMP_BLOCK
    return m


def _sel_from_cmp(n_cmp_rows, n_cmp, n_sel):
    m = np.zeros((n_cmp_rows, n_sel), np.float32)
    for j in range(n_sel):
        for c in range(4 * j - 1, 4 * j + 4):
            if 0 <= c < n_cmp:
                m[c, j] = 1.0
    return m


def _block_diag(w):
    z = jnp.zeros((HEAD_DIM, HEAD_DIM), w.dtype)
    return jnp.concatenate([jnp.concatenate([w[0], z], axis=1), jnp.concatenate([z, w[1]], axis=1)], axis=0)


def _sigmoid(v):
    return jax.nn.sigmoid(v)


def _scale_q(v):
    return v * (HEAD_DIM ** -0.5)


_EVEN_GROUPS = (
    (0, POOL_WIDTH, ((0, POOL_WIDTH, None),)),
    (POOL_WIDTH, NSA_WIDTH, ((0, NSA_WIDTH, _scale_q),)),
    (POOL_WIDTH + NSA_WIDTH, 6 * KV_COLS, tuple((k * KV_COLS, KV_COLS, None) for k in range(6))),
    (POOL_WIDTH + NSA_WIDTH + 6 * KV_COLS, LANE, ((0, LANE, _sigmoid),)),
)
_EVEN_DTYPES = (F32, BF16) + (F32,) * 6 + (F32,)


def _even_in_proj(x2d, g, w_in_pad, tm):
    return norm_matmul(x2d, g, w_in_pad, _EVEN_GROUPS, _EVEN_DTYPES, tm)


def _xattn_block(x2d, b, t, mk, mv, layer, g_pre, g_post, wq, wo, tm, tq):
    q, = norm_matmul(x2d, g_pre, wq, ((0, D_MODEL, ((0, D_MODEL, lambda v: v * (XATTN_HEAD_DIM ** -0.5)),)),),
                     (BF16,), tm)
    o = xattn(q.reshape(b, t, D_MODEL), mk, mv, layer, tq).reshape(b * t, D_MODEL)
    return proj_norm_res([o], [wo], g_post, x2d, tm)


def kernel(x_prompt, mem_prompt, x_sample, state_pool, cache_cmp_k, cache_cmp_v, cache_slc_k, cache_slc_v, cache_win_k, cache_win_v, state_lru_h, state_lru_conv, cache_mem_k, cache_mem_v, page_table, norm_gain, mem_norm_gain, w_in_even, pool_w, pool_scale, w_cmp_k, w_cmp_v, w_out_even, w_in_odd, conv_w, conv_b, lru_wa, lru_ba, lru_wx, lru_bx, lru_lambda, w_out_odd, w_xq, w_xk, w_xv, w_xo, w_ffn_gate, w_ffn_up, w_ffn_down):
    bp, tp, d = x_prompt.shape
    bs, ts, _ = x_sample.shape
    depth = norm_gain.shape[0]
    n_even = w_in_even.shape[0]
    n_pages = page_table.shape[1]
    past_len = n_pages * PAGE_SIZE
    n_phys = cache_cmp_k.shape[0]
    tsp = SAMPLE_T_PAD
    np_tok, ns_tok = bp * tp, bs * tsp
    tm_p, tm_s = 512, ns_tok
    tq, tk = 256, 128

    xp = x_prompt.reshape(np_tok, d)
    xs = jnp.pad(x_sample, ((0, 0), (0, tsp - ts), (0, 0))).reshape(ns_tok, d)

    n_sub_p = tp // CMP_STRIDE
    n_cmp_p = n_sub_p - 1
    n_sel_p = tp // SEL_BLOCK
    pool_mat = jnp.asarray(_block_mean_matrix(tp).T, BF16)
    mselt = jnp.asarray(_sel_from_cmp(n_sub_p, n_cmp_p, n_sel_p).T, BF16)
    ident = jnp.asarray(np.eye(tq, dtype=np.float32), BF16)
    ident_kv = jnp.asarray(np.eye(KV_COLS, dtype=np.float32), BF16)
    subpool = jnp.asarray(np.repeat(np.eye(SUB_PER_STEP, dtype=np.float32), CMP_STRIDE, axis=0), BF16)
    expand_np = np.zeros((LANE, PAGES_PER_STEP * PAGE_SIZE), np.float32)
    expand_np[::SEL_PER_CMP] = np.repeat(np.eye(SEL_PER_STEP, dtype=np.float32), SEL_BLOCK, axis=1)
    expand_s = jnp.asarray(expand_np, BF16)

    row = lambda v: v.reshape(1, -1)
    out_ev_p, out_ev_s, out_lru_p, out_lru_s, mem_k_p, mem_v_p = [], [], [], [], [], []

    for li in range(depth):
        gn = norm_gain[li]
        if li % 2 == 0:
            e = li // 2
            w_in = jnp.pad(w_in_even[e], ((0, 0), (0, EVEN_IN_PAD - EVEN_IN))).astype(BF16)
            pw = pool_w[e].astype(BF16)
            ps = row(pool_scale[e])
            wk_bd_t = _block_diag(w_cmp_k[e]).T.astype(BF16)
            wv_bd_t = _block_diag(w_cmp_v[e]).T.astype(BF16)
            w_out = w_out_even[e].astype(BF16)
            kv0 = POOL_WIDTH + NSA_WIDTH
            kcol = lambda k: slice(kv0 + k * KV_COLS, kv0 + (k + 1) * KV_COLS)
            w_u = w_in[:, :POOL_WIDTH]
            w_kpm = jnp.concatenate([w_in[:, kcol(2)], w_in[:, kcol(4)]], axis=1)
            w_t = w_in[:, POOL_WIDTH:kv0 + 7 * KV_COLS].T

            u, kpm, q_t, gate_t, kc, vc, ks, vs, kw, vw = even_in_prompt(xp, row(gn[0]), w_u, w_kpm, w_t,
                                                                         bp, tp, tm_p)
            u3 = u.reshape(bp, tp, POOL_WIDTH)
            a_out = pool_mix(u3, jnp.zeros((bp, POOL_HALO, POOL_WIDTH), F32), pw, ps, 0)
            kcmp, vcmp_t = compress_kv(kc, vc, pool_mat, wk_bd_t, wv_bd_t, ident_kv)
            o_nsa = nsa_prompt(q_t, gate_t, kcmp, vcmp_t, kpm.reshape(bp, tp, 2 * KV_COLS), vs, vw,
                               mselt, ident, tq, tk)
            mix_p = ([a_out, o_nsa], [w_out[:POOL_WIDTH], w_out[POOL_WIDTH:]])
            kv5 = lambda a: a.reshape(bp, NSA_KV_HEADS, HEAD_DIM, a.shape[-1]).transpose(0, 3, 1, 2)
            n_keep = min(WINDOW, tp)
            out_ev_p.append((u3[:, -POOL_STATE:], kv5(kc), kv5(vc), kv5(ks), kv5(vs),
                             kv5(kw[:, :, -n_keep:]), kv5(vw[:, :, -n_keep:])))

            u, q, kc, vc, ks, vs, kw, vw, gate = _even_in_proj(xs, row(gn[0]), w_in, tm_s)
            as3 = lambda a: a.reshape(bs, tsp, a.shape[-1])
            u3 = as3(u)
            past_pool = jnp.pad(state_pool[e], ((0, 0), (POOL_HALO - POOL_STATE, 0), (0, 0)))
            a_out = pool_mix(u3, past_pool, pw, ps, past_len)
            paged = lambda c: c.transpose(0, 1, 3, 4, 2).reshape(n_phys * n_even, KV_COLS, PAGE_SIZE)
            o_cmp, sel = sample_cmp(page_table, paged(cache_cmp_k), paged(cache_cmp_v), as3(kc), as3(vc),
                                    as3(q), wk_bd_t, wv_bd_t, subpool, ident_kv, e, n_even, ts)
            n_win = cache_win_k.shape[2]
            win_k = cache_win_k[e].transpose(0, 2, 3, 1).reshape(bs, KV_COLS, n_win)
            win_v = cache_win_v[e].transpose(0, 2, 3, 1).reshape(bs, KV_COLS, n_win)
            o_nsa = sample_slc(page_table, paged(cache_slc_k), paged(cache_slc_v), as3(q), sel, expand_s,
                               as3(ks), as3(vs), win_k, win_v, as3(kw), as3(vw), o_cmp, as3(gate),
                               e, n_even, ts)
            xs = proj_norm_res([a_out.reshape(ns_tok, POOL_WIDTH), o_nsa.reshape(ns_tok, NSA_WIDTH)],
                               [w_out[:POOL_WIDTH], w_out[POOL_WIDTH:]], row(gn[1]), xs, tm_s)
            new4 = lambda a: as3(a)[:, :ts].reshape(bs, ts, NSA_KV_HEADS, HEAD_DIM)
            out_ev_s.append((jnp.concatenate([state_pool[e], u3[:, :ts]], axis=1)[:, -POOL_STATE:],
                             new4(kc), new4(vc), new4(ks), new4(vs),
                             jnp.concatenate([cache_win_k[e], new4(kw)], axis=1)[:, -n_win:],
                             jnp.concatenate([cache_win_v[e], new4(vw)], axis=1)[:, -n_win:]))
        else:
            o = li // 2
            w_in = w_in_odd[o].astype(BF16)
            cw = jnp.pad(conv_w[o], ((0, 8 - CONV_WIDTH), (0, 0)))
            consts = (cw, row(conv_b[o]), lru_wa[o].astype(BF16), lru_wx[o].astype(BF16),
                      row(lru_ba[o]), row(lru_bx[o]), row(lru_lambda[o]))
            w_out = w_out_odd[o].astype(BF16)
            groups = ((0, LRU_WIDTH, ((0, LRU_WIDTH, None),)),
                      (LRU_WIDTH, LRU_WIDTH, ((0, LRU_WIDTH, jax.nn.gelu),)))

            gated, tail, xb_tail = lru((xp.reshape(bp, tp, d), row(gn[0]), w_in),
                                       jnp.zeros((bp, 8, LRU_WIDTH), F32), jnp.zeros((bp, 1, LRU_WIDTH), F32),
                                       *consts, tt=256, first_pos_zero=True, fused_in_proj=True)
            mix_p = ([gated], [w_out])
            out_lru_p.append((xb_tail[:, -(CONV_WIDTH - 1):], tail[:, 7]))

            xb, yb = norm_matmul(xs, row(gn[0]), w_in, groups, (F32, F32), tm_s)
            xb3, yb3 = xb.reshape(bs, tsp, LRU_WIDTH), yb.reshape(bs, tsp, LRU_WIDTH)
            conv_past = jnp.pad(state_lru_conv[o], ((0, 0), (8 - (CONV_WIDTH - 1), 0), (0, 0)))
            gated, tail, _ = lru((xb3, yb3), conv_past, state_lru_h[o][:, None, :], *consts, tt=tsp,
                                 first_pos_zero=False, fused_in_proj=False)
            xs = proj_norm_res([gated.reshape(ns_tok, LRU_WIDTH)], [w_out], row(gn[1]), xs, tm_s)
            conv_new = jnp.concatenate([state_lru_conv[o], xb3[:, :ts]], axis=1)[:, -(CONV_WIDTH - 1):]
            out_lru_s.append((conv_new, tail[:, ts - 1]))

        wq, wo = w_xq[li].astype(BF16), w_xo[li].astype(BF16)
        wkv = jnp.concatenate([w_xk[li], w_xv[li]], axis=1).astype(BF16)
        mem2d = mem_prompt.reshape(bp * N_MEM, d)
        kv_groups = ((0, D_MODEL, ((0, D_MODEL, None),)), (D_MODEL, D_MODEL, ((0, D_MODEL, None),)))
        mk, mv = norm_matmul(mem2d, row(mem_norm_gain[li]), wkv, kv_groups, (F32, F32), min(tm_p, bp * N_MEM))
        mk3, mv3 = mk.reshape(bp, N_MEM, d), mv.reshape(bp, N_MEM, d)
        mem_k_p.append(mk3.reshape(bp, N_MEM, XATTN_HEADS, XATTN_HEAD_DIM))
        mem_v_p.append(mv3.reshape(bp, N_MEM, XATTN_HEADS, XATTN_HEAD_DIM))
        wg, wu, wd = w_ffn_gate[li].astype(BF16), w_ffn_up[li].astype(BF16), w_ffn_down[li].astype(BF16)
        xp = post_mixer(mix_p[0], mix_p[1], xp.reshape(bp, tp, d), gn, mk3, mv3, wq, wo, wg, wu, wd,
                        tm_p).reshape(np_tok, d)
        xs = _xattn_block(xs, bs, tsp, cache_mem_k, cache_mem_v, li, row(gn[2]), row(gn[3]), wq, wo, tm_s, tsp)
        xs = ffn(xs, row(gn[4]), row(gn[5]), wg, wu, wd, tm_s)

    stack = lambda items, k, axis=0: jnp.stack([s[k] for s in items], axis=axis)
    y_prompt = xp.reshape(bp, tp, d)
    y_sample = xs.reshape(bs, tsp, d)[:, :ts]
    return (y_prompt, y_sample,
            stack(out_ev_p, 0), stack(out_ev_s, 0),
            stack(out_ev_p, 1, 1), stack(out_ev_s, 1, 1),
            stack(out_ev_p, 2, 1), stack(out_ev_s, 2, 1),
            stack(out_ev_p, 3, 1), stack(out_ev_s, 3, 1),
            stack(out_ev_p, 4, 1), stack(out_ev_s, 4, 1),
            stack(out_ev_p, 5), stack(out_ev_s, 5),
            stack(out_ev_p, 6), stack(out_ev_s, 6),
            stack(out_lru_p, 1), stack(out_lru_s, 1),
            stack(out_lru_p, 0), stack(out_lru_s, 0),
            jnp.stack(mem_k_p), jnp.stack(mem_v_p))
```

```python
import functools

import numpy as np
import jax
import jax.numpy as jnp
from jax import lax
from jax.experimental import pallas as pl
from jax.experimental.pallas import tpu as pltpu

F32 = jnp.float32
BF16 = jnp.bfloat16

D_MODEL = 1024
RMS_EPS = 1e-6
NEG_INF = -1e30
FORCE_SCORE = 1e30
REMOVED_SCORE = -3e38
LOG2E = 1.4426950408889634
DENOM_ROWS = 8
M_INIT = -1e29

POOL_WIDTH = 512
POOL_WINDOWS = (2, 4, 8, 16)
POOL_GROUP = 128
POOL_STATE = 15
POOL_HALO = 16

HEAD_DIM = 64
NSA_HEADS = 8
NSA_KV_HEADS = 2
HEADS_PER_GROUP = NSA_HEADS // NSA_KV_HEADS
NSA_WIDTH = 512
KV_COLS = 128
CMP_BLOCK = 32
CMP_STRIDE = 16
SEL_BLOCK = 64
SEL_TOPK = 16
WINDOW = 512
PAGE_SIZE = 128
EVEN_IN = POOL_WIDTH + NSA_WIDTH + 6 * KV_COLS + 3 * NSA_HEADS
EVEN_IN_PAD = 1920

LRU_WIDTH = 1024
LRU_BLOCKS = 4
LRU_BLOCK = 256
CONV_WIDTH = 4
LRU_C = 8.0

N_MEM = 256
XATTN_HEADS = 4
XATTN_HEAD_DIM = 256
D_FF = 2816
FF_CHUNK = 256

SAMPLE_T_PAD = 8
PAGES_PER_STEP = 16
SEL_PER_STEP = PAGES_PER_STEP * PAGE_SIZE // SEL_BLOCK
SUB_PER_STEP = PAGES_PER_STEP * PAGE_SIZE // CMP_STRIDE
SEL_PER_CMP = SEL_BLOCK // CMP_STRIDE
LANE = 128

VMEM_LIMIT_BYTES = 56 * 1024 * 1024


def _cparams(*sem):
    return pltpu.CompilerParams(dimension_semantics=sem, vmem_limit_bytes=VMEM_LIMIT_BYTES)


def _rms(x, g):
    return x * lax.rsqrt(jnp.mean(x * x, axis=-1, keepdims=True) + RMS_EPS) * g


def _dot(a, b):
    return jnp.dot(a, b, preferred_element_type=F32)


def _dot_nt(a, b):
    return lax.dot_general(a, b, (((1,), (1,)), ((), ())), preferred_element_type=F32)


def _split3(x):
    p1 = x.astype(BF16)
    r1 = x - p1.astype(F32)
    p2 = r1.astype(BF16)
    p3 = (r1 - p2.astype(F32)).astype(BF16)
    return p1, p2, p3


def _full(shape):
    n = len(shape)
    return pl.BlockSpec(shape, lambda *_: (0,) * n)


def _norm_matmul_kernel(x_ref, g_ref, w_ref, *o_refs, groups):
    h = _rms(x_ref[...], g_ref[...]).astype(BF16)
    k = 0
    for start, width, outs in groups:
        z = _dot(h, w_ref[:, start:start + width])
        for off, w, post in outs:
            v = z[:, off:off + w]
            if post is not None:
                v = post(v)
            o_refs[k][...] = v.astype(o_refs[k].dtype)
            k += 1


def norm_matmul(x, g, w, groups, out_dtypes, tm):
    n, d = x.shape
    widths = [w_ for _, _, outs in groups for _, w_, _ in outs]
    return pl.pallas_call(
        functools.partial(_norm_matmul_kernel, groups=groups),
        grid=(n // tm,),
        in_specs=[pl.BlockSpec((tm, d), lambda i: (i, 0)), _full(g.shape), _full(w.shape)],
        out_specs=[pl.BlockSpec((tm, wd), lambda i: (i, 0)) for wd in widths],
        out_shape=[jax.ShapeDtypeStruct((n, wd), dt) for wd, dt in zip(widths, out_dtypes)],
        compiler_params=_cparams("parallel"),
    )(x, g, w)


def _even_in_prompt_kernel(x_ref, g_ref, wu_ref, wk_ref, wt_ref, u_ref, kpm_ref, qt_ref, gatet_ref, *kvt_refs):
    h = _rms(x_ref[...], g_ref[...]).astype(BF16)
    u_ref[...] = _dot(h, wu_ref[...])
    kpm_ref[...] = _dot(h, wk_ref[...]).astype(kpm_ref.dtype)
    z_t = _dot_nt(wt_ref[...], h)
    qt_ref[0] = (z_t[0:NSA_WIDTH] * (HEAD_DIM ** -0.5 * LOG2E)).astype(qt_ref.dtype)
    for k, ref in enumerate(kvt_refs):
        ref[0] = z_t[NSA_WIDTH + k * KV_COLS:NSA_WIDTH + (k + 1) * KV_COLS, :]
    gatet_ref[0] = jax.nn.sigmoid(z_t[NSA_WIDTH + 6 * KV_COLS:, :])


def even_in_prompt(x, g, w_u, w_kpm, w_t, b, t, tm):
    n, d = x.shape
    per_seq = t // tm
    row_tile = lambda w: pl.BlockSpec((tm, w), lambda i: (i, 0))
    feat_tile = lambda f: pl.BlockSpec((1, f, tm), lambda i: (i // per_seq, 0, i % per_seq))
    n_kv = 6
    return pl.pallas_call(
        _even_in_prompt_kernel,
        grid=(n // tm,),
        in_specs=[row_tile(d), _full(g.shape), _full(w_u.shape), _full(w_kpm.shape), _full(w_t.shape)],
        out_specs=([row_tile(POOL_WIDTH), row_tile(2 * KV_COLS), feat_tile(NSA_WIDTH), feat_tile(LANE)]
                   + [feat_tile(KV_COLS)] * n_kv),
        out_shape=([jax.ShapeDtypeStruct((n, POOL_WIDTH), F32), jax.ShapeDtypeStruct((n, 2 * KV_COLS), BF16),
                    jax.ShapeDtypeStruct((b, NSA_WIDTH, t), BF16), jax.ShapeDtypeStruct((b, LANE, t), F32)]
                   + [jax.ShapeDtypeStruct((b, KV_COLS, t), F32)] * n_kv),
        compiler_params=_cparams("parallel"),
    )(x, g, w_u, w_kpm, w_t)


def _proj_norm_res_kernel(*refs, n_in):
    a_refs, w_refs = refs[:n_in], refs[n_in:2 * n_in]
    g_ref, x_ref, o_ref = refs[2 * n_in:]
    acc = _dot(a_refs[0][...], w_refs[0][...])
    for a_ref, w_ref in zip(a_refs[1:], w_refs[1:]):
        acc = acc + _dot(a_ref[...], w_ref[...])
    o_ref[...] = x_ref[...] + _rms(acc, g_ref[...])


def proj_norm_res(a_list, w_list, g, x, tm):
    n, d = x.shape
    n_in = len(a_list)
    return pl.pallas_call(
        functools.partial(_proj_norm_res_kernel, n_in=n_in),
        grid=(n // tm,),
        in_specs=([pl.BlockSpec((tm, a.shape[1]), lambda i: (i, 0)) for a in a_list]
                  + [_full(w.shape) for w in w_list]
                  + [_full(g.shape), pl.BlockSpec((tm, d), lambda i: (i, 0))]),
        out_specs=pl.BlockSpec((tm, d), lambda i: (i, 0)),
        out_shape=jax.ShapeDtypeStruct((n, d), F32),
        compiler_params=_cparams("parallel"),
    )(*a_list, *w_list, g, x)


def _ffn_block(x, g_in, g_out, wg_ref, wu_ref, wd_ref):
    h = _rms(x, g_in).astype(BF16)
    acc = None
    for c in range(D_FF // FF_CHUNK):
        cols = slice(c * FF_CHUNK, (c + 1) * FF_CHUNK)
        gate = _dot(h, wg_ref[:, cols])
        up = _dot(h, wu_ref[:, cols])
        act = (jax.nn.silu(gate) * up).astype(BF16)
        part = _dot(act, wd_ref[cols, :])
        acc = part if acc is None else acc + part
    return x + _rms(acc, g_out)


def _ffn_kernel(x_ref, g_in_ref, g_out_ref, wg_ref, wu_ref, wd_ref, o_ref):
    o_ref[...] = _ffn_block(x_ref[...], g_in_ref[...], g_out_ref[...], wg_ref, wu_ref, wd_ref)


def ffn(x, g_in, g_out, wg, wu, wd, tm):
    n, d = x.shape
    resident = functools.partial(pl.BlockSpec, pipeline_mode=pl.Buffered(1))
    return pl.pallas_call(
        _ffn_kernel,
        grid=(n // tm,),
        in_specs=[pl.BlockSpec((tm, d), lambda i: (i, 0)), _full(g_in.shape), _full(g_out.shape),
                  resident(wg.shape, lambda i: (0, 0)), resident(wu.shape, lambda i: (0, 0)),
                  resident(wd.shape, lambda i: (0, 0))],
        out_specs=pl.BlockSpec((tm, d), lambda i: (i, 0)),
        out_shape=jax.ShapeDtypeStruct((n, d), F32),
        compiler_params=_cparams("parallel"),
    )(x, g_in, g_out, wg, wu, wd)


def _xattn_heads(q, mk_ref, mv_ref):
    head_cols = [slice(hd * XATTN_HEAD_DIM, (hd + 1) * XATTN_HEAD_DIM) for hd in range(XATTN_HEADS)]
    if len(mk_ref.shape) == 4:
        rows = q.shape[0]
        n_kv = N_MEM * XATTN_HEADS
        k_all = mk_ref[0].reshape(n_kv, XATTN_HEAD_DIM).astype(BF16)
        v_all = mv_ref[0].reshape(n_kv, XATTN_HEAD_DIM).astype(BF16)
        q_st = jnp.concatenate([q[:, cols] for cols in head_cols], axis=0)
        own = (lax.broadcasted_iota(jnp.int32, (1, n_kv), 1) % XATTN_HEADS
               == lax.broadcasted_iota(jnp.int32, (XATTN_HEADS * rows, 1), 0) // rows)
        p = _masked_softmax_rows(_dot_nt(q_st, k_all), own)
        o_st = _dot(p.astype(BF16), v_all).astype(BF16)
        return jnp.concatenate([o_st[hd * rows:(hd + 1) * rows] for hd in range(XATTN_HEADS)], axis=-1)
    outs = []
    for cols in head_cols:
        k = mk_ref[0, :, cols].astype(BF16)
        v = mv_ref[0, :, cols].astype(BF16)
        s = _dot_nt(q[:, cols], k)
        e = jnp.exp(s - jnp.max(s, axis=-1, keepdims=True))
        p = e / jnp.sum(e, axis=-1, keepdims=True)
        outs.append(_dot(p.astype(BF16), v).astype(BF16))
    return jnp.concatenate(outs, axis=-1)


def _xattn_kernel(q_ref, mk_ref, mv_ref, o_ref):
    o_ref[0] = _xattn_heads(q_ref[0], mk_ref, mv_ref)


def xattn(q, mk, mv, layer, tq):
    b, t, d = q.shape
    mem_spec = pl.BlockSpec((None, 1) + mk.shape[2:], lambda i, j: (layer, i, 0, 0, 0))
    return pl.pallas_call(
        _xattn_kernel,
        grid=(b, t // tq),
        in_specs=[pl.BlockSpec((1, tq, d), lambda i, j: (i, j, 0)), mem_spec, mem_spec],
        out_specs=pl.BlockSpec((1, tq, d), lambda i, j: (i, j, 0)),
        out_shape=jax.ShapeDtypeStruct((b, t, d), BF16),
        compiler_params=_cparams("parallel", "parallel"),
    )(q, mk, mv)


def _post_mixer_kernel(*refs, n_in):
    a_refs, w_refs = refs[:n_in], refs[n_in:2 * n_in]
    x_ref, gn_ref, mk_ref, mv_ref, wq_ref, wo_ref, wg_ref, wu_ref, wd_ref, o_ref = refs[2 * n_in:]
    gain = lambda k: gn_ref[k:k + 1, :]
    mix = _dot(a_refs[0][0], w_refs[0][...])
    for a_ref, w_ref in zip(a_refs[1:], w_refs[1:]):
        mix = mix + _dot(a_ref[0], w_ref[...])
    x1 = x_ref[0] + _rms(mix, gain(1))
    q = (_dot(_rms(x1, gain(2)).astype(BF16), wq_ref[...]) * (XATTN_HEAD_DIM ** -0.5)).astype(BF16)
    attn = _xattn_heads(q, mk_ref, mv_ref)
    x2 = x1 + _rms(_dot(attn, wo_ref[...]), gain(3))
    o_ref[0] = _ffn_block(x2, gain(4), gain(5), wg_ref, wu_ref, wd_ref)


def post_mixer(a_list, w_list, x, gn, mk, mv, wq, wo, wg, wu, wd, tm):
    b, t, d = x.shape
    n_in = len(a_list)
    resident = lambda a: pl.BlockSpec(a.shape, lambda i, j: (0,) * a.ndim, pipeline_mode=pl.Buffered(1))
    tile = lambda w: pl.BlockSpec((1, tm, w), lambda i, j: (i, j, 0))
    per_b = lambda a: pl.BlockSpec((1,) + a.shape[1:], lambda i, j: (i, 0, 0))
    return pl.pallas_call(
        functools.partial(_post_mixer_kernel, n_in=n_in),
        grid=(b, t // tm),
        in_specs=([tile(a.shape[2]) for a in a_list] + [resident(w) for w in w_list]
                  + [tile(d), resident(gn), per_b(mk), per_b(mv)]
                  + [resident(w) for w in (wq, wo, wg, wu, wd)]),
        out_specs=tile(d),
        out_shape=jax.ShapeDtypeStruct((b, t, d), F32),
        compiler_params=_cparams("parallel", "parallel"),
    )(*a_list, *w_list, x, gn, mk, mv, wq, wo, wg, wu, wd)


def _pool_kernel(u_ref, past_ref, w_ref, scale_ref, o_ref, buf, *, t, chunk, start_pos):
    buf[0:POOL_HALO, :] = past_ref[0]
    buf[POOL_HALO:POOL_HALO + t, :] = u_ref[0]
    for c in range(t // chunk):
        base = POOL_HALO + c * chunk
        pos = start_pos + c * chunk + lax.broadcasted_iota(jnp.int32, (chunk, 1), 0)
        for gi, win in enumerate(POOL_WINDOWS):
            cols = slice(gi * POOL_GROUP, (gi + 1) * POOL_GROUP)
            cur = buf[base:base + chunk, cols]
            win_sum = cur
            for k in range(1, win):
                win_sum = win_sum + buf[base - k:base - k + chunk, cols]
            cnt = jnp.minimum(pos + 1, win).astype(F32)
            diff = win_sum / cnt - cur
            y = _dot(diff.astype(BF16), w_ref[gi]) * scale_ref[:, cols]
            o_ref[0, c * chunk:(c + 1) * chunk, cols] = y.astype(o_ref.dtype)


def pool_mix(u, past, w, scale, start_pos):
    b, t, c = u.shape
    chunk = min(t, 256)
    return pl.pallas_call(
        functools.partial(_pool_kernel, t=t, chunk=chunk, start_pos=start_pos),
        grid=(b,),
        in_specs=[pl.BlockSpec((1, t, c), lambda i: (i, 0, 0)),
                  pl.BlockSpec((1, POOL_HALO, c), lambda i: (i, 0, 0)),
                  _full(w.shape), _full(scale.shape)],
        out_specs=pl.BlockSpec((1, t, c), lambda i: (i, 0, 0)),
        out_shape=jax.ShapeDtypeStruct((b, t, c), BF16),
        scratch_shapes=[pltpu.VMEM((POOL_HALO + t, c), F32)],
        compiler_params=_cparams("parallel"),
    )(u, past, w, scale)


def _compress_kernel(kc_ref, vc_ref, pool_ref, wkt_ref, wvt_ref, ident_ref, k_ref, vt_ref):
    def compress(src, wt_ref):
        x = src[0]
        hi = x.astype(BF16)
        lo = (x - hi.astype(F32)).astype(BF16)
        mean_t = _dot(hi, pool_ref[...]) + _dot(lo, pool_ref[...])
        return _dot(wt_ref[...], mean_t.astype(BF16)).astype(BF16)
    k_ref[0] = _dot_nt(ident_ref[...], compress(kc_ref, wkt_ref)).astype(BF16)
    vt_ref[0] = compress(vc_ref, wvt_ref)


def compress_kv(kc_t, vc_t, pool_mat, wk_bd_t, wv_bd_t, ident):
    b, c, t = kc_t.shape
    n_cmp = pool_mat.shape[1]
    per_b = pl.BlockSpec((1, c, t), lambda i: (i, 0, 0))
    return pl.pallas_call(
        _compress_kernel,
        grid=(b,),
        in_specs=[per_b, per_b, _full(pool_mat.shape), _full(wk_bd_t.shape), _full(wv_bd_t.shape),
                  _full(ident.shape)],
        out_specs=[pl.BlockSpec((1, n_cmp, c), lambda i: (i, 0, 0)), pl.BlockSpec((1, c, n_cmp), lambda i: (i, 0, 0))],
        out_shape=[jax.ShapeDtypeStruct((b, n_cmp, c), BF16), jax.ShapeDtypeStruct((b, c, n_cmp), BF16)],
        compiler_params=_cparams("parallel"),
    )(kc_t, vc_t, pool_mat, wk_bd_t, wv_bd_t, ident)


def _masked_softmax_rows(s, mask):
    s = jnp.where(mask, s, NEG_INF)
    e = jnp.where(mask, jnp.exp(s - jnp.max(s, axis=-1, keepdims=True)), 0.0)
    return e / jnp.maximum(jnp.sum(e, axis=-1, keepdims=True), 1e-30)


def _flash_step(raw_ref, v_aug, mask, m_ref, acc_ref):
    tq = mask.shape[1]
    bias = jnp.where(mask, 0.0, NEG_INF)
    for hh in range(HEADS_PER_GROUP):
        lanes = slice(hh * tq, (hh + 1) * tq)
        s = raw_ref[:, lanes] + bias
        m_old = m_ref[:, lanes]
        m_new = jnp.maximum(m_old, jnp.max(s, axis=0, keepdims=True))
        alpha = jnp.exp2(m_old - m_new)
        e = jnp.exp2(s - m_new)
        m_ref[:, lanes] = m_new
        acc_ref[:, lanes] = alpha * acc_ref[:, lanes] + _dot(v_aug, e.astype(BF16))


def _nsa_prompt_kernel(qt_ref, gatet_ref, kcmp_ref, vcmpt_ref, kpm_ref, vst_ref, vwt_ref,
                       mselt_ref, ident_ref, o_ref, sel_s, raw_s, m_s, acc_s, *, tq, tk):
    i = pl.program_id(1)
    q0 = i * tq
    n_cmp = kcmp_ref.shape[1]
    n_sel = mselt_ref.shape[0]
    blocks_per_tile = tk // SEL_BLOCK
    tiles_per_q = tq // tk
    qpos_row = q0 + lax.broadcasted_iota(jnp.int32, (1, tq), 1)
    key_col = lax.broadcasted_iota(jnp.int32, (tk, 1), 0)
    cmp_end_col = lax.broadcasted_iota(jnp.int32, (n_cmp, 1), 0) * CMP_STRIDE + (CMP_BLOCK - 1)
    cmp_bias = jnp.where(cmp_end_col <= qpos_row, 0.0, NEG_INF)
    sel_j = lax.broadcasted_iota(jnp.int32, (n_sel, 1), 0)
    cur_row = qpos_row // SEL_BLOCK
    gate_t = gatet_ref[0]
    zero_half = jnp.zeros((HEAD_DIM, HEADS_PER_GROUP * tq), BF16)

    groups = range(NSA_KV_HEADS)
    grows = [slice(g * HEAD_DIM, (g + 1) * HEAD_DIM) for g in groups]
    q_pads, o_cmps = [], []
    for g in groups:
        q_t_g = jnp.concatenate(
            [qt_ref[0, (g * HEADS_PER_GROUP + hh) * HEAD_DIM:(g * HEADS_PER_GROUP + hh + 1) * HEAD_DIM, :]
             for hh in range(HEADS_PER_GROUP)], axis=1)
        q_pad = jnp.concatenate([q_t_g, zero_half] if g == 0 else [zero_half, q_t_g], axis=0)
        q_pads.append(q_pad)

        s_t = _dot(kcmp_ref[0], q_pad) + jnp.concatenate([cmp_bias] * HEADS_PER_GROUP, axis=1)
        e_t = jnp.exp2(s_t - jnp.maximum(jnp.max(s_t, axis=0, keepdims=True), M_INIT))
        p_t = e_t / jnp.maximum(jnp.sum(e_t, axis=0, keepdims=True), 1e-30)
        o_cmps.append(_dot(vcmpt_ref[0, grows[g], :], p_t.astype(BF16)))
        psum_t = p_t[:, 0:tq]
        for hh in range(1, HEADS_PER_GROUP):
            psum_t = psum_t + p_t[:, hh * tq:(hh + 1) * tq]

        imp_t = None
        for part in _split3(psum_t):
            d = _dot(mselt_ref[...], part)
            imp_t = d if imp_t is None else imp_t + d
        valid = sel_j * SEL_BLOCK <= qpos_row
        forced = (sel_j == 0) | (sel_j == cur_row) | (sel_j == cur_row - 1)
        score = jnp.where(forced, FORCE_SCORE, jnp.where(valid, imp_t, NEG_INF))
        rank = jnp.zeros((n_sel, tq), F32)
        for jp in range(n_sel):
            sj = score[jp:jp + 1, :]
            beats = (sj > score) | ((sj == score) & (sel_j > jp))
            rank = rank + jnp.where(beats, 1.0, 0.0)
        sel_s[g] = jnp.where(rank < SEL_TOPK, 1.0, 0.0)

    m_s[...] = jnp.full(m_s.shape, M_INIT, F32)
    acc_s[...] = jnp.zeros(acc_s.shape, F32)
    ones_rows = jnp.ones((DENOM_ROWS, tk), BF16)
    with_ones = lambda v_t: jnp.concatenate([v_t.astype(BF16), ones_rows], axis=0)
    n_tiles = (i + 1) * tiles_per_q
    first_win = jnp.maximum((i * tq - (WINDOW - 1)) // tk, 0)
    key_rows = lambda kt: pl.ds(pl.multiple_of(kt * tk, tk), tk)

    def issue_scores(kt, slot, branch):
        k = kpm_ref[0, key_rows(kt), branch * KV_COLS:(branch + 1) * KV_COLS]
        for g in groups:
            raw_s[slot, NSA_KV_HEADS * branch + g] = _dot(k, q_pads[g])

    def slc_update(kt, slot):
        causal = kt * tk + key_col <= qpos_row
        for g in groups:
            chosen = jnp.concatenate(
                [jnp.broadcast_to(sel_s[g, pl.ds(kt * blocks_per_tile + r, 1), :], (SEL_BLOCK, tq))
                 for r in range(blocks_per_tile)], axis=0)
            v_aug = with_ones(vst_ref[0, grows[g], key_rows(kt)])
            _flash_step(raw_s.at[slot, g], v_aug, (chosen > 0.5) & causal, m_s.at[g], acc_s.at[g])

    def win_update(kt, slot):
        dist = qpos_row - (kt * tk + key_col)
        mask = (dist >= 0) & (dist < WINDOW)
        for g in groups:
            c = NSA_KV_HEADS + g
            v_aug = with_ones(vwt_ref[0, grows[g], key_rows(kt)])
            _flash_step(raw_s.at[slot, c], v_aug, mask, m_s.at[c], acc_s.at[c])

    def far_body(pair, _):
        for slot in (0, 1):
            kt = 2 * pair + slot
            issue_scores(kt + 1, 1 - slot, 0)
            slc_update(kt, slot)
        return 0

    def band_body(pair, _):
        for slot in (0, 1):
            kt = 2 * pair + slot
            nxt = jnp.minimum(kt + 1, n_tiles - 1)
            issue_scores(nxt, 1 - slot, 0)
            issue_scores(nxt, 1 - slot, 1)
            slc_update(kt, slot)
            win_update(kt, slot)
        return 0

    issue_scores(0, 0, 0)
    lax.fori_loop(0, first_win // 2, far_body, 0)
    issue_scores(first_win, 0, 1)
    lax.fori_loop(first_win // 2, n_tiles // 2, band_body, 0)

    head_out = [None] * NSA_HEADS
    for g in groups:
        normalised = lambda c: acc_s[c, 0:HEAD_DIM, :] / acc_s[c, HEAD_DIM:HEAD_DIM + 1, :]
        o_slc = normalised(g)
        o_win = normalised(NSA_KV_HEADS + g)
        for hh in range(HEADS_PER_GROUP):
            h = g * HEADS_PER_GROUP + hh
            lanes = slice(hh * tq, (hh + 1) * tq)
            head_out[h] = (gate_t[3 * h:3 * h + 1, :] * o_cmps[g][:, lanes]
                           + gate_t[3 * h + 1:3 * h + 2, :] * o_slc[:, lanes]
                           + gate_t[3 * h + 2:3 * h + 3, :] * o_win[:, lanes])
    o_t = jnp.concatenate(head_out, axis=0).astype(BF16)
    o_ref[0] = _dot_nt(ident_ref[...], o_t).astype(o_ref.dtype)


def nsa_prompt(q_t, gate_t, kcmp, vcmp_t, kpm, vs_t, vw_t, mselt, ident, tq, tk):
    b, _, t = q_t.shape
    feat_tile = lambda f: pl.BlockSpec((1, f, tq), lambda i, j: (i, 0, j))
    whole = lambda a: pl.BlockSpec((1,) + a.shape[1:], lambda i, j: (i, 0, 0))
    per_batch = [kcmp, vcmp_t, kpm, vs_t, vw_t]
    n_chain = 2 * NSA_KV_HEADS
    assert tq % (2 * tk) == 0 and WINDOW % (2 * tk) == 0 and tk % SEL_BLOCK == 0
    return pl.pallas_call(
        functools.partial(_nsa_prompt_kernel, tq=tq, tk=tk),
        grid=(b, t // tq),
        in_specs=([feat_tile(NSA_WIDTH), feat_tile(LANE)] + [whole(a) for a in per_batch]
                  + [_full(mselt.shape), _full(ident.shape)]),
        out_specs=pl.BlockSpec((1, tq, NSA_WIDTH), lambda i, j: (i, j, 0)),
        out_shape=jax.ShapeDtypeStruct((b, t, NSA_WIDTH), BF16),
        scratch_shapes=[pltpu.VMEM((NSA_KV_HEADS, mselt.shape[0], tq), F32),
                        pltpu.VMEM((2, n_chain, tk, HEADS_PER_GROUP * tq), F32),
                        pltpu.VMEM((n_chain, 1, HEADS_PER_GROUP * tq), F32),
                        pltpu.VMEM((n_chain, HEAD_DIM + DENOM_ROWS, HEADS_PER_GROUP * tq), F32)],
        compiler_params=_cparams("parallel", "arbitrary"),
    )(q_t, gate_t, *per_batch, mselt, ident)


def _stack_heads(q, g):
    return jnp.concatenate(
        [q[:, (g * HEADS_PER_GROUP + hh) * HEAD_DIM:(g * HEADS_PER_GROUP + hh + 1) * HEAD_DIM]
         for hh in range(HEADS_PER_GROUP)], axis=0)


def _sample_cmp_kernel(pt_ref, k_hbm, v_hbm, kn_ref, vn_ref, q_ref, wkt_ref, wvt_ref, subpool_ref, ident_ref,
                       ocmp_ref, sel_ref, sub_k, sub_v, kbuf, vbuf, sems, *, n_valid, past_len, layer, n_layers):
    p = PAGES_PER_STEP
    tp = SAMPLE_T_PAD
    c = pl.program_id(1)
    n_sub = past_len // CMP_STRIDE
    steps = past_len // (p * PAGE_SIZE)

    pages = _fetch_pages(pt_ref, k_hbm, v_hbm, kbuf, vbuf, sems, layer, n_layers)
    for x, sub in zip(pages, (sub_k, sub_v)):
        hi = x.astype(BF16)
        lo = (x - hi.astype(F32)).astype(BF16)
        sub[c] = _dot(hi, subpool_ref[...]) + _dot(lo, subpool_ref[...])

    @pl.when(c == steps - 1)
    def _():
        t_col = lax.broadcasted_iota(jnp.int32, (tp, 1), 0)
        new_valid = lax.broadcasted_iota(jnp.int32, (1, tp), 1) < n_valid
        first_lane = lax.broadcasted_iota(jnp.int32, (1, SUB_PER_STEP), 1) == 0
        cmp_kv = []
        for new_ref, sub, wt_ref in ((kn_ref, sub_k, wkt_ref), (vn_ref, sub_v, wvt_ref)):
            new_t = None
            for part in _split3(new_ref[0]):
                d = _dot_nt(ident_ref[...], part)
                new_t = d if new_t is None else new_t + d
            new_sum = jnp.sum(jnp.where(new_valid, new_t, 0.0), axis=1, keepdims=True)
            sub[steps] = jnp.where(first_lane, new_sum, 0.0)
            sums = jnp.concatenate([sub[s] for s in range(steps + 1)], axis=1)
            mean_t = (sums[:, 0:n_sub] + sums[:, 1:n_sub + 1]) * (1.0 / CMP_BLOCK)
            cmp_kv.append(_dot(wt_ref[...], mean_t.astype(BF16)).astype(BF16))
        kcmp_t, vcmp_t = cmp_kv

        rows = HEADS_PER_GROUP * tp
        qpos_col = past_len + lax.broadcasted_iota(jnp.int32, (rows, 1), 0) % tp
        cmp_end_row = lax.broadcasted_iota(jnp.int32, (1, n_sub), 1) * CMP_STRIDE + (CMP_BLOCK - 1)
        cmp_mask = cmp_end_row <= qpos_col

        n_lane = sel_ref.shape[2]
        n_sel = -(-(past_len + n_valid) // SEL_BLOCK)
        lane = lax.broadcasted_iota(jnp.int32, (1, n_lane), 1)
        sel_j = lane // SEL_PER_CMP
        lane_used = (lane % SEL_PER_CMP == 0) & (sel_j < n_sel)
        qpos_t = past_len + t_col
        cur = qpos_t // SEL_BLOCK
        valid = lane_used & (sel_j * SEL_BLOCK <= qpos_t)
        forced = lane_used & ((sel_j == 0) | (sel_j == cur) | (sel_j == cur - 1))
        lane_pad = jnp.zeros((tp, n_lane - n_sub), F32)

        q = q_ref[0]
        o_heads = [None] * NSA_HEADS
        scores = []
        for g in range(NSA_KV_HEADS):
            gcols = slice(g * HEAD_DIM, (g + 1) * HEAD_DIM)
            prob = _masked_softmax_rows(_dot(_stack_heads(q, g), kcmp_t[gcols, :]), cmp_mask)
            o_g = _dot_nt(prob.astype(BF16), vcmp_t[gcols, :])
            psum = prob[0:tp]
            for hh in range(1, HEADS_PER_GROUP):
                psum = psum + prob[hh * tp:(hh + 1) * tp]
                o_heads[g * HEADS_PER_GROUP + hh] = o_g[hh * tp:(hh + 1) * tp]
            o_heads[g * HEADS_PER_GROUP] = o_g[0:tp]
            pz = jnp.concatenate([psum, lane_pad], axis=1)
            imp = pltpu.roll(pz, 1, 1) + pz
            for d in range(1, SEL_PER_CMP):
                imp = imp + pltpu.roll(pz, n_lane - d, 1)
            scores.append(jnp.where(forced, FORCE_SCORE, jnp.where(valid, imp, NEG_INF)))
        score = jnp.concatenate(scores, axis=0)
        chosen = jnp.zeros(score.shape, F32)
        for _ in range(SEL_TOPK):
            best = jnp.max(score, axis=-1, keepdims=True)
            first = jnp.min(jnp.where(score == best, lane, n_lane), axis=-1, keepdims=True)
            hit = lane == first
            chosen = jnp.where(hit, 1.0, chosen)
            score = jnp.where(hit, REMOVED_SCORE, score)
        sel_ref[0] = chosen
        ocmp_ref[0] = jnp.concatenate(o_heads, axis=-1)


def _fetch_pages(pt_ref, k_hbm, v_hbm, kbuf, vbuf, sems, layer, n_layers):
    b, c = pl.program_id(0), pl.program_id(1)
    n_b, steps = pl.num_programs(0), pl.num_programs(1)
    step = b * steps + c
    slot = step % 2

    def copies(bb, cc, to_slot, for_wait=False):
        out = []
        for k in range(PAGES_PER_STEP):
            page = 0 if for_wait else pt_ref[bb, cc * PAGES_PER_STEP + k] * n_layers + layer
            out.append(pltpu.make_async_copy(k_hbm.at[page], kbuf.at[to_slot, k], sems.at[to_slot, 0]))
            out.append(pltpu.make_async_copy(v_hbm.at[page], vbuf.at[to_slot, k], sems.at[to_slot, 1]))
        return out

    @pl.when(step == 0)
    def _():
        for cp in copies(0, 0, 0):
            cp.start()

    @pl.when(step + 1 < n_b * steps)
    def _():
        wrap = c + 1 == steps
        for cp in copies(jnp.where(wrap, b + 1, b), jnp.where(wrap, 0, c + 1), 1 - slot):
            cp.start()

    for cp in copies(b, c, slot, for_wait=True):
        cp.wait()
    k_all = jnp.concatenate([kbuf[slot, k] for k in range(PAGES_PER_STEP)], axis=1)
    v_all = jnp.concatenate([vbuf[slot, k] for k in range(PAGES_PER_STEP)], axis=1)
    return k_all, v_all


_PAGE_SCRATCH = [pltpu.VMEM((2, PAGES_PER_STEP, KV_COLS, PAGE_SIZE), F32),
                 pltpu.VMEM((2, PAGES_PER_STEP, KV_COLS, PAGE_SIZE), F32),
                 pltpu.SemaphoreType.DMA((2, 2))]


def sample_cmp(page_table, cache_k, cache_v, kc_new, vc_new, q, wk_bd_t, wv_bd_t, subpool, ident,
               layer, n_layers, n_valid):
    b, n_pages = page_table.shape
    past_len = n_pages * PAGE_SIZE
    steps = n_pages // PAGES_PER_STEP
    assert SUB_PER_STEP == LANE
    n_lane = (steps + 1) * LANE
    per_b = lambda a: pl.BlockSpec((1,) + a.shape[1:], lambda i, c, pt: (i, 0, 0))
    const = lambda a: pl.BlockSpec(a.shape, lambda i, c, pt: (0,) * a.ndim)
    consts = [wk_bd_t, wv_bd_t, subpool, ident]
    hbm = pl.BlockSpec(memory_space=pl.ANY)
    grid_spec = pltpu.PrefetchScalarGridSpec(
        num_scalar_prefetch=1,
        grid=(b, steps),
        in_specs=[hbm, hbm, per_b(kc_new), per_b(vc_new), per_b(q)] + [const(a) for a in consts],
        out_specs=[pl.BlockSpec((1, SAMPLE_T_PAD, NSA_WIDTH), lambda i, c, pt: (i, 0, 0)),
                   pl.BlockSpec((1, 2 * SAMPLE_T_PAD, n_lane), lambda i, c, pt: (i, 0, 0))],
        scratch_shapes=[pltpu.VMEM((steps + 1, KV_COLS, SUB_PER_STEP), F32)] * 2 + _PAGE_SCRATCH,
    )
    return pl.pallas_call(
        functools.partial(_sample_cmp_kernel, n_valid=n_valid, past_len=past_len, layer=layer,
                          n_layers=n_layers),
        grid_spec=grid_spec,
        out_shape=[jax.ShapeDtypeStruct((b, SAMPLE_T_PAD, NSA_WIDTH), F32),
                   jax.ShapeDtypeStruct((b, 2 * SAMPLE_T_PAD, n_lane), F32)],
        compiler_params=_cparams("arbitrary", "arbitrary"),
    )(page_table, cache_k, cache_v, kc_new, vc_new, q, *consts)


def _sample_slc_kernel(pt_ref, k_hbm, v_hbm, q_ref, sel_ref, expand_ref, ksn_ref, vsn_ref, wk_ref, wv_ref,
                       kwn_ref, vwn_ref, ocmp_ref, gate_ref, o_ref, m_s, l_s, acc_s, kbuf, vbuf, sems,
                       *, n_valid, past_len, layer, n_layers):
    tp = SAMPLE_T_PAD
    rows = HEADS_PER_GROUP * tp
    c = pl.program_id(1)
    q = q_ref[0]
    zero = jnp.zeros((rows, HEAD_DIM), BF16)
    q_bd = jnp.concatenate([jnp.concatenate([_stack_heads(q, 0), zero], axis=1),
                            jnp.concatenate([zero, _stack_heads(q, 1)], axis=1)], axis=0)
    t_col = lax.broadcasted_iota(jnp.int32, (NSA_KV_HEADS * rows, 1), 0) % tp
    new_row = lax.broadcasted_iota(jnp.int32, (1, tp), 1)
    new_mask = (new_row <= t_col) & (new_row < n_valid)

    @pl.when(c == 0)
    def _():
        m_s[...] = jnp.full(m_s.shape, NEG_INF, F32)
        l_s[...] = jnp.zeros(l_s.shape, F32)
        acc_s[...] = jnp.zeros(acc_s.shape, F32)

    k_all, v_all = _fetch_pages(pt_ref, k_hbm, v_hbm, kbuf, vbuf, sems, layer, n_layers)
    k_all_t, v_all_t = k_all.astype(BF16), v_all.astype(BF16)
    chosen = _dot(sel_ref[0].astype(BF16), expand_ref[...])
    keymask = jnp.concatenate([chosen[0:tp]] * HEADS_PER_GROUP + [chosen[tp:2 * tp]] * HEADS_PER_GROUP,
                              axis=0) > 0.5
    s = jnp.where(keymask, _dot(q_bd, k_all_t), NEG_INF)
    m_old = m_s[...]
    m_new = jnp.maximum(m_old, jnp.max(s, axis=-1, keepdims=True))
    alpha = jnp.exp(m_old - m_new)
    e = jnp.where(keymask, jnp.exp(s - m_new), 0.0)
    l_s[...] = alpha * l_s[...] + jnp.sum(e, axis=-1, keepdims=True)
    acc_s[...] = alpha * acc_s[...] + _dot_nt(e.astype(BF16), v_all_t)
    m_s[...] = m_new

    @pl.when(c == pl.num_programs(1) - 1)
    def _():
        n_win = wk_ref.shape[2]
        win_row = lax.broadcasted_iota(jnp.int32, (1, n_win), 1)
        dist = t_col + n_win - win_row
        win_mask = (dist >= 0) & (dist < WINDOW)
        s_n = jnp.where(new_mask, _dot_nt(q_bd, ksn_ref[0].astype(BF16)), NEG_INF)
        m_fin = jnp.maximum(m_new, jnp.max(s_n, axis=-1, keepdims=True))
        a_fin = jnp.exp(m_new - m_fin)
        e_n = jnp.where(new_mask, jnp.exp(s_n - m_fin), 0.0)
        l_fin = a_fin * l_s[...] + jnp.sum(e_n, axis=-1, keepdims=True)
        o_slc = (a_fin * acc_s[...] + _dot(e_n.astype(BF16), vsn_ref[0].astype(BF16))) / l_fin
        s_p = jnp.where(win_mask, _dot(q_bd, wk_ref[0].astype(BF16)), NEG_INF)
        s_w = jnp.where(new_mask, _dot_nt(q_bd, kwn_ref[0].astype(BF16)), NEG_INF)
        m_w = jnp.maximum(jnp.max(s_p, axis=-1, keepdims=True), jnp.max(s_w, axis=-1, keepdims=True))
        e_p = jnp.where(win_mask, jnp.exp(s_p - m_w), 0.0)
        e_w = jnp.where(new_mask, jnp.exp(s_w - m_w), 0.0)
        l_w = jnp.sum(e_p, axis=-1, keepdims=True) + jnp.sum(e_w, axis=-1, keepdims=True)
        o_win = (_dot_nt(e_p.astype(BF16), wv_ref[0].astype(BF16))
                 + _dot(e_w.astype(BF16), vwn_ref[0].astype(BF16))) / l_w
        gate_all = gate_ref[0]
        head_out = []
        for h in range(NSA_HEADS):
            g, hh = divmod(h, HEADS_PER_GROUP)
            r = slice(g * rows + hh * tp, g * rows + (hh + 1) * tp)
            cols = slice(g * HEAD_DIM, (g + 1) * HEAD_DIM)
            gate = gate_all[:, 3 * h:3 * h + 3]
            head_out.append(gate[:, 0:1] * ocmp_ref[0, :, h * HEAD_DIM:(h + 1) * HEAD_DIM]
                            + gate[:, 1:2] * o_slc[r, cols] + gate[:, 2:3] * o_win[r, cols])
        o_ref[0] = jnp.concatenate(head_out, axis=-1).astype(o_ref.dtype)


def sample_slc(page_table, cache_k, cache_v, q, sel, expand, ks_new, vs_new, win_k, win_v, kw_new, vw_new,
               o_cmp, gate, layer, n_layers, n_valid):
    b, n_pages = page_table.shape
    past_len = n_pages * PAGE_SIZE
    steps = n_pages // PAGES_PER_STEP
    rows = HEADS_PER_GROUP * SAMPLE_T_PAD
    per_b = lambda a: pl.BlockSpec((1,) + a.shape[1:], lambda i, c, pt: (i, 0, 0))
    const = lambda a: pl.BlockSpec(a.shape, lambda i, c, pt: (0,) * a.ndim)
    hbm = pl.BlockSpec(memory_space=pl.ANY)
    grid_spec = pltpu.PrefetchScalarGridSpec(
        num_scalar_prefetch=1,
        grid=(b, steps),
        in_specs=[hbm, hbm, per_b(q), pl.BlockSpec((1, 2 * SAMPLE_T_PAD, LANE), lambda i, c, pt: (i, 0, c)),
                  const(expand), per_b(ks_new), per_b(vs_new), per_b(win_k), per_b(win_v),
                  per_b(kw_new), per_b(vw_new), per_b(o_cmp), per_b(gate)],
        out_specs=pl.BlockSpec((1, SAMPLE_T_PAD, NSA_WIDTH), lambda i, c, pt: (i, 0, 0)),
        scratch_shapes=[pltpu.VMEM((NSA_KV_HEADS * rows, 1), F32), pltpu.VMEM((NSA_KV_HEADS * rows, 1), F32),
                        pltpu.VMEM((NSA_KV_HEADS * rows, KV_COLS), F32)] + _PAGE_SCRATCH,
    )
    return pl.pallas_call(
        functools.partial(_sample_slc_kernel, n_valid=n_valid, past_len=past_len, layer=layer,
                          n_layers=n_layers),
        grid_spec=grid_spec,
        out_shape=jax.ShapeDtypeStruct((b, SAMPLE_T_PAD, NSA_WIDTH), BF16),
        compiler_params=_cparams("arbitrary", "arbitrary"),
    )(page_table, cache_k, cache_v, q, sel, expand, ks_new, vs_new, win_k, win_v, kw_new, vw_new, o_cmp, gate)


def _lru_kernel(*refs, tt, first_pos_zero, fused_in_proj):
    n_lead = 3 if fused_in_proj else 2
    (cpast_ref, h0_ref, cw_ref, cb_ref, wa_ref, wx_ref, ba_ref, bx_ref, lam_ref,
     o_ref, tail_ref, xtail_ref, xbuf, h_carry, a_s, d_s) = refs[n_lead:]
    j = pl.program_id(1)

    @pl.when(j == 0)
    def _():
        xbuf[...] = cpast_ref[0]
        h_carry[...] = h0_ref[0]

    sub = lax.broadcasted_iota(jnp.int32, (8, LRU_WIDTH), 0)
    if fused_in_proj:
        x_ref, g_ref, win_ref = refs[:n_lead]
        h_in = _rms(x_ref[0], g_ref[...]).astype(BF16)
        x_cur = _dot(h_in, win_ref[:, 0:LRU_WIDTH])
        gy = jax.nn.gelu(_dot(h_in, win_ref[:, LRU_WIDTH:]))
    else:
        x_cur = refs[0][0]
        gy = refs[1][0]
    prev8 = xbuf[...]

    def delayed(k):
        if k == 0:
            return x_cur
        rolled = pltpu.roll(x_cur, k, 0)
        head = jnp.where(sub < k, pltpu.roll(prev8, k, 0), rolled[0:8])
        return jnp.concatenate([head, rolled[8:]], axis=0) if tt > 8 else head

    xc = delayed(CONV_WIDTH - 1) * cw_ref[0:1, :]
    for k in range(1, CONV_WIDTH):
        xc = xc + delayed(CONV_WIDTH - 1 - k) * cw_ref[k:k + 1, :]
    xf = xc + cb_ref[...]
    xf_b = xf.astype(BF16)
    r_parts, i_parts = [], []
    for n in range(LRU_BLOCKS):
        cols = slice(n * LRU_BLOCK, (n + 1) * LRU_BLOCK)
        r_parts.append(_dot(xf_b[:, cols], wa_ref[n]))
        i_parts.append(_dot(xf_b[:, cols], wx_ref[n]))
    sigmoid = lambda v: 0.5 * jnp.tanh(0.5 * v) + 0.5
    r = sigmoid(jnp.concatenate(r_parts, axis=-1) + ba_ref[...])
    gate_i = sigmoid(jnp.concatenate(i_parts, axis=-1) + bx_ref[...])
    neg_lam = -lam_ref[...]
    softplus = jnp.maximum(neg_lam, 0.0) + jnp.log1p(jnp.exp(-jnp.abs(neg_lam)))
    log_a = r * (-LRU_C * softplus)
    a = jnp.exp(log_a)
    gap = -jnp.tanh(log_a) * (a * a + 1.0)
    mult = jnp.where(gap > 0.0, gap * lax.rsqrt(gap), 0.0)
    if first_pos_zero:
        row = lax.broadcasted_iota(jnp.int32, (tt, 1), 0)
        mult = jnp.where((row == 0) & (j == 0), 1.0, mult)
    a_s[...] = a
    d_s[...] = mult * gate_i * xf

    def body(blk, h):
        rows = pl.ds(pl.multiple_of(blk * 8, 8), 8)
        a8 = a_s[rows, :]
        d8 = d_s[rows, :]
        for s in (1, 2, 4):
            keep = sub >= s
            d8 = jnp.where(keep, a8 * pltpu.roll(d8, s, 0) + d8, d8)
            a8 = jnp.where(keep, a8 * pltpu.roll(a8, s, 0), a8)
        hs = a8 * h + d8
        d_s[rows, :] = hs
        return hs[7:8, :]

    n_blk = tt // 8
    h_carry[...] = lax.fori_loop(0, n_blk, body, h_carry[...], unroll=min(4, n_blk))
    hs = d_s[...]
    o_ref[0] = (hs * gy).astype(o_ref.dtype)
    tail_ref[0] = d_s[tt - 8:tt, :]
    xtail_ref[0] = x_cur[tt - 8:tt, :]
    xbuf[...] = x_cur[tt - 8:tt, :]


def lru(lead, conv_past, h0, cw, cb, wa, wx, ba, bx, lam, tt, first_pos_zero, fused_in_proj):
    b, t, _ = lead[0].shape
    w = LRU_WIDTH
    tile = lambda a: pl.BlockSpec((1, tt, a.shape[2]), lambda i, j: (i, j, 0))
    per_b = lambda a: pl.BlockSpec((1,) + a.shape[1:], lambda i, j: (i, 0, 0))
    last8 = pl.BlockSpec((1, 8, w), lambda i, j: (i, 0, 0))
    consts = [cw, cb, wa, wx, ba, bx, lam]
    lead_specs = ([tile(lead[0]), _full(lead[1].shape), _full(lead[2].shape)] if fused_in_proj
                  else [tile(lead[0]), tile(lead[1])])
    return pl.pallas_call(
        functools.partial(_lru_kernel, tt=tt, first_pos_zero=first_pos_zero, fused_in_proj=fused_in_proj),
        grid=(b, t // tt),
        in_specs=lead_specs + [per_b(conv_past), per_b(h0)] + [_full(a.shape) for a in consts],
        out_specs=[pl.BlockSpec((1, tt, w), lambda i, j: (i, j, 0)), last8, last8],
        out_shape=[jax.ShapeDtypeStruct((b, t, w), BF16), jax.ShapeDtypeStruct((b, 8, w), F32),
                   jax.ShapeDtypeStruct((b, 8, w), F32)],
        scratch_shapes=[pltpu.VMEM((8, w), F32), pltpu.VMEM((1, w), F32),
                        pltpu.VMEM((tt, w), F32), pltpu.VMEM((tt, w), F32)],
        compiler_params=_cparams("parallel", "arbitrary"),
    )(*lead, conv_past, h0, *consts)


def _block_mean_matrix(t):
    n_cmp = t // CMP_STRIDE - CMP_BLOCK // CMP_STRIDE + 1
    m = np.zeros((t // CMP_STRIDE, t), np.float32)
    for i in range(n_cmp):
        m[i, i * CMP_STRIDE:i * CMP_STRIDE + CMP_BLOCK] = 1.0 / CMP_BLOCK
    return m


def _sel_from_cmp(n_cmp_rows, n_cmp, n_sel):
    m = np.zeros((n_cmp_rows, n_sel), np.float32)
    for j in range(n_sel):
        for c in range(4 * j - 1, 4 * j + 4):
            if 0 <= c < n_cmp:
                m[c, j] = 1.0
    return m


def _block_diag(w):
    z = jnp.zeros((HEAD_DIM, HEAD_DIM), w.dtype)
    return jnp.concatenate([jnp.concatenate([w[0], z], axis=1), jnp.concatenate([z, w[1]], axis=1)], axis=0)


def _sigmoid(v):
    return jax.nn.sigmoid(v)


def _scale_q(v):
    return v * (HEAD_DIM ** -0.5)


_EVEN_GROUPS = (
    (0, POOL_WIDTH, ((0, POOL_WIDTH, None),)),
    (POOL_WIDTH, NSA_WIDTH, ((0, NSA_WIDTH, _scale_q),)),
    (POOL_WIDTH + NSA_WIDTH, 6 * KV_COLS, tuple((k * KV_COLS, KV_COLS, None) for k in range(6))),
    (POOL_WIDTH + NSA_WIDTH + 6 * KV_COLS, LANE, ((0, LANE, _sigmoid),)),
)
_EVEN_DTYPES = (F32, BF16) + (F32,) * 6 + (F32,)


def _even_in_proj(x2d, g, w_in_pad, tm):
    return norm_matmul(x2d, g, w_in_pad, _EVEN_GROUPS, _EVEN_DTYPES, tm)


def _xattn_block(x2d, b, t, mk, mv, layer, g_pre, g_post, wq, wo, tm, tq):
    q, = norm_matmul(x2d, g_pre, wq, ((0, D_MODEL, ((0, D_MODEL, lambda v: v * (XATTN_HEAD_DIM ** -0.5)),)),),
                     (BF16,), tm)
    o = xattn(q.reshape(b, t, D_MODEL), mk, mv, layer, tq).reshape(b * t, D_MODEL)
    return proj_norm_res([o], [wo], g_post, x2d, tm)


def kernel(x_prompt, mem_prompt, x_sample, state_pool, cache_cmp_k, cache_cmp_v, cache_slc_k, cache_slc_v, cache_win_k, cache_win_v, state_lru_h, state_lru_conv, cache_mem_k, cache_mem_v, page_table, norm_gain, mem_norm_gain, w_in_even, pool_w, pool_scale, w_cmp_k, w_cmp_v, w_out_even, w_in_odd, conv_w, conv_b, lru_wa, lru_ba, lru_wx, lru_bx, lru_lambda, w_out_odd, w_xq, w_xk, w_xv, w_xo, w_ffn_gate, w_ffn_up, w_ffn_down):
    bp, tp, d = x_prompt.shape
    bs, ts, _ = x_sample.shape
    depth = norm_gain.shape[0]
    n_even = w_in_even.shape[0]
    n_pages = page_table.shape[1]
    past_len = n_pages * PAGE_SIZE
    n_phys = cache_cmp_k.shape[0]
    tsp = SAMPLE_T_PAD
    np_tok, ns_tok = bp * tp, bs * tsp
    tm_p, tm_s = 512, ns_tok
    tq, tk = 256, 128

    xp = x_prompt.reshape(np_tok, d)
    xs = jnp.pad(x_sample, ((0, 0), (0, tsp - ts), (0, 0))).reshape(ns_tok, d)

    n_sub_p = tp // CMP_STRIDE
    n_cmp_p = n_sub_p - 1
    n_sel_p = tp // SEL_BLOCK
    pool_mat = jnp.asarray(_block_mean_matrix(tp).T, BF16)
    mselt = jnp.asarray(_sel_from_cmp(n_sub_p, n_cmp_p, n_sel_p).T, BF16)
    ident = jnp.asarray(np.eye(tq, dtype=np.float32), BF16)
    ident_kv = jnp.asarray(np.eye(KV_COLS, dtype=np.float32), BF16)
    subpool = jnp.asarray(np.repeat(np.eye(SUB_PER_STEP, dtype=np.float32), CMP_STRIDE, axis=0), BF16)
    expand_np = np.zeros((LANE, PAGES_PER_STEP * PAGE_SIZE), np.float32)
    expand_np[::SEL_PER_CMP] = np.repeat(np.eye(SEL_PER_STEP, dtype=np.float32), SEL_BLOCK, axis=1)
    expand_s = jnp.asarray(expand_np, BF16)

    row = lambda v: v.reshape(1, -1)
    out_ev_p, out_ev_s, out_lru_p, out_lru_s, mem_k_p, mem_v_p = [], [], [], [], [], []

    for li in range(depth):
        gn = norm_gain[li]
        if li % 2 == 0:
            e = li // 2
            w_in = jnp.pad(w_in_even[e], ((0, 0), (0, EVEN_IN_PAD - EVEN_IN))).astype(BF16)
            pw = pool_w[e].astype(BF16)
            ps = row(pool_scale[e])
            wk_bd_t = _block_diag(w_cmp_k[e]).T.astype(BF16)
            wv_bd_t = _block_diag(w_cmp_v[e]).T.astype(BF16)
            w_out = w_out_even[e].astype(BF16)
            kv0 = POOL_WIDTH + NSA_WIDTH
            kcol = lambda k: slice(kv0 + k * KV_COLS, kv0 + (k + 1) * KV_COLS)
            w_u = w_in[:, :POOL_WIDTH]
            w_kpm = jnp.concatenate([w_in[:, kcol(2)], w_in[:, kcol(4)]], axis=1)
            w_t = w_in[:, POOL_WIDTH:kv0 + 7 * KV_COLS].T

            u, kpm, q_t, gate_t, kc, vc, ks, vs, kw, vw = even_in_prompt(xp, row(gn[0]), w_u, w_kpm, w_t,
                                                                         bp, tp, tm_p)
            u3 = u.reshape(bp, tp, POOL_WIDTH)
            a_out = pool_mix(u3, jnp.zeros((bp, POOL_HALO, POOL_WIDTH), F32), pw, ps, 0)
            kcmp, vcmp_t = compress_kv(kc, vc, pool_mat, wk_bd_t, wv_bd_t, ident_kv)
            o_nsa = nsa_prompt(q_t, gate_t, kcmp, vcmp_t, kpm.reshape(bp, tp, 2 * KV_COLS), vs, vw,
                               mselt, ident, tq, tk)
            mix_p = ([a_out, o_nsa], [w_out[:POOL_WIDTH], w_out[POOL_WIDTH:]])
            kv5 = lambda a: a.reshape(bp, NSA_KV_HEADS, HEAD_DIM, a.shape[-1]).transpose(0, 3, 1, 2)
            n_keep = min(WINDOW, tp)
            out_ev_p.append((u3[:, -POOL_STATE:], kv5(kc), kv5(vc), kv5(ks), kv5(vs),
                             kv5(kw[:, :, -n_keep:]), kv5(vw[:, :, -n_keep:])))

            u, q, kc, vc, ks, vs, kw, vw, gate = _even_in_proj(xs, row(gn[0]), w_in, tm_s)
            as3 = lambda a: a.reshape(bs, tsp, a.shape[-1])
            u3 = as3(u)
            past_pool = jnp.pad(state_pool[e], ((0, 0), (POOL_HALO - POOL_STATE, 0), (0, 0)))
            a_out = pool_mix(u3, past_pool, pw, ps, past_len)
            paged = lambda c: c.transpose(0, 1, 3, 4, 2).reshape(n_phys * n_even, KV_COLS, PAGE_SIZE)
            o_cmp, sel = sample_cmp(page_table, paged(cache_cmp_k), paged(cache_cmp_v), as3(kc), as3(vc),
                                    as3(q), wk_bd_t, wv_bd_t, subpool, ident_kv, e, n_even, ts)
            n_win = cache_win_k.shape[2]
            win_k = cache_win_k[e].transpose(0, 2, 3, 1).reshape(bs, KV_COLS, n_win)
            win_v = cache_win_v[e].transpose(0, 2, 3, 1).reshape(bs, KV_COLS, n_win)
            o_nsa = sample_slc(page_table, paged(cache_slc_k), paged(cache_slc_v), as3(q), sel, expand_s,
                               as3(ks), as3(vs), win_k, win_v, as3(kw), as3(vw), o_cmp, as3(gate),
                               e, n_even, ts)
            xs = proj_norm_res([a_out.reshape(ns_tok, POOL_WIDTH), o_nsa.reshape(ns_tok, NSA_WIDTH)],
                               [w_out[:POOL_WIDTH], w_out[POOL_WIDTH:]], row(gn[1]), xs, tm_s)
            new4 = lambda a: as3(a)[:, :ts].reshape(bs, ts, NSA_KV_HEADS, HEAD_DIM)
            out_ev_s.append((jnp.concatenate([state_pool[e], u3[:, :ts]], axis=1)[:, -POOL_STATE:],
                             new4(kc), new4(vc), new4(ks), new4(vs),
                             jnp.concatenate([cache_win_k[e], new4(kw)], axis=1)[:, -n_win:],
                             jnp.concatenate([cache_win_v[e], new4(vw)], axis=1)[:, -n_win:]))
        else:
            o = li // 2
            w_in = w_in_odd[o].astype(BF16)
            cw = jnp.pad(conv_w[o], ((0, 8 - CONV_WIDTH), (0, 0)))
            consts = (cw, row(conv_b[o]), lru_wa[o].astype(BF16), lru_wx[o].astype(BF16),
                      row(lru_ba[o]), row(lru_bx[o]), row(lru_lambda[o]))
            w_out = w_out_odd[o].astype(BF16)
            groups = ((0, LRU_WIDTH, ((0, LRU_WIDTH, None),)),
                      (LRU_WIDTH, LRU_WIDTH, ((0, LRU_WIDTH, jax.nn.gelu),)))

            gated, tail, xb_tail = lru((xp.reshape(bp, tp, d), row(gn[0]), w_in),
                                       jnp.zeros((bp, 8, LRU_WIDTH), F32), jnp.zeros((bp, 1, LRU_WIDTH), F32),
                                       *consts, tt=256, first_pos_zero=True, fused_in_proj=True)
            mix_p = ([gated], [w_out])
            out_lru_p.append((xb_tail[:, -(CONV_WIDTH - 1):], tail[:, 7]))

            xb, yb = norm_matmul(xs, row(gn[0]), w_in, groups, (F32, F32), tm_s)
            xb3, yb3 = xb.reshape(bs, tsp, LRU_WIDTH), yb.reshape(bs, tsp, LRU_WIDTH)
            conv_past = jnp.pad(state_lru_conv[o], ((0, 0), (8 - (CONV_WIDTH - 1), 0), (0, 0)))
            gated, tail, _ = lru((xb3, yb3), conv_past, state_lru_h[o][:, None, :], *consts, tt=tsp,
                                 first_pos_zero=False, fused_in_proj=False)
            xs = proj_norm_res([gated.reshape(ns_tok, LRU_WIDTH)], [w_out], row(gn[1]), xs, tm_s)
            conv_new = jnp.concatenate([state_lru_conv[o], xb3[:, :ts]], axis=1)[:, -(CONV_WIDTH - 1):]
            out_lru_s.append((conv_new, tail[:, ts - 1]))

        wq, wo = w_xq[li].astype(BF16), w_xo[li].astype(BF16)
        wkv = jnp.concatenate([w_xk[li], w_xv[li]], axis=1).astype(BF16)
        mem2d = mem_prompt.reshape(bp * N_MEM, d)
        kv_groups = ((0, D_MODEL, ((0, D_MODEL, None),)), (D_MODEL, D_MODEL, ((0, D_MODEL, None),)))
        mk, mv = norm_matmul(mem2d, row(mem_norm_gain[li]), wkv, kv_groups, (F32, F32), min(tm_p, bp * N_MEM))
        mk3, mv3 = mk.reshape(bp, N_MEM, d), mv.reshape(bp, N_MEM, d)
        mem_k_p.append(mk3.reshape(bp, N_MEM, XATTN_HEADS, XATTN_HEAD_DIM))
        mem_v_p.append(mv3.reshape(bp, N_MEM, XATTN_HEADS, XATTN_HEAD_DIM))
        wg, wu, wd = w_ffn_gate[li].astype(BF16), w_ffn_up[li].astype(BF16), w_ffn_down[li].astype(BF16)
        xp = post_mixer(mix_p[0], mix_p[1], xp.reshape(bp, tp, d), gn, mk3, mv3, wq, wo, wg, wu, wd,
                        tm_p).reshape(np_tok, d)
        xs = _xattn_block(xs, bs, tsp, cache_mem_k, cache_mem_v, li, row(gn[2]), row(gn[3]), wq, wo, tm_s, tsp)
        xs = ffn(xs, row(gn[4]), row(gn[5]), wg, wu, wd, tm_s)

    stack = lambda items, k, axis=0: jnp.stack([s[k] for s in items], axis=axis)
    y_prompt = xp.reshape(bp, tp, d)
    y_sample = xs.reshape(bs, tsp, d)[:, :ts]
    return (y_prompt, y_sample,
            stack(out_ev_p, 0), stack(out_ev_s, 0),
            stack(out_ev_p, 1, 1), stack(out_ev_s, 1, 1),
            stack(out_ev_p, 2, 1), stack(out_ev_s, 2, 1),
            stack(out_ev_p, 3, 1), stack(out_ev_s, 3, 1),
            stack(out_ev_p, 4, 1), stack(out_ev_s, 4, 1),
            stack(out_ev_p, 5), stack(out_ev_s, 5),
            stack(out_ev_p, 6), stack(out_ev_s, 6),
            stack(out_lru_p, 1), stack(out_lru_s, 1),
            stack(out_lru_p, 0), stack(out_lru_s, 0),
            jnp.stack(mem_k_p), jnp.stack(mem_v_p))
```

```python
import functools

import numpy as np
import jax
import jax.numpy as jnp
from jax import lax
from jax.experimental import pallas as pl
from jax.experimental.pallas import tpu as pltpu

F32 = jnp.float32
BF16 = jnp.bfloat16

D_MODEL = 1024
RMS_EPS = 1e-6
NEG_INF = -1e30
FORCE_SCORE = 1e30
REMOVED_SCORE = -3e38
LOG2E = 1.4426950408889634
DENOM_ROWS = 8
M_INIT = -1e29

POOL_WIDTH = 512
POOL_WINDOWS = (2, 4, 8, 16)
POOL_GROUP = 128
POOL_STATE = 15
POOL_HALO = 16

HEAD_DIM = 64
NSA_HEADS = 8
NSA_KV_HEADS = 2
HEADS_PER_GROUP = NSA_HEADS // NSA_KV_HEADS
NSA_WIDTH = 512
KV_COLS = 128
CMP_BLOCK = 32
CMP_STRIDE = 16
SEL_BLOCK = 64
SEL_TOPK = 16
WINDOW = 512
PAGE_SIZE = 128
EVEN_IN = POOL_WIDTH + NSA_WIDTH + 6 * KV_COLS + 3 * NSA_HEADS
EVEN_IN_PAD = 1920

LRU_WIDTH = 1024
LRU_BLOCKS = 4
LRU_BLOCK = 256
CONV_WIDTH = 4
LRU_C = 8.0

N_MEM = 256
XATTN_HEADS = 4
XATTN_HEAD_DIM = 256
D_FF = 2816
FF_CHUNK = 256

SAMPLE_T_PAD = 8
PAGES_PER_STEP = 16
SEL_PER_STEP = PAGES_PER_STEP * PAGE_SIZE // SEL_BLOCK
SUB_PER_STEP = PAGES_PER_STEP * PAGE_SIZE // CMP_STRIDE
SEL_PER_CMP = SEL_BLOCK // CMP_STRIDE
LANE = 128

VMEM_LIMIT_BYTES = 56 * 1024 * 1024


def _cparams(*sem):
    return pltpu.CompilerParams(dimension_semantics=sem, vmem_limit_bytes=VMEM_LIMIT_BYTES)


def _rms(x, g):
    return x * lax.rsqrt(jnp.mean(x * x, axis=-1, keepdims=True) + RMS_EPS) * g


def _dot(a, b):
    return jnp.dot(a, b, preferred_element_type=F32)


def _dot_nt(a, b):
    return lax.dot_general(a, b, (((1,), (1,)), ((), ())), preferred_element_type=F32)


def _split3(x):
    p1 = x.astype(BF16)
    r1 = x - p1.astype(F32)
    p2 = r1.astype(BF16)
    p3 = (r1 - p2.astype(F32)).astype(BF16)
    return p1, p2, p3


def _full(shape):
    n = len(shape)
    return pl.BlockSpec(shape, lambda *_: (0,) * n)


def _norm_matmul_kernel(x_ref, g_ref, w_ref, *o_refs, groups):
    h = _rms(x_ref[...], g_ref[...]).astype(BF16)
    k = 0
    for start, width, outs in groups:
        z = _dot(h, w_ref[:, start:start + width])
        for off, w, post in outs:
            v = z[:, off:off + w]
            if post is not None:
                v = post(v)
            o_refs[k][...] = v.astype(o_refs[k].dtype)
            k += 1


def norm_matmul(x, g, w, groups, out_dtypes, tm):
    n, d = x.shape
    widths = [w_ for _, _, outs in groups for _, w_, _ in outs]
    return pl.pallas_call(
        functools.partial(_norm_matmul_kernel, groups=groups),
        grid=(n // tm,),
        in_specs=[pl.BlockSpec((tm, d), lambda i: (i, 0)), _full(g.shape), _full(w.shape)],
        out_specs=[pl.BlockSpec((tm, wd), lambda i: (i, 0)) for wd in widths],
        out_shape=[jax.ShapeDtypeStruct((n, wd), dt) for wd, dt in zip(widths, out_dtypes)],
        compiler_params=_cparams("parallel"),
    )(x, g, w)


def _memory_kv_kernel(x_ref, g_ref, w_ref, k_ref, v_ref):
    h = _rms(x_ref[...], g_ref[0]).astype(BF16)
    d = k_ref.shape[-1]
    k_ref[0] = _dot(h, w_ref[0, :, 0:d])
    v_ref[0] = _dot(h, w_ref[0, :, d:])


def memory_kv(mem, gains, wkv, tm):
    n, d = mem.shape
    n_layers = wkv.shape[0]
    out = pl.BlockSpec((1, tm, d), lambda l, i: (l, i, 0))
    return pl.pallas_call(
        _memory_kv_kernel,
        grid=(n_layers, n // tm),
        in_specs=[pl.BlockSpec((tm, d), lambda l, i: (i, 0)), pl.BlockSpec((1, 1, d), lambda l, i: (l, 0, 0)),
                  pl.BlockSpec((1, d, 2 * d), lambda l, i: (l, 0, 0))],
        out_specs=[out, out],
        out_shape=[jax.ShapeDtypeStruct((n_layers, n, d), F32)] * 2,
        compiler_params=_cparams("parallel", "parallel"),
    )(mem, gains, wkv)


def _even_in_prompt_kernel(x_ref, g_ref, wu_ref, wk_ref, wt_ref, u_ref, kpm_ref, qt_ref, gatet_ref, *kvt_refs):
    h = _rms(x_ref[...], g_ref[...]).astype(BF16)
    u_ref[...] = _dot(h, wu_ref[...])
    kpm_ref[...] = _dot(h, wk_ref[...]).astype(kpm_ref.dtype)
    z_t = _dot_nt(wt_ref[...], h)
    qt_ref[0] = (z_t[0:NSA_WIDTH] * (HEAD_DIM ** -0.5 * LOG2E)).astype(qt_ref.dtype)
    for k, ref in enumerate(kvt_refs):
        ref[0] = z_t[NSA_WIDTH + k * KV_COLS:NSA_WIDTH + (k + 1) * KV_COLS, :]
    gatet_ref[0] = jax.nn.sigmoid(z_t[NSA_WIDTH + 6 * KV_COLS:, :])


def even_in_prompt(x, g, w_u, w_kpm, w_t, b, t, tm):
    n, d = x.shape
    per_seq = t // tm
    row_tile = lambda w: pl.BlockSpec((tm, w), lambda i: (i, 0))
    feat_tile = lambda f: pl.BlockSpec((1, f, tm), lambda i: (i // per_seq, 0, i % per_seq))
    n_kv = 6
    return pl.pallas_call(
        _even_in_prompt_kernel,
        grid=(n // tm,),
        in_specs=[row_tile(d), _full(g.shape), _full(w_u.shape), _full(w_kpm.shape), _full(w_t.shape)],
        out_specs=([row_tile(POOL_WIDTH), row_tile(2 * KV_COLS), feat_tile(NSA_WIDTH), feat_tile(LANE)]
                   + [feat_tile(KV_COLS)] * n_kv),
        out_shape=([jax.ShapeDtypeStruct((n, POOL_WIDTH), F32), jax.ShapeDtypeStruct((n, 2 * KV_COLS), BF16),
                    jax.ShapeDtypeStruct((b, NSA_WIDTH, t), BF16), jax.ShapeDtypeStruct((b, LANE, t), F32)]
                   + [jax.ShapeDtypeStruct((b, KV_COLS, t), F32)] * n_kv),
        compiler_params=_cparams("parallel"),
    )(x, g, w_u, w_kpm, w_t)


def _proj_norm_res_kernel(*refs, n_in):
    a_refs, w_refs = refs[:n_in], refs[n_in:2 * n_in]
    g_ref, x_ref, o_ref = refs[2 * n_in:]
    acc = _dot(a_refs[0][...], w_refs[0][...])
    for a_ref, w_ref in zip(a_refs[1:], w_refs[1:]):
        acc = acc + _dot(a_ref[...], w_ref[...])
    o_ref[...] = x_ref[...] + _rms(acc, g_ref[...])


def proj_norm_res(a_list, w_list, g, x, tm):
    n, d = x.shape
    n_in = len(a_list)
    return pl.pallas_call(
        functools.partial(_proj_norm_res_kernel, n_in=n_in),
        grid=(n // tm,),
        in_specs=([pl.BlockSpec((tm, a.shape[1]), lambda i: (i, 0)) for a in a_list]
                  + [_full(w.shape) for w in w_list]
                  + [_full(g.shape), pl.BlockSpec((tm, d), lambda i: (i, 0))]),
        out_specs=pl.BlockSpec((tm, d), lambda i: (i, 0)),
        out_shape=jax.ShapeDtypeStruct((n, d), F32),
        compiler_params=_cparams("parallel"),
    )(*a_list, *w_list, g, x)


def _ffn_block(x, g_in, g_out, wg_ref, wu_ref, wd_ref):
    h = _rms(x, g_in).astype(BF16)
    acc = None
    for c in range(D_FF // FF_CHUNK):
        cols = slice(c * FF_CHUNK, (c + 1) * FF_CHUNK)
        gate = _dot(h, wg_ref[:, cols])
        up = _dot(h, wu_ref[:, cols])
        act = (jax.nn.silu(gate) * up).astype(BF16)
        part = _dot(act, wd_ref[cols, :])
        acc = part if acc is None else acc + part
    return x + _rms(acc, g_out)


def _ffn_kernel(x_ref, g_in_ref, g_out_ref, wg_ref, wu_ref, wd_ref, o_ref):
    o_ref[...] = _ffn_block(x_ref[...], g_in_ref[...], g_out_ref[...], wg_ref, wu_ref, wd_ref)


def ffn(x, g_in, g_out, wg, wu, wd, tm):
    n, d = x.shape
    resident = functools.partial(pl.BlockSpec, pipeline_mode=pl.Buffered(1))
    return pl.pallas_call(
        _ffn_kernel,
        grid=(n // tm,),
        in_specs=[pl.BlockSpec((tm, d), lambda i: (i, 0)), _full(g_in.shape), _full(g_out.shape),
                  resident(wg.shape, lambda i: (0, 0)), resident(wu.shape, lambda i: (0, 0)),
                  resident(wd.shape, lambda i: (0, 0))],
        out_specs=pl.BlockSpec((tm, d), lambda i: (i, 0)),
        out_shape=jax.ShapeDtypeStruct((n, d), F32),
        compiler_params=_cparams("parallel"),
    )(x, g_in, g_out, wg, wu, wd)


def _xattn_heads(q, mk_ref, mv_ref):
    head_cols = [slice(hd * XATTN_HEAD_DIM, (hd + 1) * XATTN_HEAD_DIM) for hd in range(XATTN_HEADS)]
    if len(mk_ref.shape) == 4:
        rows = q.shape[0]
        n_kv = N_MEM * XATTN_HEADS
        k_all = mk_ref[0].reshape(n_kv, XATTN_HEAD_DIM).astype(BF16)
        v_all = mv_ref[0].reshape(n_kv, XATTN_HEAD_DIM).astype(BF16)
        q_st = jnp.concatenate([q[:, cols] for cols in head_cols], axis=0)
        own = (lax.broadcasted_iota(jnp.int32, (1, n_kv), 1) % XATTN_HEADS
               == lax.broadcasted_iota(jnp.int32, (XATTN_HEADS * rows, 1), 0) // rows)
        p = _masked_softmax_rows(_dot_nt(q_st, k_all), own)
        o_st = _dot(p.astype(BF16), v_all).astype(BF16)
        return jnp.concatenate([o_st[hd * rows:(hd + 1) * rows] for hd in range(XATTN_HEADS)], axis=-1)
    outs = []
    for cols in head_cols:
        k = mk_ref[0, :, cols].astype(BF16)
        v = mv_ref[0, :, cols].astype(BF16)
        s = _dot_nt(q[:, cols], k)
        e = jnp.exp(s - jnp.max(s, axis=-1, keepdims=True))
        p = e / jnp.sum(e, axis=-1, keepdims=True)
        outs.append(_dot(p.astype(BF16), v).astype(BF16))
    return jnp.concatenate(outs, axis=-1)


def _xattn_kernel(q_ref, mk_ref, mv_ref, o_ref):
    o_ref[0] = _xattn_heads(q_ref[0], mk_ref, mv_ref)


def xattn(q, mk, mv, layer, tq):
    b, t, d = q.shape
    mem_spec = pl.BlockSpec((None, 1) + mk.shape[2:], lambda i, j: (layer, i, 0, 0, 0))
    return pl.pallas_call(
        _xattn_kernel,
        grid=(b, t // tq),
        in_specs=[pl.BlockSpec((1, tq, d), lambda i, j: (i, j, 0)), mem_spec, mem_spec],
        out_specs=pl.BlockSpec((1, tq, d), lambda i, j: (i, j, 0)),
        out_shape=jax.ShapeDtypeStruct((b, t, d), BF16),
        compiler_params=_cparams("parallel", "parallel"),
    )(q, mk, mv)


def _post_mixer_kernel(*refs, n_in):
    a_refs, w_refs = refs[:n_in], refs[n_in:2 * n_in]
    x_ref, gn_ref, mk_ref, mv_ref, wq_ref, wo_ref, wg_ref, wu_ref, wd_ref, o_ref = refs[2 * n_in:]
    gain = lambda k: gn_ref[k:k + 1, :]
    mix = _dot(a_refs[0][0], w_refs[0][...])
    for a_ref, w_ref in zip(a_refs[1:], w_refs[1:]):
        mix = mix + _dot(a_ref[0], w_ref[...])
    x1 = x_ref[0] + _rms(mix, gain(1))
    q = (_dot(_rms(x1, gain(2)).astype(BF16), wq_ref[...]) * (XATTN_HEAD_DIM ** -0.5)).astype(BF16)
    attn = _xattn_heads(q, mk_ref, mv_ref)
    x2 = x1 + _rms(_dot(attn, wo_ref[...]), gain(3))
    o_ref[0] = _ffn_block(x2, gain(4), gain(5), wg_ref, wu_ref, wd_ref)


def post_mixer(a_list, w_list, x, gn, mk, mv, wq, wo, wg, wu, wd, tm):
    b, t, d = x.shape
    n_in = len(a_list)
    resident = lambda a: pl.BlockSpec(a.shape, lambda i, j: (0,) * a.ndim, pipeline_mode=pl.Buffered(1))
    tile = lambda w: pl.BlockSpec((1, tm, w), lambda i, j: (i, j, 0))
    per_b = lambda a: pl.BlockSpec((1,) + a.shape[1:], lambda i, j: (i, 0, 0))
    return pl.pallas_call(
        functools.partial(_post_mixer_kernel, n_in=n_in),
        grid=(b, t // tm),
        in_specs=([tile(a.shape[2]) for a in a_list] + [resident(w) for w in w_list]
                  + [tile(d), resident(gn), per_b(mk), per_b(mv)]
                  + [resident(w) for w in (wq, wo, wg, wu, wd)]),
        out_specs=tile(d),
        out_shape=jax.ShapeDtypeStruct((b, t, d), F32),
        compiler_params=_cparams("parallel", "parallel"),
    )(*a_list, *w_list, x, gn, mk, mv, wq, wo, wg, wu, wd)


def _pool_kernel(u_ref, past_ref, w_ref, scale_ref, o_ref, buf, *, t, chunk, start_pos):
    buf[0:POOL_HALO, :] = past_ref[0]
    buf[POOL_HALO:POOL_HALO + t, :] = u_ref[0]
    for c in range(t // chunk):
        base = POOL_HALO + c * chunk
        pos = start_pos + c * chunk + lax.broadcasted_iota(jnp.int32, (chunk, 1), 0)
        for gi, win in enumerate(POOL_WINDOWS):
            cols = slice(gi * POOL_GROUP, (gi + 1) * POOL_GROUP)
            cur = buf[base:base + chunk, cols]
            win_sum = cur
            for k in range(1, win):
                win_sum = win_sum + buf[base - k:base - k + chunk, cols]
            cnt = jnp.minimum(pos + 1, win).astype(F32)
            diff = win_sum / cnt - cur
            y = _dot(diff.astype(BF16), w_ref[gi]) * scale_ref[:, cols]
            o_ref[0, c * chunk:(c + 1) * chunk, cols] = y.astype(o_ref.dtype)


def pool_mix(u, past, w, scale, start_pos):
    b, t, c = u.shape
    chunk = min(t, 256)
    return pl.pallas_call(
        functools.partial(_pool_kernel, t=t, chunk=chunk, start_pos=start_pos),
        grid=(b,),
        in_specs=[pl.BlockSpec((1, t, c), lambda i: (i, 0, 0)),
                  pl.BlockSpec((1, POOL_HALO, c), lambda i: (i, 0, 0)),
                  _full(w.shape), _full(scale.shape)],
        out_specs=pl.BlockSpec((1, t, c), lambda i: (i, 0, 0)),
        out_shape=jax.ShapeDtypeStruct((b, t, c), BF16),
        scratch_shapes=[pltpu.VMEM((POOL_HALO + t, c), F32)],
        compiler_params=_cparams("parallel"),
    )(u, past, w, scale)


def _compress_kernel(kc_ref, vc_ref, pool_ref, wkt_ref, wvt_ref, ident_ref, k_ref, vt_ref):
    def compress(src, wt_ref):
        x = src[0]
        hi = x.astype(BF16)
        lo = (x - hi.astype(F32)).astype(BF16)
        mean_t = _dot(hi, pool_ref[...]) + _dot(lo, pool_ref[...])
        return _dot(wt_ref[...], mean_t.astype(BF16)).astype(BF16)
    k_ref[0] = _dot_nt(ident_ref[...], compress(kc_ref, wkt_ref)).astype(BF16)
    vt_ref[0] = compress(vc_ref, wvt_ref)


def compress_kv(kc_t, vc_t, pool_mat, wk_bd_t, wv_bd_t, ident):
    b, c, t = kc_t.shape
    n_cmp = pool_mat.shape[1]
    per_b = pl.BlockSpec((1, c, t), lambda i: (i, 0, 0))
    return pl.pallas_call(
        _compress_kernel,
        grid=(b,),
        in_specs=[per_b, per_b, _full(pool_mat.shape), _full(wk_bd_t.shape), _full(wv_bd_t.shape),
                  _full(ident.shape)],
        out_specs=[pl.BlockSpec((1, n_cmp, c), lambda i: (i, 0, 0)), pl.BlockSpec((1, c, n_cmp), lambda i: (i, 0, 0))],
        out_shape=[jax.ShapeDtypeStruct((b, n_cmp, c), BF16), jax.ShapeDtypeStruct((b, c, n_cmp), BF16)],
        compiler_params=_cparams("parallel"),
    )(kc_t, vc_t, pool_mat, wk_bd_t, wv_bd_t, ident)


def _masked_softmax_rows(s, mask):
    s = jnp.where(mask, s, NEG_INF)
    e = jnp.where(mask, jnp.exp(s - jnp.max(s, axis=-1, keepdims=True)), 0.0)
    return e / jnp.maximum(jnp.sum(e, axis=-1, keepdims=True), 1e-30)


def _flash_step(raw_ref, v_aug, mask, m_ref, acc_ref):
    tq = mask.shape[1]
    bias = jnp.where(mask, 0.0, NEG_INF)
    for hh in range(HEADS_PER_GROUP):
        lanes = slice(hh * tq, (hh + 1) * tq)
        s = raw_ref[:, lanes] + bias
        m_old = m_ref[:, lanes]
        m_new = jnp.maximum(m_old, jnp.max(s, axis=0, keepdims=True))
        alpha = jnp.exp2(m_old - m_new)
        e = jnp.exp2(s - m_new)
        m_ref[:, lanes] = m_new
        acc_ref[:, lanes] = alpha * acc_ref[:, lanes] + _dot(v_aug, e.astype(BF16))


def _nsa_prompt_kernel(qt_ref, gatet_ref, kcmp_ref, vcmpt_ref, kpm_ref, vst_ref, vwt_ref,
                       mselt_ref, ident_ref, o_ref, sel_s, raw_s, m_s, acc_s, *, tq, tk):
    i = pl.program_id(1)
    q0 = i * tq
    n_cmp = kcmp_ref.shape[1]
    n_sel = mselt_ref.shape[0]
    blocks_per_tile = tk // SEL_BLOCK
    tiles_per_q = tq // tk
    qpos_row = q0 + lax.broadcasted_iota(jnp.int32, (1, tq), 1)
    key_col = lax.broadcasted_iota(jnp.int32, (tk, 1), 0)
    cmp_end_col = lax.broadcasted_iota(jnp.int32, (n_cmp, 1), 0) * CMP_STRIDE + (CMP_BLOCK - 1)
    cmp_bias = jnp.where(cmp_end_col <= qpos_row, 0.0, NEG_INF)
    sel_j = lax.broadcasted_iota(jnp.int32, (n_sel, 1), 0)
    cur_row = qpos_row // SEL_BLOCK
    gate_t = gatet_ref[0]
    zero_half = jnp.zeros((HEAD_DIM, HEADS_PER_GROUP * tq), BF16)

    groups = range(NSA_KV_HEADS)
    grows = [slice(g * HEAD_DIM, (g + 1) * HEAD_DIM) for g in groups]
    q_pads, o_cmps = [], []
    for g in groups:
        q_t_g = jnp.concatenate(
            [qt_ref[0, (g * HEADS_PER_GROUP + hh) * HEAD_DIM:(g * HEADS_PER_GROUP + hh + 1) * HEAD_DIM, :]
             for hh in range(HEADS_PER_GROUP)], axis=1)
        q_pad = jnp.concatenate([q_t_g, zero_half] if g == 0 else [zero_half, q_t_g], axis=0)
        q_pads.append(q_pad)

        s_t = _dot(kcmp_ref[0], q_pad) + jnp.concatenate([cmp_bias] * HEADS_PER_GROUP, axis=1)
        e_t = jnp.exp2(s_t - jnp.maximum(jnp.max(s_t, axis=0, keepdims=True), M_INIT))
        p_t = e_t / jnp.maximum(jnp.sum(e_t, axis=0, keepdims=True), 1e-30)
        o_cmps.append(_dot(vcmpt_ref[0, grows[g], :], p_t.astype(BF16)))
        psum_t = p_t[:, 0:tq]
        for hh in range(1, HEADS_PER_GROUP):
            psum_t = psum_t + p_t[:, hh * tq:(hh + 1) * tq]

        imp_t = None
        for part in _split3(psum_t):
            d = _dot(mselt_ref[...], part)
            imp_t = d if imp_t is None else imp_t + d
        valid = sel_j * SEL_BLOCK <= qpos_row
        forced = (sel_j == 0) | (sel_j == cur_row) | (sel_j == cur_row - 1)
        score = jnp.where(forced, FORCE_SCORE, jnp.where(valid, imp_t, NEG_INF))
        rank = jnp.zeros((n_sel, tq), F32)
        for jp in range(n_sel):
            sj = score[jp:jp + 1, :]
            beats = (sj > score) | ((sj == score) & (sel_j > jp))
            rank = rank + jnp.where(beats, 1.0, 0.0)
        sel_s[g] = jnp.where(rank < SEL_TOPK, 1.0, 0.0)

    m_s[...] = jnp.full(m_s.shape, M_INIT, F32)
    acc_s[...] = jnp.zeros(acc_s.shape, F32)
    ones_rows = jnp.ones((DENOM_ROWS, tk), BF16)
    with_ones = lambda v_t: jnp.concatenate([v_t.astype(BF16), ones_rows], axis=0)
    n_tiles = (i + 1) * tiles_per_q
    first_win = jnp.maximum((i * tq - (WINDOW - 1)) // tk, 0)
    key_rows = lambda kt: pl.ds(pl.multiple_of(kt * tk, tk), tk)

    def issue_scores(kt, slot, branch):
        k = kpm_ref[0, key_rows(kt), branch * KV_COLS:(branch + 1) * KV_COLS]
        for g in groups:
            raw_s[slot, NSA_KV_HEADS * branch + g] = _dot(k, q_pads[g])

    def slc_update(kt, slot):
        causal = kt * tk + key_col <= qpos_row
        for g in groups:
            chosen = jnp.concatenate(
                [jnp.broadcast_to(sel_s[g, pl.ds(kt * blocks_per_tile + r, 1), :], (SEL_BLOCK, tq))
                 for r in range(blocks_per_tile)], axis=0)
            v_aug = with_ones(vst_ref[0, grows[g], key_rows(kt)])
            _flash_step(raw_s.at[slot, g], v_aug, (chosen > 0.5) & causal, m_s.at[g], acc_s.at[g])

    def win_update(kt, slot):
        dist = qpos_row - (kt * tk + key_col)
        mask = (dist >= 0) & (dist < WINDOW)
        for g in groups:
            c = NSA_KV_HEADS + g
            v_aug = with_ones(vwt_ref[0, grows[g], key_rows(kt)])
            _flash_step(raw_s.at[slot, c], v_aug, mask, m_s.at[c], acc_s.at[c])

    def far_body(pair, _):
        for slot in (0, 1):
            kt = 2 * pair + slot
            issue_scores(kt + 1, 1 - slot, 0)
            slc_update(kt, slot)
        return 0

    def band_body(pair, _):
        for slot in (0, 1):
            kt = 2 * pair + slot
            nxt = jnp.minimum(kt + 1, n_tiles - 1)
            issue_scores(nxt, 1 - slot, 0)
            issue_scores(nxt, 1 - slot, 1)
            slc_update(kt, slot)
            win_update(kt, slot)
        return 0

    issue_scores(0, 0, 0)
    lax.fori_loop(0, first_win // 2, far_body, 0)
    issue_scores(first_win, 0, 1)
    lax.fori_loop(first_win // 2, n_tiles // 2, band_body, 0)

    head_out = [None] * NSA_HEADS
    for g in groups:
        normalised = lambda c: acc_s[c, 0:HEAD_DIM, :] / acc_s[c, HEAD_DIM:HEAD_DIM + 1, :]
        o_slc = normalised(g)
        o_win = normalised(NSA_KV_HEADS + g)
        for hh in range(HEADS_PER_GROUP):
            h = g * HEADS_PER_GROUP + hh
            lanes = slice(hh * tq, (hh + 1) * tq)
            head_out[h] = (gate_t[3 * h:3 * h + 1, :] * o_cmps[g][:, lanes]
                           + gate_t[3 * h + 1:3 * h + 2, :] * o_slc[:, lanes]
                           + gate_t[3 * h + 2:3 * h + 3, :] * o_win[:, lanes])
    o_t = jnp.concatenate(head_out, axis=0).astype(BF16)
    o_ref[0] = _dot_nt(ident_ref[...], o_t).astype(o_ref.dtype)


def nsa_prompt(q_t, gate_t, kcmp, vcmp_t, kpm, vs_t, vw_t, mselt, ident, tq, tk):
    b, _, t = q_t.shape
    feat_tile = lambda f: pl.BlockSpec((1, f, tq), lambda i, j: (i, 0, j))
    whole = lambda a: pl.BlockSpec((1,) + a.shape[1:], lambda i, j: (i, 0, 0))
    per_batch = [kcmp, vcmp_t, kpm, vs_t, vw_t]
    n_chain = 2 * NSA_KV_HEADS
    assert tq % (2 * tk) == 0 and WINDOW % (2 * tk) == 0 and tk % SEL_BLOCK == 0
    return pl.pallas_call(
        functools.partial(_nsa_prompt_kernel, tq=tq, tk=tk),
        grid=(b, t // tq),
        in_specs=([feat_tile(NSA_WIDTH), feat_tile(LANE)] + [whole(a) for a in per_batch]
                  + [_full(mselt.shape), _full(ident.shape)]),
        out_specs=pl.BlockSpec((1, tq, NSA_WIDTH), lambda i, j: (i, j, 0)),
        out_shape=jax.ShapeDtypeStruct((b, t, NSA_WIDTH), BF16),
        scratch_shapes=[pltpu.VMEM((NSA_KV_HEADS, mselt.shape[0], tq), F32),
                        pltpu.VMEM((2, n_chain, tk, HEADS_PER_GROUP * tq), F32),
                        pltpu.VMEM((n_chain, 1, HEADS_PER_GROUP * tq), F32),
                        pltpu.VMEM((n_chain, HEAD_DIM + DENOM_ROWS, HEADS_PER_GROUP * tq), F32)],
        compiler_params=_cparams("parallel", "arbitrary"),
    )(q_t, gate_t, *per_batch, mselt, ident)


def _stack_heads(q, g):
    return jnp.concatenate(
        [q[:, (g * HEADS_PER_GROUP + hh) * HEAD_DIM:(g * HEADS_PER_GROUP + hh + 1) * HEAD_DIM]
         for hh in range(HEADS_PER_GROUP)], axis=0)


def _sample_cmp_kernel(pt_ref, k_hbm, v_hbm, kn_ref, vn_ref, q_ref, wkt_ref, wvt_ref, subpool_ref, ident_ref,
                       ocmp_ref, sel_ref, sub_k, sub_v, kbuf, vbuf, sems, *, n_valid, past_len, layer, n_layers):
    p = PAGES_PER_STEP
    tp = SAMPLE_T_PAD
    c = pl.program_id(1)
    n_sub = past_len // CMP_STRIDE
    steps = past_len // (p * PAGE_SIZE)

    pages = _fetch_pages(pt_ref, k_hbm, v_hbm, kbuf, vbuf, sems, layer, n_layers)
    for x, sub in zip(pages, (sub_k, sub_v)):
        hi = x.astype(BF16)
        lo = (x - hi.astype(F32)).astype(BF16)
        sub[c] = _dot(hi, subpool_ref[...]) + _dot(lo, subpool_ref[...])

    @pl.when(c == steps - 1)
    def _():
        t_col = lax.broadcasted_iota(jnp.int32, (tp, 1), 0)
        new_valid = lax.broadcasted_iota(jnp.int32, (1, tp), 1) < n_valid
        first_lane = lax.broadcasted_iota(jnp.int32, (1, SUB_PER_STEP), 1) == 0
        cmp_kv = []
        for new_ref, sub, wt_ref in ((kn_ref, sub_k, wkt_ref), (vn_ref, sub_v, wvt_ref)):
            new_t = None
            for part in _split3(new_ref[0]):
                d = _dot_nt(ident_ref[...], part)
                new_t = d if new_t is None else new_t + d
            new_sum = jnp.sum(jnp.where(new_valid, new_t, 0.0), axis=1, keepdims=True)
            sub[steps] = jnp.where(first_lane, new_sum, 0.0)
            sums = jnp.concatenate([sub[s] for s in range(steps + 1)], axis=1)
            mean_t = (sums[:, 0:n_sub] + sums[:, 1:n_sub + 1]) * (1.0 / CMP_BLOCK)
            cmp_kv.append(_dot(wt_ref[...], mean_t.astype(BF16)).astype(BF16))
        kcmp_t, vcmp_t = cmp_kv

        rows = HEADS_PER_GROUP * tp
        qpos_col = past_len + lax.broadcasted_iota(jnp.int32, (rows, 1), 0) % tp
        cmp_end_row = lax.broadcasted_iota(jnp.int32, (1, n_sub), 1) * CMP_STRIDE + (CMP_BLOCK - 1)
        cmp_mask = cmp_end_row <= qpos_col

        n_lane = sel_ref.shape[2]
        n_sel = -(-(past_len + n_valid) // SEL_BLOCK)
        lane = lax.broadcasted_iota(jnp.int32, (1, n_lane), 1)
        sel_j = lane // SEL_PER_CMP
        lane_used = (lane % SEL_PER_CMP == 0) & (sel_j < n_sel)
        qpos_t = past_len + t_col
        cur = qpos_t // SEL_BLOCK
        valid = lane_used & (sel_j * SEL_BLOCK <= qpos_t)
        forced = lane_used & ((sel_j == 0) | (sel_j == cur) | (sel_j == cur - 1))
        lane_pad = jnp.zeros((tp, n_lane - n_sub), F32)

        q = q_ref[0]
        o_heads = [None] * NSA_HEADS
        scores = []
        for g in range(NSA_KV_HEADS):
            gcols = slice(g * HEAD_DIM, (g + 1) * HEAD_DIM)
            prob = _masked_softmax_rows(_dot(_stack_heads(q, g), kcmp_t[gcols, :]), cmp_mask)
            o_g = _dot_nt(prob.astype(BF16), vcmp_t[gcols, :])
            psum = prob[0:tp]
            for hh in range(1, HEADS_PER_GROUP):
                psum = psum + prob[hh * tp:(hh + 1) * tp]
                o_heads[g * HEADS_PER_GROUP + hh] = o_g[hh * tp:(hh + 1) * tp]
            o_heads[g * HEADS_PER_GROUP] = o_g[0:tp]
            pz = jnp.concatenate([psum, lane_pad], axis=1)
            imp = pltpu.roll(pz, 1, 1) + pz
            for d in range(1, SEL_PER_CMP):
                imp = imp + pltpu.roll(pz, n_lane - d, 1)
            scores.append(jnp.where(forced, FORCE_SCORE, jnp.where(valid, imp, NEG_INF)))
        score = jnp.concatenate(scores, axis=0)
        chosen = jnp.zeros(score.shape, F32)
        for _ in range(SEL_TOPK):
            best = jnp.max(score, axis=-1, keepdims=True)
            first = jnp.min(jnp.where(score == best, lane, n_lane), axis=-1, keepdims=True)
            hit = lane == first
            chosen = jnp.where(hit, 1.0, chosen)
            score = jnp.where(hit, REMOVED_SCORE, score)
        sel_ref[0] = chosen
        ocmp_ref[0] = jnp.concatenate(o_heads, axis=-1)


def _fetch_pages(pt_ref, k_hbm, v_hbm, kbuf, vbuf, sems, layer, n_layers):
    b, c = pl.program_id(0), pl.program_id(1)
    n_b, steps = pl.num_programs(0), pl.num_programs(1)
    step = b * steps + c
    slot = step % 2

    def copies(bb, cc, to_slot, for_wait=False):
        out = []
        for k in range(PAGES_PER_STEP):
            page = 0 if for_wait else pt_ref[bb, cc * PAGES_PER_STEP + k] * n_layers + layer
            out.append(pltpu.make_async_copy(k_hbm.at[page], kbuf.at[to_slot, k], sems.at[to_slot, 0]))
            out.append(pltpu.make_async_copy(v_hbm.at[page], vbuf.at[to_slot, k], sems.at[to_slot, 1]))
        return out

    @pl.when(step == 0)
    def _():
        for cp in copies(0, 0, 0):
            cp.start()

    @pl.when(step + 1 < n_b * steps)
    def _():
        wrap = c + 1 == steps
        for cp in copies(jnp.where(wrap, b + 1, b), jnp.where(wrap, 0, c + 1), 1 - slot):
            cp.start()

    for cp in copies(b, c, slot, for_wait=True):
        cp.wait()
    k_all = jnp.concatenate([kbuf[slot, k] for k in range(PAGES_PER_STEP)], axis=1)
    v_all = jnp.concatenate([vbuf[slot, k] for k in range(PAGES_PER_STEP)], axis=1)
    return k_all, v_all


_PAGE_SCRATCH = [pltpu.VMEM((2, PAGES_PER_STEP, KV_COLS, PAGE_SIZE), F32),
                 pltpu.VMEM((2, PAGES_PER_STEP, KV_COLS, PAGE_SIZE), F32),
                 pltpu.SemaphoreType.DMA((2, 2))]


def sample_cmp(page_table, cache_k, cache_v, kc_new, vc_new, q, wk_bd_t, wv_bd_t, subpool, ident,
               layer, n_layers, n_valid):
    b, n_pages = page_table.shape
    past_len = n_pages * PAGE_SIZE
    steps = n_pages // PAGES_PER_STEP
    assert SUB_PER_STEP == LANE
    n_lane = (steps + 1) * LANE
    per_b = lambda a: pl.BlockSpec((1,) + a.shape[1:], lambda i, c, pt: (i, 0, 0))
    const = lambda a: pl.BlockSpec(a.shape, lambda i, c, pt: (0,) * a.ndim)
    consts = [wk_bd_t, wv_bd_t, subpool, ident]
    hbm = pl.BlockSpec(memory_space=pl.ANY)
    grid_spec = pltpu.PrefetchScalarGridSpec(
        num_scalar_prefetch=1,
        grid=(b, steps),
        in_specs=[hbm, hbm, per_b(kc_new), per_b(vc_new), per_b(q)] + [const(a) for a in consts],
        out_specs=[pl.BlockSpec((1, SAMPLE_T_PAD, NSA_WIDTH), lambda i, c, pt: (i, 0, 0)),
                   pl.BlockSpec((1, 2 * SAMPLE_T_PAD, n_lane), lambda i, c, pt: (i, 0, 0))],
        scratch_shapes=[pltpu.VMEM((steps + 1, KV_COLS, SUB_PER_STEP), F32)] * 2 + _PAGE_SCRATCH,
    )
    return pl.pallas_call(
        functools.partial(_sample_cmp_kernel, n_valid=n_valid, past_len=past_len, layer=layer,
                          n_layers=n_layers),
        grid_spec=grid_spec,
        out_shape=[jax.ShapeDtypeStruct((b, SAMPLE_T_PAD, NSA_WIDTH), F32),
                   jax.ShapeDtypeStruct((b, 2 * SAMPLE_T_PAD, n_lane), F32)],
        compiler_params=_cparams("arbitrary", "arbitrary"),
    )(page_table, cache_k, cache_v, kc_new, vc_new, q, *consts)


def _sample_slc_kernel(pt_ref, k_hbm, v_hbm, q_ref, sel_ref, expand_ref, ksn_ref, vsn_ref, wk_ref, wv_ref,
                       kwn_ref, vwn_ref, ocmp_ref, gate_ref, o_ref, m_s, l_s, acc_s, kbuf, vbuf, sems,
                       *, n_valid, past_len, layer, n_layers):
    tp = SAMPLE_T_PAD
    rows = HEADS_PER_GROUP * tp
    c = pl.program_id(1)
    q = q_ref[0]
    zero = jnp.zeros((rows, HEAD_DIM), BF16)
    q_bd = jnp.concatenate([jnp.concatenate([_stack_heads(q, 0), zero], axis=1),
                            jnp.concatenate([zero, _stack_heads(q, 1)], axis=1)], axis=0)
    t_col = lax.broadcasted_iota(jnp.int32, (NSA_KV_HEADS * rows, 1), 0) % tp
    new_row = lax.broadcasted_iota(jnp.int32, (1, tp), 1)
    new_mask = (new_row <= t_col) & (new_row < n_valid)

    @pl.when(c == 0)
    def _():
        m_s[...] = jnp.full(m_s.shape, NEG_INF, F32)
        l_s[...] = jnp.zeros(l_s.shape, F32)
        acc_s[...] = jnp.zeros(acc_s.shape, F32)

    k_all, v_all = _fetch_pages(pt_ref, k_hbm, v_hbm, kbuf, vbuf, sems, layer, n_layers)
    k_all_t, v_all_t = k_all.astype(BF16), v_all.astype(BF16)
    chosen = _dot(sel_ref[0].astype(BF16), expand_ref[...])
    keymask = jnp.concatenate([chosen[0:tp]] * HEADS_PER_GROUP + [chosen[tp:2 * tp]] * HEADS_PER_GROUP,
                              axis=0) > 0.5
    s = jnp.where(keymask, _dot(q_bd, k_all_t), NEG_INF)
    m_old = m_s[...]
    m_new = jnp.maximum(m_old, jnp.max(s, axis=-1, keepdims=True))
    alpha = jnp.exp(m_old - m_new)
    e = jnp.where(keymask, jnp.exp(s - m_new), 0.0)
    l_s[...] = alpha * l_s[...] + jnp.sum(e, axis=-1, keepdims=True)
    acc_s[...] = alpha * acc_s[...] + _dot_nt(e.astype(BF16), v_all_t)
    m_s[...] = m_new

    @pl.when(c == pl.num_programs(1) - 1)
    def _():
        n_win = wk_ref.shape[2]
        win_row = lax.broadcasted_iota(jnp.int32, (1, n_win), 1)
        dist = t_col + n_win - win_row
        win_mask = (dist >= 0) & (dist < WINDOW)
        s_n = jnp.where(new_mask, _dot_nt(q_bd, ksn_ref[0].astype(BF16)), NEG_INF)
        m_fin = jnp.maximum(m_new, jnp.max(s_n, axis=-1, keepdims=True))
        a_fin = jnp.exp(m_new - m_fin)
        e_n = jnp.where(new_mask, jnp.exp(s_n - m_fin), 0.0)
        l_fin = a_fin * l_s[...] + jnp.sum(e_n, axis=-1, keepdims=True)
        o_slc = (a_fin * acc_s[...] + _dot(e_n.astype(BF16), vsn_ref[0].astype(BF16))) / l_fin
        s_p = jnp.where(win_mask, _dot(q_bd, wk_ref[0].astype(BF16)), NEG_INF)
        s_w = jnp.where(new_mask, _dot_nt(q_bd, kwn_ref[0].astype(BF16)), NEG_INF)
        m_w = jnp.maximum(jnp.max(s_p, axis=-1, keepdims=True), jnp.max(s_w, axis=-1, keepdims=True))
        e_p = jnp.where(win_mask, jnp.exp(s_p - m_w), 0.0)
        e_w = jnp.where(new_mask, jnp.exp(s_w - m_w), 0.0)
        l_w = jnp.sum(e_p, axis=-1, keepdims=True) + jnp.sum(e_w, axis=-1, keepdims=True)
        o_win = (_dot_nt(e_p.astype(BF16), wv_ref[0].astype(BF16))
                 + _dot(e_w.astype(BF16), vwn_ref[0].astype(BF16))) / l_w
        gate_all = gate_ref[0]
        head_out = []
        for h in range(NSA_HEADS):
            g, hh = divmod(h, HEADS_PER_GROUP)
            r = slice(g * rows + hh * tp, g * rows + (hh + 1) * tp)
            cols = slice(g * HEAD_DIM, (g + 1) * HEAD_DIM)
            gate = gate_all[:, 3 * h:3 * h + 3]
            head_out.append(gate[:, 0:1] * ocmp_ref[0, :, h * HEAD_DIM:(h + 1) * HEAD_DIM]
                            + gate[:, 1:2] * o_slc[r, cols] + gate[:, 2:3] * o_win[r, cols])
        o_ref[0] = jnp.concatenate(head_out, axis=-1).astype(o_ref.dtype)


def sample_slc(page_table, cache_k, cache_v, q, sel, expand, ks_new, vs_new, win_k, win_v, kw_new, vw_new,
               o_cmp, gate, layer, n_layers, n_valid):
    b, n_pages = page_table.shape
    past_len = n_pages * PAGE_SIZE
    steps = n_pages // PAGES_PER_STEP
    rows = HEADS_PER_GROUP * SAMPLE_T_PAD
    per_b = lambda a: pl.BlockSpec((1,) + a.shape[1:], lambda i, c, pt: (i, 0, 0))
    const = lambda a: pl.BlockSpec(a.shape, lambda i, c, pt: (0,) * a.ndim)
    hbm = pl.BlockSpec(memory_space=pl.ANY)
    grid_spec = pltpu.PrefetchScalarGridSpec(
        num_scalar_prefetch=1,
        grid=(b, steps),
        in_specs=[hbm, hbm, per_b(q), pl.BlockSpec((1, 2 * SAMPLE_T_PAD, LANE), lambda i, c, pt: (i, 0, c)),
                  const(expand), per_b(ks_new), per_b(vs_new), per_b(win_k), per_b(win_v),
                  per_b(kw_new), per_b(vw_new), per_b(o_cmp), per_b(gate)],
        out_specs=pl.BlockSpec((1, SAMPLE_T_PAD, NSA_WIDTH), lambda i, c, pt: (i, 0, 0)),
        scratch_shapes=[pltpu.VMEM((NSA_KV_HEADS * rows, 1), F32), pltpu.VMEM((NSA_KV_HEADS * rows, 1), F32),
                        pltpu.VMEM((NSA_KV_HEADS * rows, KV_COLS), F32)] + _PAGE_SCRATCH,
    )
    return pl.pallas_call(
        functools.partial(_sample_slc_kernel, n_valid=n_valid, past_len=past_len, layer=layer,
                          n_layers=n_layers),
        grid_spec=grid_spec,
        out_shape=jax.ShapeDtypeStruct((b, SAMPLE_T_PAD, NSA_WIDTH), BF16),
        compiler_params=_cparams("arbitrary", "arbitrary"),
    )(page_table, cache_k, cache_v, q, sel, expand, ks_new, vs_new, win_k, win_v, kw_new, vw_new, o_cmp, gate)


def _lru_kernel(*refs, tt, first_pos_zero, fused_in_proj):
    n_lead = 3 if fused_in_proj else 2
    (cpast_ref, h0_ref, cw_ref, cb_ref, wa_ref, wx_ref, ba_ref, bx_ref, lam_ref,
     o_ref, tail_ref, xtail_ref, xbuf, h_carry, a_s, d_s) = refs[n_lead:]
    j = pl.program_id(1)

    @pl.when(j == 0)
    def _():
        xbuf[...] = cpast_ref[0]
        h_carry[...] = h0_ref[0]

    sub = lax.broadcasted_iota(jnp.int32, (8, LRU_WIDTH), 0)
    if fused_in_proj:
        x_ref, g_ref, win_ref = refs[:n_lead]
        h_in = _rms(x_ref[0], g_ref[...]).astype(BF16)
        x_cur = _dot(h_in, win_ref[:, 0:LRU_WIDTH])
        gy = jax.nn.gelu(_dot(h_in, win_ref[:, LRU_WIDTH:]))
    else:
        x_cur = refs[0][0]
        gy = refs[1][0]
    prev8 = xbuf[...]

    def delayed(k):
        if k == 0:
            return x_cur
        rolled = pltpu.roll(x_cur, k, 0)
        head = jnp.where(sub < k, pltpu.roll(prev8, k, 0), rolled[0:8])
        return jnp.concatenate([head, rolled[8:]], axis=0) if tt > 8 else head

    xc = delayed(CONV_WIDTH - 1) * cw_ref[0:1, :]
    for k in range(1, CONV_WIDTH):
        xc = xc + delayed(CONV_WIDTH - 1 - k) * cw_ref[k:k + 1, :]
    xf = xc + cb_ref[...]
    xf_b = xf.astype(BF16)
    r_parts, i_parts = [], []
    for n in range(LRU_BLOCKS):
        cols = slice(n * LRU_BLOCK, (n + 1) * LRU_BLOCK)
        r_parts.append(_dot(xf_b[:, cols], wa_ref[n]))
        i_parts.append(_dot(xf_b[:, cols], wx_ref[n]))
    sigmoid = lambda v: 0.5 * jnp.tanh(0.5 * v) + 0.5
    r = sigmoid(jnp.concatenate(r_parts, axis=-1) + ba_ref[...])
    gate_i = sigmoid(jnp.concatenate(i_parts, axis=-1) + bx_ref[...])
    neg_lam = -lam_ref[...]
    softplus = jnp.maximum(neg_lam, 0.0) + jnp.log1p(jnp.exp(-jnp.abs(neg_lam)))
    log_a = r * (-LRU_C * softplus)
    a = jnp.exp(log_a)
    gap = -jnp.tanh(log_a) * (a * a + 1.0)
    mult = jnp.where(gap > 0.0, gap * lax.rsqrt(gap), 0.0)
    if first_pos_zero:
        row = lax.broadcasted_iota(jnp.int32, (tt, 1), 0)
        mult = jnp.where((row == 0) & (j == 0), 1.0, mult)
    a_s[...] = a
    d_s[...] = mult * gate_i * xf

    def body(blk, h):
        rows = pl.ds(pl.multiple_of(blk * 8, 8), 8)
        a8 = a_s[rows, :]
        d8 = d_s[rows, :]
        for s in (1, 2, 4):
            keep = sub >= s
            d8 = jnp.where(keep, a8 * pltpu.roll(d8, s, 0) + d8, d8)
            a8 = jnp.where(keep, a8 * pltpu.roll(a8, s, 0), a8)
        hs = a8 * h + d8
        d_s[rows, :] = hs
        return hs[7:8, :]

    n_blk = tt // 8
    h_carry[...] = lax.fori_loop(0, n_blk, body, h_carry[...], unroll=min(4, n_blk))
    hs = d_s[...]
    o_ref[0] = (hs * gy).astype(o_ref.dtype)
    tail_ref[0] = d_s[tt - 8:tt, :]
    xtail_ref[0] = x_cur[tt - 8:tt, :]
    xbuf[...] = x_cur[tt - 8:tt, :]


def lru(lead, conv_past, h0, cw, cb, wa, wx, ba, bx, lam, tt, first_pos_zero, fused_in_proj):
    b, t, _ = lead[0].shape
    w = LRU_WIDTH
    tile = lambda a: pl.BlockSpec((1, tt, a.shape[2]), lambda i, j: (i, j, 0))
    per_b = lambda a: pl.BlockSpec((1,) + a.shape[1:], lambda i, j: (i, 0, 0))
    last8 = pl.BlockSpec((1, 8, w), lambda i, j: (i, 0, 0))
    consts = [cw, cb, wa, wx, ba, bx, lam]
    lead_specs = ([tile(lead[0]), _full(lead[1].shape), _full(lead[2].shape)] if fused_in_proj
                  else [tile(lead[0]), tile(lead[1])])
    return pl.pallas_call(
        functools.partial(_lru_kernel, tt=tt, first_pos_zero=first_pos_zero, fused_in_proj=fused_in_proj),
        grid=(b, t // tt),
        in_specs=lead_specs + [per_b(conv_past), per_b(h0)] + [_full(a.shape) for a in consts],
        out_specs=[pl.BlockSpec((1, tt, w), lambda i, j: (i, j, 0)), last8, last8],
        out_shape=[jax.ShapeDtypeStruct((b, t, w), BF16), jax.ShapeDtypeStruct((b, 8, w), F32),
                   jax.ShapeDtypeStruct((b, 8, w), F32)],
        scratch_shapes=[pltpu.VMEM((8, w), F32), pltpu.VMEM((1, w), F32),
                        pltpu.VMEM((tt, w), F32), pltpu.VMEM((tt, w), F32)],
        compiler_params=_cparams("parallel", "arbitrary"),
    )(*lead, conv_past, h0, *consts)


def _block_mean_matrix(t):
    n_cmp = t // CMP_STRIDE - CMP_BLOCK // CMP_STRIDE + 1
    m = np.zeros((t // CMP_STRIDE, t), np.float32)
    for i in range(n_cmp):
        m[i, i * CMP_STRIDE:i * CMP_STRIDE + CMP_BLOCK] = 1.0 / CMP_BLOCK
    return m


def _sel_from_cmp(n_cmp_rows, n_cmp, n_sel):
    m = np.zeros((n_cmp_rows, n_sel), np.float32)
    for j in range(n_sel):
        for c in range(4 * j - 1, 4 * j + 4):
            if 0 <= c < n_cmp:
                m[c, j] = 1.0
    return m


def _block_diag(w):
    z = jnp.zeros((HEAD_DIM, HEAD_DIM), w.dtype)
    return jnp.concatenate([jnp.concatenate([w[0], z], axis=1), jnp.concatenate([z, w[1]], axis=1)], axis=0)


def _sigmoid(v):
    return jax.nn.sigmoid(v)


def _scale_q(v):
    return v * (HEAD_DIM ** -0.5)


_EVEN_GROUPS = (
    (0, POOL_WIDTH, ((0, POOL_WIDTH, None),)),
    (POOL_WIDTH, NSA_WIDTH, ((0, NSA_WIDTH, _scale_q),)),
    (POOL_WIDTH + NSA_WIDTH, 6 * KV_COLS, tuple((k * KV_COLS, KV_COLS, None) for k in range(6))),
    (POOL_WIDTH + NSA_WIDTH + 6 * KV_COLS, LANE, ((0, LANE, _sigmoid),)),
)
_EVEN_DTYPES = (F32, BF16) + (F32,) * 6 + (F32,)


def _even_in_proj(x2d, g, w_in_pad, tm):
    return norm_matmul(x2d, g, w_in_pad, _EVEN_GROUPS, _EVEN_DTYPES, tm)


def _xattn_block(x2d, b, t, mk, mv, layer, g_pre, g_post, wq, wo, tm, tq):
    q, = norm_matmul(x2d, g_pre, wq, ((0, D_MODEL, ((0, D_MODEL, lambda v: v * (XATTN_HEAD_DIM ** -0.5)),)),),
                     (BF16,), tm)
    o = xattn(q.reshape(b, t, D_MODEL), mk, mv, layer, tq).reshape(b * t, D_MODEL)
    return proj_norm_res([o], [wo], g_post, x2d, tm)


def kernel(x_prompt, mem_prompt, x_sample, state_pool, cache_cmp_k, cache_cmp_v, cache_slc_k, cache_slc_v, cache_win_k, cache_win_v, state_lru_h, state_lru_conv, cache_mem_k, cache_mem_v, page_table, norm_gain, mem_norm_gain, w_in_even, pool_w, pool_scale, w_cmp_k, w_cmp_v, w_out_even, w_in_odd, conv_w, conv_b, lru_wa, lru_ba, lru_wx, lru_bx, lru_lambda, w_out_odd, w_xq, w_xk, w_xv, w_xo, w_ffn_gate, w_ffn_up, w_ffn_down):
    bp, tp, d = x_prompt.shape
    bs, ts, _ = x_sample.shape
    depth = norm_gain.shape[0]
    n_even = w_in_even.shape[0]
    n_pages = page_table.shape[1]
    past_len = n_pages * PAGE_SIZE
    n_phys = cache_cmp_k.shape[0]
    tsp = SAMPLE_T_PAD
    np_tok, ns_tok = bp * tp, bs * tsp
    tm_p, tm_s = 512, ns_tok
    tq, tk = 256, 128
    tt_lru = min(512, tp)

    xp = x_prompt.reshape(np_tok, d)
    xs = jnp.pad(x_sample, ((0, 0), (0, tsp - ts), (0, 0))).reshape(ns_tok, d)

    n_sub_p = tp // CMP_STRIDE
    n_cmp_p = n_sub_p - 1
    n_sel_p = tp // SEL_BLOCK
    pool_mat = jnp.asarray(_block_mean_matrix(tp).T, BF16)
    mselt = jnp.asarray(_sel_from_cmp(n_sub_p, n_cmp_p, n_sel_p).T, BF16)
    ident = jnp.asarray(np.eye(tq, dtype=np.float32), BF16)
    ident_kv = jnp.asarray(np.eye(KV_COLS, dtype=np.float32), BF16)
    subpool = jnp.asarray(np.repeat(np.eye(SUB_PER_STEP, dtype=np.float32), CMP_STRIDE, axis=0), BF16)
    expand_np = np.zeros((LANE, PAGES_PER_STEP * PAGE_SIZE), np.float32)
    expand_np[::SEL_PER_CMP] = np.repeat(np.eye(SEL_PER_STEP, dtype=np.float32), SEL_BLOCK, axis=1)
    expand_s = jnp.asarray(expand_np, BF16)

    row = lambda v: v.reshape(1, -1)
    out_ev_p, out_ev_s, out_lru_p, out_lru_s = [], [], [], []

    mem_k_all, mem_v_all = memory_kv(mem_prompt.reshape(bp * N_MEM, d), mem_norm_gain[:, None, :],
                                     jnp.concatenate([w_xk, w_xv], axis=2).astype(BF16), min(tm_p, bp * N_MEM))

    for li in range(depth):
        gn = norm_gain[li]
        if li % 2 == 0:
            e = li // 2
            w_in = jnp.pad(w_in_even[e], ((0, 0), (0, EVEN_IN_PAD - EVEN_IN))).astype(BF16)
            pw = pool_w[e].astype(BF16)
            ps = row(pool_scale[e])
            wk_bd_t = _block_diag(w_cmp_k[e]).T.astype(BF16)
            wv_bd_t = _block_diag(w_cmp_v[e]).T.astype(BF16)
            w_out = w_out_even[e].astype(BF16)
            kv0 = POOL_WIDTH + NSA_WIDTH
            kcol = lambda k: slice(kv0 + k * KV_COLS, kv0 + (k + 1) * KV_COLS)
            w_u = w_in[:, :POOL_WIDTH]
            w_kpm = jnp.concatenate([w_in[:, kcol(2)], w_in[:, kcol(4)]], axis=1)
            w_t = w_in[:, POOL_WIDTH:kv0 + 7 * KV_COLS].T

            u, kpm, q_t, gate_t, kc, vc, ks, vs, kw, vw = even_in_prompt(xp, row(gn[0]), w_u, w_kpm, w_t,
                                                                         bp, tp, tm_p)
            u3 = u.reshape(bp, tp, POOL_WIDTH)
            a_out = pool_mix(u3, jnp.zeros((bp, POOL_HALO, POOL_WIDTH), F32), pw, ps, 0)
            kcmp, vcmp_t = compress_kv(kc, vc, pool_mat, wk_bd_t, wv_bd_t, ident_kv)
            o_nsa = nsa_prompt(q_t, gate_t, kcmp, vcmp_t, kpm.reshape(bp, tp, 2 * KV_COLS), vs, vw,
                               mselt, ident, tq, tk)
            mix_p = ([a_out, o_nsa], [w_out[:POOL_WIDTH], w_out[POOL_WIDTH:]])
            kv5 = lambda a: a.reshape(bp, NSA_KV_HEADS, HEAD_DIM, a.shape[-1]).transpose(0, 3, 1, 2)
            n_keep = min(WINDOW, tp)
            out_ev_p.append((u3[:, -POOL_STATE:], kv5(kc), kv5(vc), kv5(ks), kv5(vs),
                             kv5(kw[:, :, -n_keep:]), kv5(vw[:, :, -n_keep:])))

            u, q, kc, vc, ks, vs, kw, vw, gate = _even_in_proj(xs, row(gn[0]), w_in, tm_s)
            as3 = lambda a: a.reshape(bs, tsp, a.shape[-1])
            u3 = as3(u)
            past_pool = jnp.pad(state_pool[e], ((0, 0), (POOL_HALO - POOL_STATE, 0), (0, 0)))
            a_out = pool_mix(u3, past_pool, pw, ps, past_len)
            paged = lambda c: c.transpose(0, 1, 3, 4, 2).reshape(n_phys * n_even, KV_COLS, PAGE_SIZE)
            o_cmp, sel = sample_cmp(page_table, paged(cache_cmp_k), paged(cache_cmp_v), as3(kc), as3(vc),
                                    as3(q), wk_bd_t, wv_bd_t, subpool, ident_kv, e, n_even, ts)
            n_win = cache_win_k.shape[2]
            win_k = cache_win_k[e].transpose(0, 2, 3, 1).reshape(bs, KV_COLS, n_win)
            win_v = cache_win_v[e].transpose(0, 2, 3, 1).reshape(bs, KV_COLS, n_win)
            o_nsa = sample_slc(page_table, paged(cache_slc_k), paged(cache_slc_v), as3(q), sel, expand_s,
                               as3(ks), as3(vs), win_k, win_v, as3(kw), as3(vw), o_cmp, as3(gate),
                               e, n_even, ts)
            xs = proj_norm_res([a_out.reshape(ns_tok, POOL_WIDTH), o_nsa.reshape(ns_tok, NSA_WIDTH)],
                               [w_out[:POOL_WIDTH], w_out[POOL_WIDTH:]], row(gn[1]), xs, tm_s)
            new4 = lambda a: as3(a)[:, :ts].reshape(bs, ts, NSA_KV_HEADS, HEAD_DIM)
            out_ev_s.append((jnp.concatenate([state_pool[e], u3[:, :ts]], axis=1)[:, -POOL_STATE:],
                             new4(kc), new4(vc), new4(ks), new4(vs),
                             jnp.concatenate([cache_win_k[e], new4(kw)], axis=1)[:, -n_win:],
                             jnp.concatenate([cache_win_v[e], new4(vw)], axis=1)[:, -n_win:]))
        else:
            o = li // 2
            w_in = w_in_odd[o].astype(BF16)
            cw = jnp.pad(conv_w[o], ((0, 8 - CONV_WIDTH), (0, 0)))
            consts = (cw, row(conv_b[o]), lru_wa[o].astype(BF16), lru_wx[o].astype(BF16),
                      row(lru_ba[o]), row(lru_bx[o]), row(lru_lambda[o]))
            w_out = w_out_odd[o].astype(BF16)
            groups = ((0, LRU_WIDTH, ((0, LRU_WIDTH, None),)),
                      (LRU_WIDTH, LRU_WIDTH, ((0, LRU_WIDTH, jax.nn.gelu),)))

            gated, tail, xb_tail = lru((xp.reshape(bp, tp, d), row(gn[0]), w_in),
                                       jnp.zeros((bp, 8, LRU_WIDTH), F32), jnp.zeros((bp, 1, LRU_WIDTH), F32),
                                       *consts, tt=tt_lru, first_pos_zero=True, fused_in_proj=True)
            mix_p = ([gated], [w_out])
            out_lru_p.append((xb_tail[:, -(CONV_WIDTH - 1):], tail[:, 7]))

            xb, yb = norm_matmul(xs, row(gn[0]), w_in, groups, (F32, F32), tm_s)
            xb3, yb3 = xb.reshape(bs, tsp, LRU_WIDTH), yb.reshape(bs, tsp, LRU_WIDTH)
            conv_past = jnp.pad(state_lru_conv[o], ((0, 0), (8 - (CONV_WIDTH - 1), 0), (0, 0)))
            gated, tail, _ = lru((xb3, yb3), conv_past, state_lru_h[o][:, None, :], *consts, tt=tsp,
                                 first_pos_zero=False, fused_in_proj=False)
            xs = proj_norm_res([gated.reshape(ns_tok, LRU_WIDTH)], [w_out], row(gn[1]), xs, tm_s)
            conv_new = jnp.concatenate([state_lru_conv[o], xb3[:, :ts]], axis=1)[:, -(CONV_WIDTH - 1):]
            out_lru_s.append((conv_new, tail[:, ts - 1]))

        wq, wo = w_xq[li].astype(BF16), w_xo[li].astype(BF16)
        mk3, mv3 = mem_k_all[li].reshape(bp, N_MEM, d), mem_v_all[li].reshape(bp, N_MEM, d)
        wg, wu, wd = w_ffn_gate[li].astype(BF16), w_ffn_up[li].astype(BF16), w_ffn_down[li].astype(BF16)
        xp = post_mixer(mix_p[0], mix_p[1], xp.reshape(bp, tp, d), gn, mk3, mv3, wq, wo, wg, wu, wd,
                        tm_p).reshape(np_tok, d)
        xs = _xattn_block(xs, bs, tsp, cache_mem_k, cache_mem_v, li, row(gn[2]), row(gn[3]), wq, wo, tm_s, tsp)
        xs = ffn(xs, row(gn[4]), row(gn[5]), wg, wu, wd, tm_s)

    stack = lambda items, k, axis=0: jnp.stack([s[k] for s in items], axis=axis)
    y_prompt = xp.reshape(bp, tp, d)
    y_sample = xs.reshape(bs, tsp, d)[:, :ts]
    return (y_prompt, y_sample,
            stack(out_ev_p, 0), stack(out_ev_s, 0),
            stack(out_ev_p, 1, 1), stack(out_ev_s, 1, 1),
            stack(out_ev_p, 2, 1), stack(out_ev_s, 2, 1),
            stack(out_ev_p, 3, 1), stack(out_ev_s, 3, 1),
            stack(out_ev_p, 4, 1), stack(out_ev_s, 4, 1),
            stack(out_ev_p, 5), stack(out_ev_s, 5),
            stack(out_ev_p, 6), stack(out_ev_s, 6),
            stack(out_lru_p, 1), stack(out_lru_s, 1),
            stack(out_lru_p, 0), stack(out_lru_s, 0),
            mem_k_all.reshape(depth, bp, N_MEM, XATTN_HEADS, XATTN_HEAD_DIM),
            mem_v_all.reshape(depth, bp, N_MEM, XATTN_HEADS, XATTN_HEAD_DIM))
```

```python
import functools

import numpy as np
import jax
import jax.numpy as jnp
from jax import lax
from jax.experimental import pallas as pl
from jax.experimental.pallas import tpu as pltpu

F32 = jnp.float32
BF16 = jnp.bfloat16

D_MODEL = 1024
RMS_EPS = 1e-6
NEG_INF = -1e30
FORCE_SCORE = 1e30
REMOVED_SCORE = -3e38
LOG2E = 1.4426950408889634
DENOM_ROWS = 8
M_INIT = -1e29

POOL_WIDTH = 512
POOL_WINDOWS = (2, 4, 8, 16)
POOL_GROUP = 128
POOL_STATE = 15
POOL_HALO = 16

HEAD_DIM = 64
NSA_HEADS = 8
NSA_KV_HEADS = 2
HEADS_PER_GROUP = NSA_HEADS // NSA_KV_HEADS
NSA_WIDTH = 512
KV_COLS = 128
CMP_BLOCK = 32
CMP_STRIDE = 16
SEL_BLOCK = 64
SEL_TOPK = 16
WINDOW = 512
PAGE_SIZE = 128
EVEN_IN = POOL_WIDTH + NSA_WIDTH + 6 * KV_COLS + 3 * NSA_HEADS
EVEN_IN_PAD = 1920

LRU_WIDTH = 1024
LRU_BLOCKS = 4
LRU_BLOCK = 256
CONV_WIDTH = 4
LRU_C = 8.0

N_MEM = 256
XATTN_HEADS = 4
XATTN_HEAD_DIM = 256
D_FF = 2816
FF_CHUNK = 256

SAMPLE_T_PAD = 8
PAGES_PER_STEP = 16
SEL_PER_STEP = PAGES_PER_STEP * PAGE_SIZE // SEL_BLOCK
SUB_PER_STEP = PAGES_PER_STEP * PAGE_SIZE // CMP_STRIDE
SEL_PER_CMP = SEL_BLOCK // CMP_STRIDE
LANE = 128

VMEM_LIMIT_BYTES = 56 * 1024 * 1024


def _cparams(*sem):
    return pltpu.CompilerParams(dimension_semantics=sem, vmem_limit_bytes=VMEM_LIMIT_BYTES)


def _rms(x, g):
    return x * lax.rsqrt(jnp.mean(x * x, axis=-1, keepdims=True) + RMS_EPS) * g


def _dot(a, b):
    return jnp.dot(a, b, preferred_element_type=F32)


def _dot_nt(a, b):
    return lax.dot_general(a, b, (((1,), (1,)), ((), ())), preferred_element_type=F32)


def _split3(x):
    p1 = x.astype(BF16)
    r1 = x - p1.astype(F32)
    p2 = r1.astype(BF16)
    p3 = (r1 - p2.astype(F32)).astype(BF16)
    return p1, p2, p3


def _full(shape):
    n = len(shape)
    return pl.BlockSpec(shape, lambda *_: (0,) * n)


def _norm_matmul_kernel(x_ref, g_ref, w_ref, *o_refs, groups):
    h = _rms(x_ref[...], g_ref[...]).astype(BF16)
    k = 0
    for start, width, outs in groups:
        z = _dot(h, w_ref[:, start:start + width])
        for off, w, post in outs:
            v = z[:, off:off + w]
            if post is not None:
                v = post(v)
            o_refs[k][...] = v.astype(o_refs[k].dtype)
            k += 1


def norm_matmul(x, g, w, groups, out_dtypes, tm):
    n, d = x.shape
    widths = [w_ for _, _, outs in groups for _, w_, _ in outs]
    return pl.pallas_call(
        functools.partial(_norm_matmul_kernel, groups=groups),
        grid=(n // tm,),
        in_specs=[pl.BlockSpec((tm, d), lambda i: (i, 0)), _full(g.shape), _full(w.shape)],
        out_specs=[pl.BlockSpec((tm, wd), lambda i: (i, 0)) for wd in widths],
        out_shape=[jax.ShapeDtypeStruct((n, wd), dt) for wd, dt in zip(widths, out_dtypes)],
        compiler_params=_cparams("parallel"),
    )(x, g, w)


def _memory_kv_kernel(x_ref, g_ref, w_ref, k_ref, v_ref):
    h = _rms(x_ref[...], g_ref[0]).astype(BF16)
    d = k_ref.shape[-1]
    k_ref[0] = _dot(h, w_ref[0, :, 0:d])
    v_ref[0] = _dot(h, w_ref[0, :, d:])


def memory_kv(mem, gains, wkv, tm):
    n, d = mem.shape
    n_layers = wkv.shape[0]
    out = pl.BlockSpec((1, tm, d), lambda l, i: (l, i, 0))
    return pl.pallas_call(
        _memory_kv_kernel,
        grid=(n_layers, n // tm),
        in_specs=[pl.BlockSpec((tm, d), lambda l, i: (i, 0)), pl.BlockSpec((1, 1, d), lambda l, i: (l, 0, 0)),
                  pl.BlockSpec((1, d, 2 * d), lambda l, i: (l, 0, 0))],
        out_specs=[out, out],
        out_shape=[jax.ShapeDtypeStruct((n_layers, n, d), F32)] * 2,
        compiler_params=_cparams("parallel", "parallel"),
    )(mem, gains, wkv)


def _even_in_prompt_kernel(x_ref, g_ref, wu_ref, wk_ref, wt_ref, u_ref, kpm_ref, qt_ref, gatet_ref, *kvt_refs):
    h = _rms(x_ref[...], g_ref[...]).astype(BF16)
    u_ref[...] = _dot(h, wu_ref[...])
    kpm_ref[...] = _dot(h, wk_ref[...]).astype(kpm_ref.dtype)
    z_t = _dot_nt(wt_ref[...], h)
    qt_ref[0] = (z_t[0:NSA_WIDTH] * (HEAD_DIM ** -0.5 * LOG2E)).astype(qt_ref.dtype)
    for k, ref in enumerate(kvt_refs):
        ref[0] = z_t[NSA_WIDTH + k * KV_COLS:NSA_WIDTH + (k + 1) * KV_COLS, :]
    gatet_ref[0] = jax.nn.sigmoid(z_t[NSA_WIDTH + 6 * KV_COLS:, :])


def even_in_prompt(x, g, w_u, w_kpm, w_t, b, t, tm):
    n, d = x.shape
    per_seq = t // tm
    row_tile = lambda w: pl.BlockSpec((tm, w), lambda i: (i, 0))
    feat_tile = lambda f: pl.BlockSpec((1, f, tm), lambda i: (i // per_seq, 0, i % per_seq))
    n_kv = 6
    return pl.pallas_call(
        _even_in_prompt_kernel,
        grid=(n // tm,),
        in_specs=[row_tile(d), _full(g.shape), _full(w_u.shape), _full(w_kpm.shape), _full(w_t.shape)],
        out_specs=([row_tile(POOL_WIDTH), row_tile(2 * KV_COLS), feat_tile(NSA_WIDTH), feat_tile(LANE)]
                   + [feat_tile(KV_COLS)] * n_kv),
        out_shape=([jax.ShapeDtypeStruct((n, POOL_WIDTH), F32), jax.ShapeDtypeStruct((n, 2 * KV_COLS), BF16),
                    jax.ShapeDtypeStruct((b, NSA_WIDTH, t), BF16), jax.ShapeDtypeStruct((b, LANE, t), F32)]
                   + [jax.ShapeDtypeStruct((b, KV_COLS, t), F32)] * n_kv),
        compiler_params=_cparams("parallel"),
    )(x, g, w_u, w_kpm, w_t)


def _proj_norm_res_kernel(*refs, n_in):
    a_refs, w_refs = refs[:n_in], refs[n_in:2 * n_in]
    g_ref, x_ref, o_ref = refs[2 * n_in:]
    acc = _dot(a_refs[0][...], w_refs[0][...])
    for a_ref, w_ref in zip(a_refs[1:], w_refs[1:]):
        acc = acc + _dot(a_ref[...], w_ref[...])
    o_ref[...] = x_ref[...] + _rms(acc, g_ref[...])


def proj_norm_res(a_list, w_list, g, x, tm):
    n, d = x.shape
    n_in = len(a_list)
    return pl.pallas_call(
        functools.partial(_proj_norm_res_kernel, n_in=n_in),
        grid=(n // tm,),
        in_specs=([pl.BlockSpec((tm, a.shape[1]), lambda i: (i, 0)) for a in a_list]
                  + [_full(w.shape) for w in w_list]
                  + [_full(g.shape), pl.BlockSpec((tm, d), lambda i: (i, 0))]),
        out_specs=pl.BlockSpec((tm, d), lambda i: (i, 0)),
        out_shape=jax.ShapeDtypeStruct((n, d), F32),
        compiler_params=_cparams("parallel"),
    )(*a_list, *w_list, g, x)


def _ffn_block(x, g_in, g_out, wg_ref, wu_ref, wd_ref):
    h = _rms(x, g_in).astype(BF16)
    acc = None
    for c in range(D_FF // FF_CHUNK):
        cols = slice(c * FF_CHUNK, (c + 1) * FF_CHUNK)
        gate = _dot(h, wg_ref[:, cols])
        up = _dot(h, wu_ref[:, cols])
        act = (jax.nn.silu(gate) * up).astype(BF16)
        part = _dot(act, wd_ref[cols, :])
        acc = part if acc is None else acc + part
    return x + _rms(acc, g_out)


def _ffn_kernel(x_ref, g_in_ref, g_out_ref, wg_ref, wu_ref, wd_ref, o_ref):
    o_ref[...] = _ffn_block(x_ref[...], g_in_ref[...], g_out_ref[...], wg_ref, wu_ref, wd_ref)


def ffn(x, g_in, g_out, wg, wu, wd, tm):
    n, d = x.shape
    resident = functools.partial(pl.BlockSpec, pipeline_mode=pl.Buffered(1))
    return pl.pallas_call(
        _ffn_kernel,
        grid=(n // tm,),
        in_specs=[pl.BlockSpec((tm, d), lambda i: (i, 0)), _full(g_in.shape), _full(g_out.shape),
                  resident(wg.shape, lambda i: (0, 0)), resident(wu.shape, lambda i: (0, 0)),
                  resident(wd.shape, lambda i: (0, 0))],
        out_specs=pl.BlockSpec((tm, d), lambda i: (i, 0)),
        out_shape=jax.ShapeDtypeStruct((n, d), F32),
        compiler_params=_cparams("parallel"),
    )(x, g_in, g_out, wg, wu, wd)


def _xattn_heads(q, mk_ref, mv_ref):
    head_cols = [slice(hd * XATTN_HEAD_DIM, (hd + 1) * XATTN_HEAD_DIM) for hd in range(XATTN_HEADS)]
    if len(mk_ref.shape) == 4:
        rows = q.shape[0]
        n_kv = N_MEM * XATTN_HEADS
        k_all = mk_ref[0].reshape(n_kv, XATTN_HEAD_DIM).astype(BF16)
        v_all = mv_ref[0].reshape(n_kv, XATTN_HEAD_DIM).astype(BF16)
        q_st = jnp.concatenate([q[:, cols] for cols in head_cols], axis=0)
        own = (lax.broadcasted_iota(jnp.int32, (1, n_kv), 1) % XATTN_HEADS
               == lax.broadcasted_iota(jnp.int32, (XATTN_HEADS * rows, 1), 0) // rows)
        p = _masked_softmax_rows(_dot_nt(q_st, k_all), own)
        o_st = _dot(p.astype(BF16), v_all).astype(BF16)
        return jnp.concatenate([o_st[hd * rows:(hd + 1) * rows] for hd in range(XATTN_HEADS)], axis=-1)
    outs = []
    for cols in head_cols:
        k = mk_ref[0, :, cols].astype(BF16)
        v = mv_ref[0, :, cols].astype(BF16)
        s = _dot_nt(q[:, cols], k)
        e = jnp.exp(s - jnp.max(s, axis=-1, keepdims=True))
        p = e / jnp.sum(e, axis=-1, keepdims=True)
        outs.append(_dot(p.astype(BF16), v).astype(BF16))
    return jnp.concatenate(outs, axis=-1)


def _xattn_kernel(q_ref, mk_ref, mv_ref, o_ref):
    o_ref[0] = _xattn_heads(q_ref[0], mk_ref, mv_ref)


def xattn(q, mk, mv, layer, tq):
    b, t, d = q.shape
    mem_spec = pl.BlockSpec((None, 1) + mk.shape[2:], lambda i, j: (layer, i, 0, 0, 0))
    return pl.pallas_call(
        _xattn_kernel,
        grid=(b, t // tq),
        in_specs=[pl.BlockSpec((1, tq, d), lambda i, j: (i, j, 0)), mem_spec, mem_spec],
        out_specs=pl.BlockSpec((1, tq, d), lambda i, j: (i, j, 0)),
        out_shape=jax.ShapeDtypeStruct((b, t, d), BF16),
        compiler_params=_cparams("parallel", "parallel"),
    )(q, mk, mv)


def _post_mixer_kernel(*refs, n_in):
    a_refs, w_refs = refs[:n_in], refs[n_in:2 * n_in]
    x_ref, gn_ref, mk_ref, mv_ref, wq_ref, wo_ref, wg_ref, wu_ref, wd_ref, o_ref = refs[2 * n_in:]
    gain = lambda k: gn_ref[k:k + 1, :]
    mix = _dot(a_refs[0][0], w_refs[0][...])
    for a_ref, w_ref in zip(a_refs[1:], w_refs[1:]):
        mix = mix + _dot(a_ref[0], w_ref[...])
    x1 = x_ref[0] + _rms(mix, gain(1))
    q = (_dot(_rms(x1, gain(2)).astype(BF16), wq_ref[...]) * (XATTN_HEAD_DIM ** -0.5)).astype(BF16)
    attn = _xattn_heads(q, mk_ref, mv_ref)
    x2 = x1 + _rms(_dot(attn, wo_ref[...]), gain(3))
    o_ref[0] = _ffn_block(x2, gain(4), gain(5), wg_ref, wu_ref, wd_ref)


def post_mixer(a_list, w_list, x, gn, mk, mv, wq, wo, wg, wu, wd, tm):
    b, t, d = x.shape
    n_in = len(a_list)
    resident = lambda a: pl.BlockSpec(a.shape, lambda i, j: (0,) * a.ndim, pipeline_mode=pl.Buffered(1))
    tile = lambda w: pl.BlockSpec((1, tm, w), lambda i, j: (i, j, 0))
    per_b = lambda a: pl.BlockSpec((1,) + a.shape[1:], lambda i, j: (i, 0, 0))
    return pl.pallas_call(
        functools.partial(_post_mixer_kernel, n_in=n_in),
        grid=(b, t // tm),
        in_specs=([tile(a.shape[2]) for a in a_list] + [resident(w) for w in w_list]
                  + [tile(d), resident(gn), per_b(mk), per_b(mv)]
                  + [resident(w) for w in (wq, wo, wg, wu, wd)]),
        out_specs=tile(d),
        out_shape=jax.ShapeDtypeStruct((b, t, d), F32),
        compiler_params=_cparams("parallel", "parallel"),
    )(*a_list, *w_list, x, gn, mk, mv, wq, wo, wg, wu, wd)


def _pool_kernel(u_ref, past_ref, w_ref, scale_ref, o_ref, buf, *, t, chunk, start_pos):
    buf[0:POOL_HALO, :] = past_ref[0]
    buf[POOL_HALO:POOL_HALO + t, :] = u_ref[0]
    for c in range(t // chunk):
        base = POOL_HALO + c * chunk
        pos = start_pos + c * chunk + lax.broadcasted_iota(jnp.int32, (chunk, 1), 0)
        for gi, win in enumerate(POOL_WINDOWS):
            cols = slice(gi * POOL_GROUP, (gi + 1) * POOL_GROUP)
            cur = buf[base:base + chunk, cols]
            win_sum = cur
            for k in range(1, win):
                win_sum = win_sum + buf[base - k:base - k + chunk, cols]
            cnt = jnp.minimum(pos + 1, win).astype(F32)
            diff = win_sum / cnt - cur
            y = _dot(diff.astype(BF16), w_ref[gi]) * scale_ref[:, cols]
            o_ref[0, c * chunk:(c + 1) * chunk, cols] = y.astype(o_ref.dtype)


def pool_mix(u, past, w, scale, start_pos):
    b, t, c = u.shape
    chunk = min(t, 256)
    return pl.pallas_call(
        functools.partial(_pool_kernel, t=t, chunk=chunk, start_pos=start_pos),
        grid=(b,),
        in_specs=[pl.BlockSpec((1, t, c), lambda i: (i, 0, 0)),
                  pl.BlockSpec((1, POOL_HALO, c), lambda i: (i, 0, 0)),
                  _full(w.shape), _full(scale.shape)],
        out_specs=pl.BlockSpec((1, t, c), lambda i: (i, 0, 0)),
        out_shape=jax.ShapeDtypeStruct((b, t, c), BF16),
        scratch_shapes=[pltpu.VMEM((POOL_HALO + t, c), F32)],
        compiler_params=_cparams("parallel"),
    )(u, past, w, scale)


def _compress_kernel(kc_ref, vc_ref, pool_ref, wkt_ref, wvt_ref, ident_ref, k_ref, vt_ref):
    def compress(src, wt_ref):
        x = src[0]
        hi = x.astype(BF16)
        lo = (x - hi.astype(F32)).astype(BF16)
        mean_t = _dot(hi, pool_ref[...]) + _dot(lo, pool_ref[...])
        return _dot(wt_ref[...], mean_t.astype(BF16)).astype(BF16)
    k_ref[0] = _dot_nt(ident_ref[...], compress(kc_ref, wkt_ref)).astype(BF16)
    vt_ref[0] = compress(vc_ref, wvt_ref)


def compress_kv(kc_t, vc_t, pool_mat, wk_bd_t, wv_bd_t, ident):
    b, c, t = kc_t.shape
    n_cmp = pool_mat.shape[1]
    per_b = pl.BlockSpec((1, c, t), lambda i: (i, 0, 0))
    return pl.pallas_call(
        _compress_kernel,
        grid=(b,),
        in_specs=[per_b, per_b, _full(pool_mat.shape), _full(wk_bd_t.shape), _full(wv_bd_t.shape),
                  _full(ident.shape)],
        out_specs=[pl.BlockSpec((1, n_cmp, c), lambda i: (i, 0, 0)), pl.BlockSpec((1, c, n_cmp), lambda i: (i, 0, 0))],
        out_shape=[jax.ShapeDtypeStruct((b, n_cmp, c), BF16), jax.ShapeDtypeStruct((b, c, n_cmp), BF16)],
        compiler_params=_cparams("parallel"),
    )(kc_t, vc_t, pool_mat, wk_bd_t, wv_bd_t, ident)


def _masked_softmax_rows(s, mask):
    s = jnp.where(mask, s, NEG_INF)
    e = jnp.where(mask, jnp.exp(s - jnp.max(s, axis=-1, keepdims=True)), 0.0)
    return e / jnp.maximum(jnp.sum(e, axis=-1, keepdims=True), 1e-30)


def _flash_step(raw_ref, v_aug, mask, m_ref, acc_ref):
    tq = mask.shape[1]
    bias = jnp.where(mask, 0.0, NEG_INF)
    for hh in range(HEADS_PER_GROUP):
        lanes = slice(hh * tq, (hh + 1) * tq)
        s = raw_ref[:, lanes] + bias
        m_old = m_ref[:, lanes]
        m_new = jnp.maximum(m_old, jnp.max(s, axis=0, keepdims=True))
        alpha = jnp.exp2(m_old - m_new)
        e = jnp.exp2(s - m_new)
        m_ref[:, lanes] = m_new
        acc_ref[:, lanes] = alpha * acc_ref[:, lanes] + _dot(v_aug, e.astype(BF16))


def _nsa_prompt_kernel(qt_ref, gatet_ref, kcmp_ref, vcmpt_ref, kpm_ref, vst_ref, vwt_ref,
                       mselt_ref, ident_ref, o_ref, sel_s, raw_s, m_s, acc_s, *, tq, tk):
    i = pl.program_id(1)
    q0 = i * tq
    n_cmp = kcmp_ref.shape[1]
    n_sel = mselt_ref.shape[0]
    blocks_per_tile = tk // SEL_BLOCK
    tiles_per_q = tq // tk
    qpos_row = q0 + lax.broadcasted_iota(jnp.int32, (1, tq), 1)
    key_col = lax.broadcasted_iota(jnp.int32, (tk, 1), 0)
    cmp_end_col = lax.broadcasted_iota(jnp.int32, (n_cmp, 1), 0) * CMP_STRIDE + (CMP_BLOCK - 1)
    cmp_bias = jnp.where(cmp_end_col <= qpos_row, 0.0, NEG_INF)
    sel_j = lax.broadcasted_iota(jnp.int32, (n_sel, 1), 0)
    cur_row = qpos_row // SEL_BLOCK
    gate_t = gatet_ref[0]
    zero_half = jnp.zeros((HEAD_DIM, HEADS_PER_GROUP * tq), BF16)

    groups = range(NSA_KV_HEADS)
    grows = [slice(g * HEAD_DIM, (g + 1) * HEAD_DIM) for g in groups]
    q_pads, o_cmps = [], []
    for g in groups:
        q_t_g = jnp.concatenate(
            [qt_ref[0, (g * HEADS_PER_GROUP + hh) * HEAD_DIM:(g * HEADS_PER_GROUP + hh + 1) * HEAD_DIM, :]
             for hh in range(HEADS_PER_GROUP)], axis=1)
        q_pad = jnp.concatenate([q_t_g, zero_half] if g == 0 else [zero_half, q_t_g], axis=0)
        q_pads.append(q_pad)

        s_t = _dot(kcmp_ref[0], q_pad) + jnp.concatenate([cmp_bias] * HEADS_PER_GROUP, axis=1)
        e_t = jnp.exp2(s_t - jnp.maximum(jnp.max(s_t, axis=0, keepdims=True), M_INIT))
        p_t = e_t / jnp.maximum(jnp.sum(e_t, axis=0, keepdims=True), 1e-30)
        o_cmps.append(_dot(vcmpt_ref[0, grows[g], :], p_t.astype(BF16)))
        psum_t = p_t[:, 0:tq]
        for hh in range(1, HEADS_PER_GROUP):
            psum_t = psum_t + p_t[:, hh * tq:(hh + 1) * tq]

        imp_t = None
        for part in _split3(psum_t):
            d = _dot(mselt_ref[...], part)
            imp_t = d if imp_t is None else imp_t + d
        valid = sel_j * SEL_BLOCK <= qpos_row
        forced = (sel_j == 0) | (sel_j == cur_row) | (sel_j == cur_row - 1)
        score = jnp.where(forced, FORCE_SCORE, jnp.where(valid, imp_t, NEG_INF))
        rank = jnp.zeros((n_sel, tq), F32)
        for jp in range(n_sel):
            sj = score[jp:jp + 1, :]
            beats = (sj > score) | ((sj == score) & (sel_j > jp))
            rank = rank + jnp.where(beats, 1.0, 0.0)
        sel_s[g] = jnp.where(rank < SEL_TOPK, 1.0, 0.0)

    m_s[...] = jnp.full(m_s.shape, M_INIT, F32)
    acc_s[...] = jnp.zeros(acc_s.shape, F32)
    ones_rows = jnp.ones((DENOM_ROWS, tk), BF16)
    with_ones = lambda v_t: jnp.concatenate([v_t.astype(BF16), ones_rows], axis=0)
    n_tiles = (i + 1) * tiles_per_q
    first_win = jnp.maximum((i * tq - (WINDOW - 1)) // tk, 0)
    key_rows = lambda kt: pl.ds(pl.multiple_of(kt * tk, tk), tk)

    def issue_scores(kt, slot, branch):
        k = kpm_ref[0, key_rows(kt), branch * KV_COLS:(branch + 1) * KV_COLS]
        for g in groups:
            raw_s[slot, NSA_KV_HEADS * branch + g] = _dot(k, q_pads[g])

    def slc_update(kt, slot):
        causal = kt * tk + key_col <= qpos_row
        for g in groups:
            chosen = jnp.concatenate(
                [jnp.broadcast_to(sel_s[g, pl.ds(kt * blocks_per_tile + r, 1), :], (SEL_BLOCK, tq))
                 for r in range(blocks_per_tile)], axis=0)
            v_aug = with_ones(vst_ref[0, grows[g], key_rows(kt)])
            _flash_step(raw_s.at[slot, g], v_aug, (chosen > 0.5) & causal, m_s.at[g], acc_s.at[g])

    def win_update(kt, slot):
        dist = qpos_row - (kt * tk + key_col)
        mask = (dist >= 0) & (dist < WINDOW)
        for g in groups:
            c = NSA_KV_HEADS + g
            v_aug = with_ones(vwt_ref[0, grows[g], key_rows(kt)])
            _flash_step(raw_s.at[slot, c], v_aug, mask, m_s.at[c], acc_s.at[c])

    def far_body(pair, _):
        for slot in (0, 1):
            kt = 2 * pair + slot
            issue_scores(kt + 1, 1 - slot, 0)
            slc_update(kt, slot)
        return 0

    def band_body(pair, _):
        for slot in (0, 1):
            kt = 2 * pair + slot
            nxt = jnp.minimum(kt + 1, n_tiles - 1)
            issue_scores(nxt, 1 - slot, 0)
            issue_scores(nxt, 1 - slot, 1)
            slc_update(kt, slot)
            win_update(kt, slot)
        return 0

    issue_scores(0, 0, 0)
    lax.fori_loop(0, first_win // 2, far_body, 0)
    issue_scores(first_win, 0, 1)
    lax.fori_loop(first_win // 2, n_tiles // 2, band_body, 0)

    head_out = [None] * NSA_HEADS
    for g in groups:
        normalised = lambda c: acc_s[c, 0:HEAD_DIM, :] / acc_s[c, HEAD_DIM:HEAD_DIM + 1, :]
        o_slc = normalised(g)
        o_win = normalised(NSA_KV_HEADS + g)
        for hh in range(HEADS_PER_GROUP):
            h = g * HEADS_PER_GROUP + hh
            lanes = slice(hh * tq, (hh + 1) * tq)
            head_out[h] = (gate_t[3 * h:3 * h + 1, :] * o_cmps[g][:, lanes]
                           + gate_t[3 * h + 1:3 * h + 2, :] * o_slc[:, lanes]
                           + gate_t[3 * h + 2:3 * h + 3, :] * o_win[:, lanes])
    o_t = jnp.concatenate(head_out, axis=0).astype(BF16)
    o_ref[0] = _dot_nt(ident_ref[...], o_t).astype(o_ref.dtype)


def nsa_prompt(q_t, gate_t, kcmp, vcmp_t, kpm, vs_t, vw_t, mselt, ident, tq, tk):
    b, _, t = q_t.shape
    feat_tile = lambda f: pl.BlockSpec((1, f, tq), lambda i, j: (i, 0, j))
    whole = lambda a: pl.BlockSpec((1,) + a.shape[1:], lambda i, j: (i, 0, 0))
    per_batch = [kcmp, vcmp_t, kpm, vs_t, vw_t]
    n_chain = 2 * NSA_KV_HEADS
    assert tq % (2 * tk) == 0 and WINDOW % (2 * tk) == 0 and tk % SEL_BLOCK == 0
    return pl.pallas_call(
        functools.partial(_nsa_prompt_kernel, tq=tq, tk=tk),
        grid=(b, t // tq),
        in_specs=([feat_tile(NSA_WIDTH), feat_tile(LANE)] + [whole(a) for a in per_batch]
                  + [_full(mselt.shape), _full(ident.shape)]),
        out_specs=pl.BlockSpec((1, tq, NSA_WIDTH), lambda i, j: (i, j, 0)),
        out_shape=jax.ShapeDtypeStruct((b, t, NSA_WIDTH), BF16),
        scratch_shapes=[pltpu.VMEM((NSA_KV_HEADS, mselt.shape[0], tq), F32),
                        pltpu.VMEM((2, n_chain, tk, HEADS_PER_GROUP * tq), F32),
                        pltpu.VMEM((n_chain, 1, HEADS_PER_GROUP * tq), F32),
                        pltpu.VMEM((n_chain, HEAD_DIM + DENOM_ROWS, HEADS_PER_GROUP * tq), F32)],
        compiler_params=_cparams("parallel", "arbitrary"),
    )(q_t, gate_t, *per_batch, mselt, ident)


def _stack_heads(q, g):
    return jnp.concatenate(
        [q[:, (g * HEADS_PER_GROUP + hh) * HEAD_DIM:(g * HEADS_PER_GROUP + hh + 1) * HEAD_DIM]
         for hh in range(HEADS_PER_GROUP)], axis=0)


def _sample_cmp_kernel(pt_ref, k_hbm, v_hbm, kn_ref, vn_ref, q_ref, wkt_ref, wvt_ref, subpool_ref, ident_ref,
                       ocmp_ref, sel_ref, sub_k, sub_v, kbuf, vbuf, sems, *, n_valid, past_len, layer, n_layers):
    p = PAGES_PER_STEP
    tp = SAMPLE_T_PAD
    c = pl.program_id(1)
    n_sub = past_len // CMP_STRIDE
    steps = past_len // (p * PAGE_SIZE)

    pages = _fetch_pages(pt_ref, k_hbm, v_hbm, kbuf, vbuf, sems, layer, n_layers)
    for x, sub in zip(pages, (sub_k, sub_v)):
        hi = x.astype(BF16)
        lo = (x - hi.astype(F32)).astype(BF16)
        sub[c] = _dot(hi, subpool_ref[...]) + _dot(lo, subpool_ref[...])

    @pl.when(c == steps - 1)
    def _():
        t_col = lax.broadcasted_iota(jnp.int32, (tp, 1), 0)
        new_valid = lax.broadcasted_iota(jnp.int32, (1, tp), 1) < n_valid
        first_lane = lax.broadcasted_iota(jnp.int32, (1, SUB_PER_STEP), 1) == 0
        cmp_kv = []
        for new_ref, sub, wt_ref in ((kn_ref, sub_k, wkt_ref), (vn_ref, sub_v, wvt_ref)):
            new_t = None
            for part in _split3(new_ref[0]):
                d = _dot_nt(ident_ref[...], part)
                new_t = d if new_t is None else new_t + d
            new_sum = jnp.sum(jnp.where(new_valid, new_t, 0.0), axis=1, keepdims=True)
            sub[steps] = jnp.where(first_lane, new_sum, 0.0)
            sums = jnp.concatenate([sub[s] for s in range(steps + 1)], axis=1)
            mean_t = (sums[:, 0:n_sub] + sums[:, 1:n_sub + 1]) * (1.0 / CMP_BLOCK)
            cmp_kv.append(_dot(wt_ref[...], mean_t.astype(BF16)).astype(BF16))
        kcmp_t, vcmp_t = cmp_kv

        rows = HEADS_PER_GROUP * tp
        qpos_col = past_len + lax.broadcasted_iota(jnp.int32, (rows, 1), 0) % tp
        cmp_end_row = lax.broadcasted_iota(jnp.int32, (1, n_sub), 1) * CMP_STRIDE + (CMP_BLOCK - 1)
        cmp_mask = cmp_end_row <= qpos_col

        n_lane = sel_ref.shape[2]
        n_sel = -(-(past_len + n_valid) // SEL_BLOCK)
        lane = lax.broadcasted_iota(jnp.int32, (1, n_lane), 1)
        sel_j = lane // SEL_PER_CMP
        lane_used = (lane % SEL_PER_CMP == 0) & (sel_j < n_sel)
        qpos_t = past_len + t_col
        cur = qpos_t // SEL_BLOCK
        valid = lane_used & (sel_j * SEL_BLOCK <= qpos_t)
        forced = lane_used & ((sel_j == 0) | (sel_j == cur) | (sel_j == cur - 1))
        lane_pad = jnp.zeros((tp, n_lane - n_sub), F32)

        q = q_ref[0]
        o_heads = [None] * NSA_HEADS
        scores = []
        for g in range(NSA_KV_HEADS):
            gcols = slice(g * HEAD_DIM, (g + 1) * HEAD_DIM)
            prob = _masked_softmax_rows(_dot(_stack_heads(q, g), kcmp_t[gcols, :]), cmp_mask)
            o_g = _dot_nt(prob.astype(BF16), vcmp_t[gcols, :])
            psum = prob[0:tp]
            for hh in range(1, HEADS_PER_GROUP):
                psum = psum + prob[hh * tp:(hh + 1) * tp]
                o_heads[g * HEADS_PER_GROUP + hh] = o_g[hh * tp:(hh + 1) * tp]
            o_heads[g * HEADS_PER_GROUP] = o_g[0:tp]
            pz = jnp.concatenate([psum, lane_pad], axis=1)
            imp = pltpu.roll(pz, 1, 1) + pz
            for d in range(1, SEL_PER_CMP):
                imp = imp + pltpu.roll(pz, n_lane - d, 1)
            scores.append(jnp.where(forced, FORCE_SCORE, jnp.where(valid, imp, NEG_INF)))
        sel_ref[0] = jnp.concatenate(scores, axis=0)
        ocmp_ref[0] = jnp.concatenate(o_heads, axis=-1)


def _fetch_pages(pt_ref, k_hbm, v_hbm, kbuf, vbuf, sems, layer, n_layers):
    b, c = pl.program_id(0), pl.program_id(1)
    n_b, steps = pl.num_programs(0), pl.num_programs(1)
    step = b * steps + c
    slot = step % 2

    def copies(bb, cc, to_slot, for_wait=False):
        out = []
        for k in range(PAGES_PER_STEP):
            page = 0 if for_wait else pt_ref[bb, cc * PAGES_PER_STEP + k] * n_layers + layer
            out.append(pltpu.make_async_copy(k_hbm.at[page], kbuf.at[to_slot, k], sems.at[to_slot, 0]))
            out.append(pltpu.make_async_copy(v_hbm.at[page], vbuf.at[to_slot, k], sems.at[to_slot, 1]))
        return out

    @pl.when(step == 0)
    def _():
        for cp in copies(0, 0, 0):
            cp.start()

    @pl.when(step + 1 < n_b * steps)
    def _():
        wrap = c + 1 == steps
        for cp in copies(jnp.where(wrap, b + 1, b), jnp.where(wrap, 0, c + 1), 1 - slot):
            cp.start()

    for cp in copies(b, c, slot, for_wait=True):
        cp.wait()
    k_all = jnp.concatenate([kbuf[slot, k] for k in range(PAGES_PER_STEP)], axis=1)
    v_all = jnp.concatenate([vbuf[slot, k] for k in range(PAGES_PER_STEP)], axis=1)
    return k_all, v_all


_PAGE_SCRATCH = [pltpu.VMEM((2, PAGES_PER_STEP, KV_COLS, PAGE_SIZE), F32),
                 pltpu.VMEM((2, PAGES_PER_STEP, KV_COLS, PAGE_SIZE), F32),
                 pltpu.SemaphoreType.DMA((2, 2))]


def sample_cmp(page_table, cache_k, cache_v, kc_new, vc_new, q, wk_bd_t, wv_bd_t, subpool, ident,
               layer, n_layers, n_valid):
    b, n_pages = page_table.shape
    past_len = n_pages * PAGE_SIZE
    steps = n_pages // PAGES_PER_STEP
    assert SUB_PER_STEP == LANE
    n_lane = (steps + 1) * LANE
    per_b = lambda a: pl.BlockSpec((1,) + a.shape[1:], lambda i, c, pt: (i, 0, 0))
    const = lambda a: pl.BlockSpec(a.shape, lambda i, c, pt: (0,) * a.ndim)
    consts = [wk_bd_t, wv_bd_t, subpool, ident]
    hbm = pl.BlockSpec(memory_space=pl.ANY)
    grid_spec = pltpu.PrefetchScalarGridSpec(
        num_scalar_prefetch=1,
        grid=(b, steps),
        in_specs=[hbm, hbm, per_b(kc_new), per_b(vc_new), per_b(q)] + [const(a) for a in consts],
        out_specs=[pl.BlockSpec((1, SAMPLE_T_PAD, NSA_WIDTH), lambda i, c, pt: (i, 0, 0)),
                   pl.BlockSpec((1, 2 * SAMPLE_T_PAD, n_lane), lambda i, c, pt: (i, 0, 0))],
        scratch_shapes=[pltpu.VMEM((steps + 1, KV_COLS, SUB_PER_STEP), F32)] * 2 + _PAGE_SCRATCH,
    )
    return pl.pallas_call(
        functools.partial(_sample_cmp_kernel, n_valid=n_valid, past_len=past_len, layer=layer,
                          n_layers=n_layers),
        grid_spec=grid_spec,
        out_shape=[jax.ShapeDtypeStruct((b, SAMPLE_T_PAD, NSA_WIDTH), F32),
                   jax.ShapeDtypeStruct((b, 2 * SAMPLE_T_PAD, n_lane), F32)],
        compiler_params=_cparams("arbitrary", "arbitrary"),
    )(page_table, cache_k, cache_v, kc_new, vc_new, q, *consts)


def _topk_kernel(score_ref, o_ref):
    score = score_ref[...]
    n_lane = score.shape[-1]
    lane = lax.broadcasted_iota(jnp.int32, (1, n_lane), 1)
    chosen = jnp.zeros(score.shape, F32)
    for _ in range(SEL_TOPK):
        best = jnp.max(score, axis=-1, keepdims=True)
        first = jnp.min(jnp.where(score == best, lane, n_lane), axis=-1, keepdims=True)
        hit = lane == first
        chosen = jnp.where(hit, 1.0, chosen)
        score = jnp.where(hit, REMOVED_SCORE, score)
    o_ref[...] = chosen


def topk_mask(score, rows_per_step):
    r, n_lane = score.shape
    spec = pl.BlockSpec((rows_per_step, n_lane), lambda i: (i, 0))
    return pl.pallas_call(
        _topk_kernel,
        grid=(r // rows_per_step,),
        in_specs=[spec],
        out_specs=spec,
        out_shape=jax.ShapeDtypeStruct((r, n_lane), F32),
        compiler_params=_cparams("parallel"),
    )(score)


def _sample_slc_kernel(pt_ref, k_hbm, v_hbm, q_ref, sel_ref, expand_ref, ksn_ref, vsn_ref, wk_ref, wv_ref,
                       kwn_ref, vwn_ref, ocmp_ref, gate_ref, o_ref, m_s, l_s, acc_s, kbuf, vbuf, sems,
                       *, n_valid, past_len, layer, n_layers):
    tp = SAMPLE_T_PAD
    rows = HEADS_PER_GROUP * tp
    c = pl.program_id(1)
    q = q_ref[0]
    zero = jnp.zeros((rows, HEAD_DIM), BF16)
    q_bd = jnp.concatenate([jnp.concatenate([_stack_heads(q, 0), zero], axis=1),
                            jnp.concatenate([zero, _stack_heads(q, 1)], axis=1)], axis=0)
    t_col = lax.broadcasted_iota(jnp.int32, (NSA_KV_HEADS * rows, 1), 0) % tp
    new_row = lax.broadcasted_iota(jnp.int32, (1, tp), 1)
    new_mask = (new_row <= t_col) & (new_row < n_valid)

    @pl.when(c == 0)
    def _():
        m_s[...] = jnp.full(m_s.shape, NEG_INF, F32)
        l_s[...] = jnp.zeros(l_s.shape, F32)
        acc_s[...] = jnp.zeros(acc_s.shape, F32)

    k_all, v_all = _fetch_pages(pt_ref, k_hbm, v_hbm, kbuf, vbuf, sems, layer, n_layers)
    k_all_t, v_all_t = k_all.astype(BF16), v_all.astype(BF16)
    chosen = _dot(sel_ref[0].astype(BF16), expand_ref[...])
    keymask = jnp.concatenate([chosen[0:tp]] * HEADS_PER_GROUP + [chosen[tp:2 * tp]] * HEADS_PER_GROUP,
                              axis=0) > 0.5
    s = jnp.where(keymask, _dot(q_bd, k_all_t), NEG_INF)
    m_old = m_s[...]
    m_new = jnp.maximum(m_old, jnp.max(s, axis=-1, keepdims=True))
    alpha = jnp.exp(m_old - m_new)
    e = jnp.where(keymask, jnp.exp(s - m_new), 0.0)
    l_s[...] = alpha * l_s[...] + jnp.sum(e, axis=-1, keepdims=True)
    acc_s[...] = alpha * acc_s[...] + _dot_nt(e.astype(BF16), v_all_t)
    m_s[...] = m_new

    @pl.when(c == pl.num_programs(1) - 1)
    def _():
        n_win = wk_ref.shape[2]
        win_row = lax.broadcasted_iota(jnp.int32, (1, n_win), 1)
        dist = t_col + n_win - win_row
        win_mask = (dist >= 0) & (dist < WINDOW)
        s_n = jnp.where(new_mask, _dot_nt(q_bd, ksn_ref[0].astype(BF16)), NEG_INF)
        m_fin = jnp.maximum(m_new, jnp.max(s_n, axis=-1, keepdims=True))
        a_fin = jnp.exp(m_new - m_fin)
        e_n = jnp.where(new_mask, jnp.exp(s_n - m_fin), 0.0)
        l_fin = a_fin * l_s[...] + jnp.sum(e_n, axis=-1, keepdims=True)
        o_slc = (a_fin * acc_s[...] + _dot(e_n.astype(BF16), vsn_ref[0].astype(BF16))) / l_fin
        s_p = jnp.where(win_mask, _dot(q_bd, wk_ref[0].astype(BF16)), NEG_INF)
        s_w = jnp.where(new_mask, _dot_nt(q_bd, kwn_ref[0].astype(BF16)), NEG_INF)
        m_w = jnp.maximum(jnp.max(s_p, axis=-1, keepdims=True), jnp.max(s_w, axis=-1, keepdims=True))
        e_p = jnp.where(win_mask, jnp.exp(s_p - m_w), 0.0)
        e_w = jnp.where(new_mask, jnp.exp(s_w - m_w), 0.0)
        l_w = jnp.sum(e_p, axis=-1, keepdims=True) + jnp.sum(e_w, axis=-1, keepdims=True)
        o_win = (_dot_nt(e_p.astype(BF16), wv_ref[0].astype(BF16))
                 + _dot(e_w.astype(BF16), vwn_ref[0].astype(BF16))) / l_w
        gate_all = gate_ref[0]
        head_out = []
        for h in range(NSA_HEADS):
            g, hh = divmod(h, HEADS_PER_GROUP)
            r = slice(g * rows + hh * tp, g * rows + (hh + 1) * tp)
            cols = slice(g * HEAD_DIM, (g + 1) * HEAD_DIM)
            gate = gate_all[:, 3 * h:3 * h + 3]
            head_out.append(gate[:, 0:1] * ocmp_ref[0, :, h * HEAD_DIM:(h + 1) * HEAD_DIM]
                            + gate[:, 1:2] * o_slc[r, cols] + gate[:, 2:3] * o_win[r, cols])
        o_ref[0] = jnp.concatenate(head_out, axis=-1).astype(o_ref.dtype)


def sample_slc(page_table, cache_k, cache_v, q, sel, expand, ks_new, vs_new, win_k, win_v, kw_new, vw_new,
               o_cmp, gate, layer, n_layers, n_valid):
    b, n_pages = page_table.shape
    past_len = n_pages * PAGE_SIZE
    steps = n_pages // PAGES_PER_STEP
    rows = HEADS_PER_GROUP * SAMPLE_T_PAD
    per_b = lambda a: pl.BlockSpec((1,) + a.shape[1:], lambda i, c, pt: (i, 0, 0))
    const = lambda a: pl.BlockSpec(a.shape, lambda i, c, pt: (0,) * a.ndim)
    hbm = pl.BlockSpec(memory_space=pl.ANY)
    grid_spec = pltpu.PrefetchScalarGridSpec(
        num_scalar_prefetch=1,
        grid=(b, steps),
        in_specs=[hbm, hbm, per_b(q), pl.BlockSpec((1, 2 * SAMPLE_T_PAD, LANE), lambda i, c, pt: (i, 0, c)),
                  const(expand), per_b(ks_new), per_b(vs_new), per_b(win_k), per_b(win_v),
                  per_b(kw_new), per_b(vw_new), per_b(o_cmp), per_b(gate)],
        out_specs=pl.BlockSpec((1, SAMPLE_T_PAD, NSA_WIDTH), lambda i, c, pt: (i, 0, 0)),
        scratch_shapes=[pltpu.VMEM((NSA_KV_HEADS * rows, 1), F32), pltpu.VMEM((NSA_KV_HEADS * rows, 1), F32),
                        pltpu.VMEM((NSA_KV_HEADS * rows, KV_COLS), F32)] + _PAGE_SCRATCH,
    )
    return pl.pallas_call(
        functools.partial(_sample_slc_kernel, n_valid=n_valid, past_len=past_len, layer=layer,
                          n_layers=n_layers),
        grid_spec=grid_spec,
        out_shape=jax.ShapeDtypeStruct((b, SAMPLE_T_PAD, NSA_WIDTH), BF16),
        compiler_params=_cparams("arbitrary", "arbitrary"),
    )(page_table, cache_k, cache_v, q, sel, expand, ks_new, vs_new, win_k, win_v, kw_new, vw_new, o_cmp, gate)


def _lru_kernel(*refs, tt, first_pos_zero, fused_in_proj):
    n_lead = 3 if fused_in_proj else 2
    (cpast_ref, h0_ref, cw_ref, cb_ref, wa_ref, wx_ref, ba_ref, bx_ref, lam_ref,
     o_ref, tail_ref, xtail_ref, xbuf, h_carry, a_s, d_s) = refs[n_lead:]
    j = pl.program_id(1)

    @pl.when(j == 0)
    def _():
        xbuf[...] = cpast_ref[0]
        h_carry[...] = h0_ref[0]

    sub = lax.broadcasted_iota(jnp.int32, (8, LRU_WIDTH), 0)
    if fused_in_proj:
        x_ref, g_ref, win_ref = refs[:n_lead]
        h_in = _rms(x_ref[0], g_ref[...]).astype(BF16)
        x_cur = _dot(h_in, win_ref[:, 0:LRU_WIDTH])
        gy = jax.nn.gelu(_dot(h_in, win_ref[:, LRU_WIDTH:]))
    else:
        x_cur = refs[0][0]
        gy = refs[1][0]
    prev8 = xbuf[...]

    def delayed(k):
        if k == 0:
            return x_cur
        rolled = pltpu.roll(x_cur, k, 0)
        head = jnp.where(sub < k, pltpu.roll(prev8, k, 0), rolled[0:8])
        return jnp.concatenate([head, rolled[8:]], axis=0) if tt > 8 else head

    xc = delayed(CONV_WIDTH - 1) * cw_ref[0:1, :]
    for k in range(1, CONV_WIDTH):
        xc = xc + delayed(CONV_WIDTH - 1 - k) * cw_ref[k:k + 1, :]
    xf = xc + cb_ref[...]
    xf_b = xf.astype(BF16)
    r_parts, i_parts = [], []
    for n in range(LRU_BLOCKS):
        cols = slice(n * LRU_BLOCK, (n + 1) * LRU_BLOCK)
        r_parts.append(_dot(xf_b[:, cols], wa_ref[n]))
        i_parts.append(_dot(xf_b[:, cols], wx_ref[n]))
    sigmoid = lambda v: 0.5 * jnp.tanh(0.5 * v) + 0.5
    r = sigmoid(jnp.concatenate(r_parts, axis=-1) + ba_ref[...])
    gate_i = sigmoid(jnp.concatenate(i_parts, axis=-1) + bx_ref[...])
    neg_lam = -lam_ref[...]
    softplus = jnp.maximum(neg_lam, 0.0) + jnp.log1p(jnp.exp(-jnp.abs(neg_lam)))
    log_a = r * (-LRU_C * softplus)
    a = jnp.exp(log_a)
    gap = -jnp.tanh(log_a) * (a * a + 1.0)
    mult = jnp.where(gap > 0.0, gap * lax.rsqrt(gap), 0.0)
    if first_pos_zero:
        row = lax.broadcasted_iota(jnp.int32, (tt, 1), 0)
        mult = jnp.where((row == 0) & (j == 0), 1.0, mult)
    a_s[...] = a
    d_s[...] = mult * gate_i * xf

    def body(blk, h):
        rows = pl.ds(pl.multiple_of(blk * 8, 8), 8)
        a8 = a_s[rows, :]
        d8 = d_s[rows, :]
        for s in (1, 2, 4):
            keep = sub >= s
            d8 = jnp.where(keep, a8 * pltpu.roll(d8, s, 0) + d8, d8)
            a8 = jnp.where(keep, a8 * pltpu.roll(a8, s, 0), a8)
        hs = a8 * h + d8
        d_s[rows, :] = hs
        return hs[7:8, :]

    n_blk = tt // 8
    h_carry[...] = lax.fori_loop(0, n_blk, body, h_carry[...], unroll=min(4, n_blk))
    hs = d_s[...]
    o_ref[0] = (hs * gy).astype(o_ref.dtype)
    tail_ref[0] = d_s[tt - 8:tt, :]
    xtail_ref[0] = x_cur[tt - 8:tt, :]
    xbuf[...] = x_cur[tt - 8:tt, :]


def lru(lead, conv_past, h0, cw, cb, wa, wx, ba, bx, lam, tt, first_pos_zero, fused_in_proj):
    b, t, _ = lead[0].shape
    w = LRU_WIDTH
    tile = lambda a: pl.BlockSpec((1, tt, a.shape[2]), lambda i, j: (i, j, 0))
    per_b = lambda a: pl.BlockSpec((1,) + a.shape[1:], lambda i, j: (i, 0, 0))
    last8 = pl.BlockSpec((1, 8, w), lambda i, j: (i, 0, 0))
    consts = [cw, cb, wa, wx, ba, bx, lam]
    lead_specs = ([tile(lead[0]), _full(lead[1].shape), _full(lead[2].shape)] if fused_in_proj
                  else [tile(lead[0]), tile(lead[1])])
    return pl.pallas_call(
        functools.partial(_lru_kernel, tt=tt, first_pos_zero=first_pos_zero, fused_in_proj=fused_in_proj),
        grid=(b, t // tt),
        in_specs=lead_specs + [per_b(conv_past), per_b(h0)] + [_full(a.shape) for a in consts],
        out_specs=[pl.BlockSpec((1, tt, w), lambda i, j: (i, j, 0)), last8, last8],
        out_shape=[jax.ShapeDtypeStruct((b, t, w), BF16), jax.ShapeDtypeStruct((b, 8, w), F32),
                   jax.ShapeDtypeStruct((b, 8, w), F32)],
        scratch_shapes=[pltpu.VMEM((8, w), F32), pltpu.VMEM((1, w), F32),
                        pltpu.VMEM((tt, w), F32), pltpu.VMEM((tt, w), F32)],
        compiler_params=_cparams("parallel", "arbitrary"),
    )(*lead, conv_past, h0, *consts)


def _block_mean_matrix(t):
    n_cmp = t // CMP_STRIDE - CMP_BLOCK // CMP_STRIDE + 1
    m = np.zeros((t // CMP_STRIDE, t), np.float32)
    for i in range(n_cmp):
        m[i, i * CMP_STRIDE:i * CMP_STRIDE + CMP_BLOCK] = 1.0 / CMP_BLOCK
    return m


def _sel_from_cmp(n_cmp_rows, n_cmp, n_sel):
    m = np.zeros((n_cmp_rows, n_sel), np.float32)
    for j in range(n_sel):
        for c in range(4 * j - 1, 4 * j + 4):
            if 0 <= c < n_cmp:
                m[c, j] = 1.0
    return m


def _block_diag(w):
    z = jnp.zeros((HEAD_DIM, HEAD_DIM), w.dtype)
    return jnp.concatenate([jnp.concatenate([w[0], z], axis=1), jnp.concatenate([z, w[1]], axis=1)], axis=0)


def _sigmoid(v):
    return jax.nn.sigmoid(v)


def _scale_q(v):
    return v * (HEAD_DIM ** -0.5)


_EVEN_GROUPS = (
    (0, POOL_WIDTH, ((0, POOL_WIDTH, None),)),
    (POOL_WIDTH, NSA_WIDTH, ((0, NSA_WIDTH, _scale_q),)),
    (POOL_WIDTH + NSA_WIDTH, 6 * KV_COLS, tuple((k * KV_COLS, KV_COLS, None) for k in range(6))),
    (POOL_WIDTH + NSA_WIDTH + 6 * KV_COLS, LANE, ((0, LANE, _sigmoid),)),
)
_EVEN_DTYPES = (F32, BF16) + (F32,) * 6 + (F32,)


def _even_in_proj(x2d, g, w_in_pad, tm):
    return norm_matmul(x2d, g, w_in_pad, _EVEN_GROUPS, _EVEN_DTYPES, tm)


def _xattn_block(x2d, b, t, mk, mv, layer, g_pre, g_post, wq, wo, tm, tq):
    q, = norm_matmul(x2d, g_pre, wq, ((0, D_MODEL, ((0, D_MODEL, lambda v: v * (XATTN_HEAD_DIM ** -0.5)),)),),
                     (BF16,), tm)
    o = xattn(q.reshape(b, t, D_MODEL), mk, mv, layer, tq).reshape(b * t, D_MODEL)
    return proj_norm_res([o], [wo], g_post, x2d, tm)


def kernel(x_prompt, mem_prompt, x_sample, state_pool, cache_cmp_k, cache_cmp_v, cache_slc_k, cache_slc_v, cache_win_k, cache_win_v, state_lru_h, state_lru_conv, cache_mem_k, cache_mem_v, page_table, norm_gain, mem_norm_gain, w_in_even, pool_w, pool_scale, w_cmp_k, w_cmp_v, w_out_even, w_in_odd, conv_w, conv_b, lru_wa, lru_ba, lru_wx, lru_bx, lru_lambda, w_out_odd, w_xq, w_xk, w_xv, w_xo, w_ffn_gate, w_ffn_up, w_ffn_down):
    bp, tp, d = x_prompt.shape
    bs, ts, _ = x_sample.shape
    depth = norm_gain.shape[0]
    n_even = w_in_even.shape[0]
    n_pages = page_table.shape[1]
    past_len = n_pages * PAGE_SIZE
    n_phys = cache_cmp_k.shape[0]
    tsp = SAMPLE_T_PAD
    np_tok, ns_tok = bp * tp, bs * tsp
    tm_p, tm_s = 512, ns_tok
    tq, tk = 256, 128
    tt_lru = min(512, tp)

    xp = x_prompt.reshape(np_tok, d)
    xs = jnp.pad(x_sample, ((0, 0), (0, tsp - ts), (0, 0))).reshape(ns_tok, d)

    n_sub_p = tp // CMP_STRIDE
    n_cmp_p = n_sub_p - 1
    n_sel_p = tp // SEL_BLOCK
    pool_mat = jnp.asarray(_block_mean_matrix(tp).T, BF16)
    mselt = jnp.asarray(_sel_from_cmp(n_sub_p, n_cmp_p, n_sel_p).T, BF16)
    ident = jnp.asarray(np.eye(tq, dtype=np.float32), BF16)
    ident_kv = jnp.asarray(np.eye(KV_COLS, dtype=np.float32), BF16)
    subpool = jnp.asarray(np.repeat(np.eye(SUB_PER_STEP, dtype=np.float32), CMP_STRIDE, axis=0), BF16)
    expand_np = np.zeros((LANE, PAGES_PER_STEP * PAGE_SIZE), np.float32)
    expand_np[::SEL_PER_CMP] = np.repeat(np.eye(SEL_PER_STEP, dtype=np.float32), SEL_BLOCK, axis=1)
    expand_s = jnp.asarray(expand_np, BF16)

    row = lambda v: v.reshape(1, -1)
    out_ev_p, out_ev_s, out_lru_p, out_lru_s = [], [], [], []

    mem_k_all, mem_v_all = memory_kv(mem_prompt.reshape(bp * N_MEM, d), mem_norm_gain[:, None, :],
                                     jnp.concatenate([w_xk, w_xv], axis=2).astype(BF16), min(tm_p, bp * N_MEM))

    for li in range(depth):
        gn = norm_gain[li]
        if li % 2 == 0:
            e = li // 2
            w_in = jnp.pad(w_in_even[e], ((0, 0), (0, EVEN_IN_PAD - EVEN_IN))).astype(BF16)
            pw = pool_w[e].astype(BF16)
            ps = row(pool_scale[e])
            wk_bd_t = _block_diag(w_cmp_k[e]).T.astype(BF16)
            wv_bd_t = _block_diag(w_cmp_v[e]).T.astype(BF16)
            w_out = w_out_even[e].astype(BF16)
            kv0 = POOL_WIDTH + NSA_WIDTH
            kcol = lambda k: slice(kv0 + k * KV_COLS, kv0 + (k + 1) * KV_COLS)
            w_u = w_in[:, :POOL_WIDTH]
            w_kpm = jnp.concatenate([w_in[:, kcol(2)], w_in[:, kcol(4)]], axis=1)
            w_t = w_in[:, POOL_WIDTH:kv0 + 7 * KV_COLS].T

            u, kpm, q_t, gate_t, kc, vc, ks, vs, kw, vw = even_in_prompt(xp, row(gn[0]), w_u, w_kpm, w_t,
                                                                         bp, tp, tm_p)
            u3 = u.reshape(bp, tp, POOL_WIDTH)
            a_out = pool_mix(u3, jnp.zeros((bp, POOL_HALO, POOL_WIDTH), F32), pw, ps, 0)
            kcmp, vcmp_t = compress_kv(kc, vc, pool_mat, wk_bd_t, wv_bd_t, ident_kv)
            o_nsa = nsa_prompt(q_t, gate_t, kcmp, vcmp_t, kpm.reshape(bp, tp, 2 * KV_COLS), vs, vw,
                               mselt, ident, tq, tk)
            mix_p = ([a_out, o_nsa], [w_out[:POOL_WIDTH], w_out[POOL_WIDTH:]])
            kv5 = lambda a: a.reshape(bp, NSA_KV_HEADS, HEAD_DIM, a.shape[-1]).transpose(0, 3, 1, 2)
            n_keep = min(WINDOW, tp)
            out_ev_p.append((u3[:, -POOL_STATE:], kv5(kc), kv5(vc), kv5(ks), kv5(vs),
                             kv5(kw[:, :, -n_keep:]), kv5(vw[:, :, -n_keep:])))

            u, q, kc, vc, ks, vs, kw, vw, gate = _even_in_proj(xs, row(gn[0]), w_in, tm_s)
            as3 = lambda a: a.reshape(bs, tsp, a.shape[-1])
            u3 = as3(u)
            past_pool = jnp.pad(state_pool[e], ((0, 0), (POOL_HALO - POOL_STATE, 0), (0, 0)))
            a_out = pool_mix(u3, past_pool, pw, ps, past_len)
            paged = lambda c: c.transpose(0, 1, 3, 4, 2).reshape(n_phys * n_even, KV_COLS, PAGE_SIZE)
            o_cmp, sel_score = sample_cmp(page_table, paged(cache_cmp_k), paged(cache_cmp_v), as3(kc), as3(vc),
                                          as3(q), wk_bd_t, wv_bd_t, subpool, ident_kv, e, n_even, ts)
            sel_rows = bs * NSA_KV_HEADS * tsp
            sel = topk_mask(sel_score.reshape(sel_rows, -1), min(128, sel_rows)).reshape(sel_score.shape)
            n_win = cache_win_k.shape[2]
            win_k = cache_win_k[e].transpose(0, 2, 3, 1).reshape(bs, KV_COLS, n_win)
            win_v = cache_win_v[e].transpose(0, 2, 3, 1).reshape(bs, KV_COLS, n_win)
            o_nsa = sample_slc(page_table, paged(cache_slc_k), paged(cache_slc_v), as3(q), sel, expand_s,
                               as3(ks), as3(vs), win_k, win_v, as3(kw), as3(vw), o_cmp, as3(gate),
                               e, n_even, ts)
            xs = proj_norm_res([a_out.reshape(ns_tok, POOL_WIDTH), o_nsa.reshape(ns_tok, NSA_WIDTH)],
                               [w_out[:POOL_WIDTH], w_out[POOL_WIDTH:]], row(gn[1]), xs, tm_s)
            new4 = lambda a: as3(a)[:, :ts].reshape(bs, ts, NSA_KV_HEADS, HEAD_DIM)
            out_ev_s.append((jnp.concatenate([state_pool[e], u3[:, :ts]], axis=1)[:, -POOL_STATE:],
                             new4(kc), new4(vc), new4(ks), new4(vs),
                             jnp.concatenate([cache_win_k[e], new4(kw)], axis=1)[:, -n_win:],
                             jnp.concatenate([cache_win_v[e], new4(vw)], axis=1)[:, -n_win:]))
        else:
            o = li // 2
            w_in = w_in_odd[o].astype(BF16)
            cw = jnp.pad(conv_w[o], ((0, 8 - CONV_WIDTH), (0, 0)))
            consts = (cw, row(conv_b[o]), lru_wa[o].astype(BF16), lru_wx[o].astype(BF16),
                      row(lru_ba[o]), row(lru_bx[o]), row(lru_lambda[o]))
            w_out = w_out_odd[o].astype(BF16)
            groups = ((0, LRU_WIDTH, ((0, LRU_WIDTH, None),)),
                      (LRU_WIDTH, LRU_WIDTH, ((0, LRU_WIDTH, jax.nn.gelu),)))

            gated, tail, xb_tail = lru((xp.reshape(bp, tp, d), row(gn[0]), w_in),
                                       jnp.zeros((bp, 8, LRU_WIDTH), F32), jnp.zeros((bp, 1, LRU_WIDTH), F32),
                                       *consts, tt=tt_lru, first_pos_zero=True, fused_in_proj=True)
            mix_p = ([gated], [w_out])
            out_lru_p.append((xb_tail[:, -(CONV_WIDTH - 1):], tail[:, 7]))

            xb, yb = norm_matmul(xs, row(gn[0]), w_in, groups, (F32, F32), tm_s)
            xb3, yb3 = xb.reshape(bs, tsp, LRU_WIDTH), yb.reshape(bs, tsp, LRU_WIDTH)
            conv_past = jnp.pad(state_lru_conv[o], ((0, 0), (8 - (CONV_WIDTH - 1), 0), (0, 0)))
            gated, tail, _ = lru((xb3, yb3), conv_past, state_lru_h[o][:, None, :], *consts, tt=tsp,
                                 first_pos_zero=False, fused_in_proj=False)
            xs = proj_norm_res([gated.reshape(ns_tok, LRU_WIDTH)], [w_out], row(gn[1]), xs, tm_s)
            conv_new = jnp.concatenate([state_lru_conv[o], xb3[:, :ts]], axis=1)[:, -(CONV_WIDTH - 1):]
            out_lru_s.append((conv_new, tail[:, ts - 1]))

        wq, wo = w_xq[li].astype(BF16), w_xo[li].astype(BF16)
        mk3, mv3 = mem_k_all[li].reshape(bp, N_MEM, d), mem_v_all[li].reshape(bp, N_MEM, d)
        wg, wu, wd = w_ffn_gate[li].astype(BF16), w_ffn_up[li].astype(BF16), w_ffn_down[li].astype(BF16)
        xp = post_mixer(mix_p[0], mix_p[1], xp.reshape(bp, tp, d), gn, mk3, mv3, wq, wo, wg, wu, wd,
                        tm_p).reshape(np_tok, d)
        xs = _xattn_block(xs, bs, tsp, cache_mem_k, cache_mem_v, li, row(gn[2]), row(gn[3]), wq, wo, tm_s, tsp)
        xs = ffn(xs, row(gn[4]), row(gn[5]), wg, wu, wd, tm_s)

    stack = lambda items, k, axis=0: jnp.stack([s[k] for s in items], axis=axis)
    y_prompt = xp.reshape(bp, tp, d)
    y_sample = xs.reshape(bs, tsp, d)[:, :ts]
    return (y_prompt, y_sample,
            stack(out_ev_p, 0), stack(out_ev_s, 0),
            stack(out_ev_p, 1, 1), stack(out_ev_s, 1, 1),
            stack(out_ev_p, 2, 1), stack(out_ev_s, 2, 1),
            stack(out_ev_p, 3, 1), stack(out_ev_s, 3, 1),
            stack(out_ev_p, 4, 1), stack(out_ev_s, 4, 1),
            stack(out_ev_p, 5), stack(out_ev_s, 5),
            stack(out_ev_p, 6), stack(out_ev_s, 6),
            stack(out_lru_p, 1), stack(out_lru_s, 1),
            stack(out_lru_p, 0), stack(out_lru_s, 0),
            mem_k_all.reshape(depth, bp, N_MEM, XATTN_HEADS, XATTN_HEAD_DIM),
            mem_v_all.reshape(depth, bp, N_MEM, XATTN_HEADS, XATTN_HEAD_DIM))
```

```python
import functools

import numpy as np
import jax
import jax.numpy as jnp
from jax import lax
from jax.experimental import pallas as pl
from jax.experimental.pallas import tpu as pltpu

F32 = jnp.float32
BF16 = jnp.bfloat16

D_MODEL = 1024
RMS_EPS = 1e-6
NEG_INF = -1e30
FORCE_SCORE = 1e30
REMOVED_SCORE = -3e38
LOG2E = 1.4426950408889634
DENOM_ROWS = 8
M_INIT = -1e29

POOL_WIDTH = 512
POOL_WINDOWS = (2, 4, 8, 16)
POOL_GROUP = 128
POOL_STATE = 15
POOL_HALO = 16

HEAD_DIM = 64
NSA_HEADS = 8
NSA_KV_HEADS = 2
HEADS_PER_GROUP = NSA_HEADS // NSA_KV_HEADS
NSA_WIDTH = 512
KV_COLS = 128
CMP_BLOCK = 32
CMP_STRIDE = 16
SEL_BLOCK = 64
SEL_TOPK = 16
WINDOW = 512
PAGE_SIZE = 128
EVEN_IN = POOL_WIDTH + NSA_WIDTH + 6 * KV_COLS + 3 * NSA_HEADS
EVEN_IN_PAD = 1920

LRU_WIDTH = 1024
LRU_BLOCKS = 4
LRU_BLOCK = 256
CONV_WIDTH = 4
LRU_C = 8.0

N_MEM = 256
XATTN_HEADS = 4
XATTN_HEAD_DIM = 256
D_FF = 2816
FF_CHUNK = 256

SAMPLE_T_PAD = 8
PAGES_PER_STEP = 16
PAGE_SLOTS = 3
SEL_PER_STEP = PAGES_PER_STEP * PAGE_SIZE // SEL_BLOCK
SUB_PER_STEP = PAGES_PER_STEP * PAGE_SIZE // CMP_STRIDE
SEL_PER_CMP = SEL_BLOCK // CMP_STRIDE
LANE = 128

VMEM_LIMIT_BYTES = 56 * 1024 * 1024


def _cparams(*sem):
    return pltpu.CompilerParams(dimension_semantics=sem, vmem_limit_bytes=VMEM_LIMIT_BYTES)


def _rms(x, g):
    return x * lax.rsqrt(jnp.mean(x * x, axis=-1, keepdims=True) + RMS_EPS) * g


def _dot(a, b):
    return jnp.dot(a, b, preferred_element_type=F32)


def _dot_nt(a, b):
    return lax.dot_general(a, b, (((1,), (1,)), ((), ())), preferred_element_type=F32)


def _split3(x):
    p1 = x.astype(BF16)
    r1 = x - p1.astype(F32)
    p2 = r1.astype(BF16)
    p3 = (r1 - p2.astype(F32)).astype(BF16)
    return p1, p2, p3


def _full(shape):
    n = len(shape)
    return pl.BlockSpec(shape, lambda *_: (0,) * n)


def _norm_matmul_kernel(x_ref, g_ref, w_ref, *o_refs, groups):
    h = _rms(x_ref[...], g_ref[...]).astype(BF16)
    k = 0
    for start, width, outs in groups:
        z = _dot(h, w_ref[:, start:start + width])
        for off, w, post in outs:
            v = z[:, off:off + w]
            if post is not None:
                v = post(v)
            o_refs[k][...] = v.astype(o_refs[k].dtype)
            k += 1


def norm_matmul(x, g, w, groups, out_dtypes, tm):
    n, d = x.shape
    widths = [w_ for _, _, outs in groups for _, w_, _ in outs]
    return pl.pallas_call(
        functools.partial(_norm_matmul_kernel, groups=groups),
        grid=(n // tm,),
        in_specs=[pl.BlockSpec((tm, d), lambda i: (i, 0)), _full(g.shape), _full(w.shape)],
        out_specs=[pl.BlockSpec((tm, wd), lambda i: (i, 0)) for wd in widths],
        out_shape=[jax.ShapeDtypeStruct((n, wd), dt) for wd, dt in zip(widths, out_dtypes)],
        compiler_params=_cparams("parallel"),
    )(x, g, w)


def _memory_kv_kernel(x_ref, g_ref, w_ref, k_ref, v_ref):
    h = _rms(x_ref[...], g_ref[0]).astype(BF16)
    d = k_ref.shape[-1]
    k_ref[0] = _dot(h, w_ref[0, :, 0:d])
    v_ref[0] = _dot(h, w_ref[0, :, d:])


def memory_kv(mem, gains, wkv, tm):
    n, d = mem.shape
    n_layers = wkv.shape[0]
    out = pl.BlockSpec((1, tm, d), lambda l, i: (l, i, 0))
    return pl.pallas_call(
        _memory_kv_kernel,
        grid=(n_layers, n // tm),
        in_specs=[pl.BlockSpec((tm, d), lambda l, i: (i, 0)), pl.BlockSpec((1, 1, d), lambda l, i: (l, 0, 0)),
                  pl.BlockSpec((1, d, 2 * d), lambda l, i: (l, 0, 0))],
        out_specs=[out, out],
        out_shape=[jax.ShapeDtypeStruct((n_layers, n, d), F32)] * 2,
        compiler_params=_cparams("parallel", "parallel"),
    )(mem, gains, wkv)


def _even_in_prompt_kernel(x_ref, g_ref, wu_ref, wk_ref, wt_ref, u_ref, kpm_ref, qt_ref, gatet_ref, *kvt_refs):
    h = _rms(x_ref[...], g_ref[...]).astype(BF16)
    u_ref[...] = _dot(h, wu_ref[...])
    kpm_ref[...] = _dot(h, wk_ref[...]).astype(kpm_ref.dtype)
    z_t = _dot_nt(wt_ref[...], h)
    qt_ref[0] = (z_t[0:NSA_WIDTH] * (HEAD_DIM ** -0.5 * LOG2E)).astype(qt_ref.dtype)
    for k, ref in enumerate(kvt_refs):
        ref[0] = z_t[NSA_WIDTH + k * KV_COLS:NSA_WIDTH + (k + 1) * KV_COLS, :]
    gatet_ref[0] = jax.nn.sigmoid(z_t[NSA_WIDTH + 6 * KV_COLS:, :])


def even_in_prompt(x, g, w_u, w_kpm, w_t, b, t, tm):
    n, d = x.shape
    per_seq = t // tm
    row_tile = lambda w: pl.BlockSpec((tm, w), lambda i: (i, 0))
    feat_tile = lambda f: pl.BlockSpec((1, f, tm), lambda i: (i // per_seq, 0, i % per_seq))
    n_kv = 6
    return pl.pallas_call(
        _even_in_prompt_kernel,
        grid=(n // tm,),
        in_specs=[row_tile(d), _full(g.shape), _full(w_u.shape), _full(w_kpm.shape), _full(w_t.shape)],
        out_specs=([row_tile(POOL_WIDTH), row_tile(2 * KV_COLS), feat_tile(NSA_WIDTH), feat_tile(LANE)]
                   + [feat_tile(KV_COLS)] * n_kv),
        out_shape=([jax.ShapeDtypeStruct((n, POOL_WIDTH), F32), jax.ShapeDtypeStruct((n, 2 * KV_COLS), BF16),
                    jax.ShapeDtypeStruct((b, NSA_WIDTH, t), BF16), jax.ShapeDtypeStruct((b, LANE, t), F32)]
                   + [jax.ShapeDtypeStruct((b, KV_COLS, t), F32)] * n_kv),
        compiler_params=_cparams("parallel"),
    )(x, g, w_u, w_kpm, w_t)


def _proj_norm_res_kernel(*refs, n_in):
    a_refs, w_refs = refs[:n_in], refs[n_in:2 * n_in]
    g_ref, x_ref, o_ref = refs[2 * n_in:]
    acc = _dot(a_refs[0][...], w_refs[0][...])
    for a_ref, w_ref in zip(a_refs[1:], w_refs[1:]):
        acc = acc + _dot(a_ref[...], w_ref[...])
    o_ref[...] = x_ref[...] + _rms(acc, g_ref[...])


def proj_norm_res(a_list, w_list, g, x, tm):
    n, d = x.shape
    n_in = len(a_list)
    return pl.pallas_call(
        functools.partial(_proj_norm_res_kernel, n_in=n_in),
        grid=(n // tm,),
        in_specs=([pl.BlockSpec((tm, a.shape[1]), lambda i: (i, 0)) for a in a_list]
                  + [_full(w.shape) for w in w_list]
                  + [_full(g.shape), pl.BlockSpec((tm, d), lambda i: (i, 0))]),
        out_specs=pl.BlockSpec((tm, d), lambda i: (i, 0)),
        out_shape=jax.ShapeDtypeStruct((n, d), F32),
        compiler_params=_cparams("parallel"),
    )(*a_list, *w_list, g, x)


def _ffn_block(x, g_in, g_out, wg_ref, wu_ref, wd_ref):
    h = _rms(x, g_in).astype(BF16)
    acc = None
    for c in range(D_FF // FF_CHUNK):
        cols = slice(c * FF_CHUNK, (c + 1) * FF_CHUNK)
        gate = _dot(h, wg_ref[:, cols])
        up = _dot(h, wu_ref[:, cols])
        act = (jax.nn.silu(gate) * up).astype(BF16)
        part = _dot(act, wd_ref[cols, :])
        acc = part if acc is None else acc + part
    return x + _rms(acc, g_out)


def _ffn_kernel(x_ref, g_in_ref, g_out_ref, wg_ref, wu_ref, wd_ref, o_ref):
    o_ref[...] = _ffn_block(x_ref[...], g_in_ref[...], g_out_ref[...], wg_ref, wu_ref, wd_ref)


def ffn(x, g_in, g_out, wg, wu, wd, tm):
    n, d = x.shape
    resident = functools.partial(pl.BlockSpec, pipeline_mode=pl.Buffered(1))
    return pl.pallas_call(
        _ffn_kernel,
        grid=(n // tm,),
        in_specs=[pl.BlockSpec((tm, d), lambda i: (i, 0)), _full(g_in.shape), _full(g_out.shape),
                  resident(wg.shape, lambda i: (0, 0)), resident(wu.shape, lambda i: (0, 0)),
                  resident(wd.shape, lambda i: (0, 0))],
        out_specs=pl.BlockSpec((tm, d), lambda i: (i, 0)),
        out_shape=jax.ShapeDtypeStruct((n, d), F32),
        compiler_params=_cparams("parallel"),
    )(x, g_in, g_out, wg, wu, wd)


def _xattn_heads(q, mk_ref, mv_ref):
    head_cols = [slice(hd * XATTN_HEAD_DIM, (hd + 1) * XATTN_HEAD_DIM) for hd in range(XATTN_HEADS)]
    if len(mk_ref.shape) == 4:
        rows = q.shape[0]
        n_kv = N_MEM * XATTN_HEADS
        k_all = mk_ref[0].reshape(n_kv, XATTN_HEAD_DIM).astype(BF16)
        v_all = mv_ref[0].reshape(n_kv, XATTN_HEAD_DIM).astype(BF16)
        q_st = jnp.concatenate([q[:, cols] for cols in head_cols], axis=0)
        own = (lax.broadcasted_iota(jnp.int32, (1, n_kv), 1) % XATTN_HEADS
               == lax.broadcasted_iota(jnp.int32, (XATTN_HEADS * rows, 1), 0) // rows)
        p = _masked_softmax_rows(_dot_nt(q_st, k_all), own)
        o_st = _dot(p.astype(BF16), v_all).astype(BF16)
        return jnp.concatenate([o_st[hd * rows:(hd + 1) * rows] for hd in range(XATTN_HEADS)], axis=-1)
    outs = []
    for cols in head_cols:
        k = mk_ref[0, :, cols].astype(BF16)
        v = mv_ref[0, :, cols].astype(BF16)
        s = _dot_nt(q[:, cols], k)
        e = jnp.exp(s - jnp.max(s, axis=-1, keepdims=True))
        p = e / jnp.sum(e, axis=-1, keepdims=True)
        outs.append(_dot(p.astype(BF16), v).astype(BF16))
    return jnp.concatenate(outs, axis=-1)


def _xattn_kernel(q_ref, mk_ref, mv_ref, o_ref):
    o_ref[0] = _xattn_heads(q_ref[0], mk_ref, mv_ref)


def xattn(q, mk, mv, layer, tq):
    b, t, d = q.shape
    mem_spec = pl.BlockSpec((None, 1) + mk.shape[2:], lambda i, j: (layer, i, 0, 0, 0))
    return pl.pallas_call(
        _xattn_kernel,
        grid=(b, t // tq),
        in_specs=[pl.BlockSpec((1, tq, d), lambda i, j: (i, j, 0)), mem_spec, mem_spec],
        out_specs=pl.BlockSpec((1, tq, d), lambda i, j: (i, j, 0)),
        out_shape=jax.ShapeDtypeStruct((b, t, d), BF16),
        compiler_params=_cparams("parallel", "parallel"),
    )(q, mk, mv)


def _post_mixer_kernel(*refs, n_in):
    a_refs, w_refs = refs[:n_in], refs[n_in:2 * n_in]
    x_ref, gn_ref, mk_ref, mv_ref, wq_ref, wo_ref, wg_ref, wu_ref, wd_ref, o_ref = refs[2 * n_in:]
    gain = lambda k: gn_ref[k:k + 1, :]
    mix = _dot(a_refs[0][0], w_refs[0][...])
    for a_ref, w_ref in zip(a_refs[1:], w_refs[1:]):
        mix = mix + _dot(a_ref[0], w_ref[...])
    x1 = x_ref[0] + _rms(mix, gain(1))
    q = (_dot(_rms(x1, gain(2)).astype(BF16), wq_ref[...]) * (XATTN_HEAD_DIM ** -0.5)).astype(BF16)
    attn = _xattn_heads(q, mk_ref, mv_ref)
    x2 = x1 + _rms(_dot(attn, wo_ref[...]), gain(3))
    o_ref[0] = _ffn_block(x2, gain(4), gain(5), wg_ref, wu_ref, wd_ref)


def post_mixer(a_list, w_list, x, gn, mk, mv, wq, wo, wg, wu, wd, tm):
    b, t, d = x.shape
    n_in = len(a_list)
    resident = lambda a: pl.BlockSpec(a.shape, lambda i, j: (0,) * a.ndim, pipeline_mode=pl.Buffered(1))
    tile = lambda w: pl.BlockSpec((1, tm, w), lambda i, j: (i, j, 0))
    per_b = lambda a: pl.BlockSpec((1,) + a.shape[1:], lambda i, j: (i, 0, 0))
    return pl.pallas_call(
        functools.partial(_post_mixer_kernel, n_in=n_in),
        grid=(b, t // tm),
        in_specs=([tile(a.shape[2]) for a in a_list] + [resident(w) for w in w_list]
                  + [tile(d), resident(gn), per_b(mk), per_b(mv)]
                  + [resident(w) for w in (wq, wo, wg, wu, wd)]),
        out_specs=tile(d),
        out_shape=jax.ShapeDtypeStruct((b, t, d), F32),
        compiler_params=_cparams("parallel", "parallel"),
    )(*a_list, *w_list, x, gn, mk, mv, wq, wo, wg, wu, wd)


def _pool_kernel(u_ref, past_ref, w_ref, scale_ref, o_ref, buf, *, t, chunk, start_pos):
    buf[0:POOL_HALO, :] = past_ref[0]
    buf[POOL_HALO:POOL_HALO + t, :] = u_ref[0]
    for c in range(t // chunk):
        base = POOL_HALO + c * chunk
        pos = start_pos + c * chunk + lax.broadcasted_iota(jnp.int32, (chunk, 1), 0)
        for gi, win in enumerate(POOL_WINDOWS):
            cols = slice(gi * POOL_GROUP, (gi + 1) * POOL_GROUP)
            cur = buf[base:base + chunk, cols]
            win_sum = cur
            for k in range(1, win):
                win_sum = win_sum + buf[base - k:base - k + chunk, cols]
            cnt = jnp.minimum(pos + 1, win).astype(F32)
            diff = win_sum / cnt - cur
            y = _dot(diff.astype(BF16), w_ref[gi]) * scale_ref[:, cols]
            o_ref[0, c * chunk:(c + 1) * chunk, cols] = y.astype(o_ref.dtype)


def pool_mix(u, past, w, scale, start_pos):
    b, t, c = u.shape
    chunk = min(t, 256)
    return pl.pallas_call(
        functools.partial(_pool_kernel, t=t, chunk=chunk, start_pos=start_pos),
        grid=(b,),
        in_specs=[pl.BlockSpec((1, t, c), lambda i: (i, 0, 0)),
                  pl.BlockSpec((1, POOL_HALO, c), lambda i: (i, 0, 0)),
                  _full(w.shape), _full(scale.shape)],
        out_specs=pl.BlockSpec((1, t, c), lambda i: (i, 0, 0)),
        out_shape=jax.ShapeDtypeStruct((b, t, c), BF16),
        scratch_shapes=[pltpu.VMEM((POOL_HALO + t, c), F32)],
        compiler_params=_cparams("parallel"),
    )(u, past, w, scale)


def _compress_kernel(kc_ref, vc_ref, pool_ref, wkt_ref, wvt_ref, ident_ref, k_ref, vt_ref):
    def compress(src, wt_ref):
        x = src[0]
        hi = x.astype(BF16)
        lo = (x - hi.astype(F32)).astype(BF16)
        mean_t = _dot(hi, pool_ref[...]) + _dot(lo, pool_ref[...])
        return _dot(wt_ref[...], mean_t.astype(BF16)).astype(BF16)
    k_ref[0] = _dot_nt(ident_ref[...], compress(kc_ref, wkt_ref)).astype(BF16)
    vt_ref[0] = compress(vc_ref, wvt_ref)


def compress_kv(kc_t, vc_t, pool_mat, wk_bd_t, wv_bd_t, ident):
    b, c, t = kc_t.shape
    n_cmp = pool_mat.shape[1]
    per_b = pl.BlockSpec((1, c, t), lambda i: (i, 0, 0))
    return pl.pallas_call(
        _compress_kernel,
        grid=(b,),
        in_specs=[per_b, per_b, _full(pool_mat.shape), _full(wk_bd_t.shape), _full(wv_bd_t.shape),
                  _full(ident.shape)],
        out_specs=[pl.BlockSpec((1, n_cmp, c), lambda i: (i, 0, 0)), pl.BlockSpec((1, c, n_cmp), lambda i: (i, 0, 0))],
        out_shape=[jax.ShapeDtypeStruct((b, n_cmp, c), BF16), jax.ShapeDtypeStruct((b, c, n_cmp), BF16)],
        compiler_params=_cparams("parallel"),
    )(kc_t, vc_t, pool_mat, wk_bd_t, wv_bd_t, ident)


def _masked_softmax_rows(s, mask):
    s = jnp.where(mask, s, NEG_INF)
    e = jnp.where(mask, jnp.exp(s - jnp.max(s, axis=-1, keepdims=True)), 0.0)
    return e / jnp.maximum(jnp.sum(e, axis=-1, keepdims=True), 1e-30)


def _flash_step(raw_ref, v_aug, mask, m_ref, acc_ref):
    tq = mask.shape[1]
    bias = jnp.where(mask, 0.0, NEG_INF)
    for hh in range(HEADS_PER_GROUP):
        lanes = slice(hh * tq, (hh + 1) * tq)
        s = raw_ref[:, lanes] + bias
        m_old = m_ref[:, lanes]
        m_new = jnp.maximum(m_old, jnp.max(s, axis=0, keepdims=True))
        alpha = jnp.exp2(m_old - m_new)
        e = jnp.exp2(s - m_new)
        m_ref[:, lanes] = m_new
        acc_ref[:, lanes] = alpha * acc_ref[:, lanes] + _dot(v_aug, e.astype(BF16))


def _nsa_prompt_kernel(qt_ref, gatet_ref, kcmp_ref, vcmpt_ref, kpm_ref, vst_ref, vwt_ref,
                       mselt_ref, ident_ref, o_ref, sel_s, raw_s, m_s, acc_s, *, tq, tk):
    i = pl.program_id(1)
    q0 = i * tq
    n_cmp = kcmp_ref.shape[1]
    n_sel = mselt_ref.shape[0]
    blocks_per_tile = tk // SEL_BLOCK
    tiles_per_q = tq // tk
    qpos_row = q0 + lax.broadcasted_iota(jnp.int32, (1, tq), 1)
    key_col = lax.broadcasted_iota(jnp.int32, (tk, 1), 0)
    cmp_end_col = lax.broadcasted_iota(jnp.int32, (n_cmp, 1), 0) * CMP_STRIDE + (CMP_BLOCK - 1)
    cmp_bias = jnp.where(cmp_end_col <= qpos_row, 0.0, NEG_INF)
    sel_j = lax.broadcasted_iota(jnp.int32, (n_sel, 1), 0)
    cur_row = qpos_row // SEL_BLOCK
    gate_t = gatet_ref[0]
    zero_half = jnp.zeros((HEAD_DIM, HEADS_PER_GROUP * tq), BF16)

    groups = range(NSA_KV_HEADS)
    grows = [slice(g * HEAD_DIM, (g + 1) * HEAD_DIM) for g in groups]
    q_pads, o_cmps = [], []
    for g in groups:
        q_t_g = jnp.concatenate(
            [qt_ref[0, (g * HEADS_PER_GROUP + hh) * HEAD_DIM:(g * HEADS_PER_GROUP + hh + 1) * HEAD_DIM, :]
             for hh in range(HEADS_PER_GROUP)], axis=1)
        q_pad = jnp.concatenate([q_t_g, zero_half] if g == 0 else [zero_half, q_t_g], axis=0)
        q_pads.append(q_pad)

        s_t = _dot(kcmp_ref[0], q_pad) + jnp.concatenate([cmp_bias] * HEADS_PER_GROUP, axis=1)
        e_t = jnp.exp2(s_t - jnp.maximum(jnp.max(s_t, axis=0, keepdims=True), M_INIT))
        p_t = e_t / jnp.maximum(jnp.sum(e_t, axis=0, keepdims=True), 1e-30)
        o_cmps.append(_dot(vcmpt_ref[0, grows[g], :], p_t.astype(BF16)))
        psum_t = p_t[:, 0:tq]
        for hh in range(1, HEADS_PER_GROUP):
            psum_t = psum_t + p_t[:, hh * tq:(hh + 1) * tq]

        imp_t = None
        for part in _split3(psum_t):
            d = _dot(mselt_ref[...], part)
            imp_t = d if imp_t is None else imp_t + d
        valid = sel_j * SEL_BLOCK <= qpos_row
        forced = (sel_j == 0) | (sel_j == cur_row) | (sel_j == cur_row - 1)
        score = jnp.where(forced, FORCE_SCORE, jnp.where(valid, imp_t, NEG_INF))
        rank = jnp.zeros((n_sel, tq), F32)
        for jp in range(n_sel):
            sj = score[jp:jp + 1, :]
            beats = (sj > score) | ((sj == score) & (sel_j > jp))
            rank = rank + jnp.where(beats, 1.0, 0.0)
        sel_s[g] = jnp.where(rank < SEL_TOPK, 1.0, 0.0)

    m_s[...] = jnp.full(m_s.shape, M_INIT, F32)
    acc_s[...] = jnp.zeros(acc_s.shape, F32)
    ones_rows = jnp.ones((DENOM_ROWS, tk), BF16)
    with_ones = lambda v_t: jnp.concatenate([v_t.astype(BF16), ones_rows], axis=0)
    n_tiles = (i + 1) * tiles_per_q
    first_win = jnp.maximum((i * tq - (WINDOW - 1)) // tk, 0)
    key_rows = lambda kt: pl.ds(pl.multiple_of(kt * tk, tk), tk)

    def issue_scores(kt, slot, branch):
        k = kpm_ref[0, key_rows(kt), branch * KV_COLS:(branch + 1) * KV_COLS]
        for g in groups:
            raw_s[slot, NSA_KV_HEADS * branch + g] = _dot(k, q_pads[g])

    def slc_update(kt, slot):
        causal = kt * tk + key_col <= qpos_row
        for g in groups:
            chosen = jnp.concatenate(
                [jnp.broadcast_to(sel_s[g, pl.ds(kt * blocks_per_tile + r, 1), :], (SEL_BLOCK, tq))
                 for r in range(blocks_per_tile)], axis=0)
            v_aug = with_ones(vst_ref[0, grows[g], key_rows(kt)])
            _flash_step(raw_s.at[slot, g], v_aug, (chosen > 0.5) & causal, m_s.at[g], acc_s.at[g])

    def win_update(kt, slot):
        dist = qpos_row - (kt * tk + key_col)
        mask = (dist >= 0) & (dist < WINDOW)
        for g in groups:
            c = NSA_KV_HEADS + g
            v_aug = with_ones(vwt_ref[0, grows[g], key_rows(kt)])
            _flash_step(raw_s.at[slot, c], v_aug, mask, m_s.at[c], acc_s.at[c])

    def far_body(pair, _):
        for slot in (0, 1):
            kt = 2 * pair + slot
            issue_scores(kt + 1, 1 - slot, 0)
            slc_update(kt, slot)
        return 0

    def band_body(pair, _):
        for slot in (0, 1):
            kt = 2 * pair + slot
            nxt = jnp.minimum(kt + 1, n_tiles - 1)
            issue_scores(nxt, 1 - slot, 0)
            issue_scores(nxt, 1 - slot, 1)
            slc_update(kt, slot)
            win_update(kt, slot)
        return 0

    issue_scores(0, 0, 0)
    lax.fori_loop(0, first_win // 2, far_body, 0)
    issue_scores(first_win, 0, 1)
    lax.fori_loop(first_win // 2, n_tiles // 2, band_body, 0)

    head_out = [None] * NSA_HEADS
    for g in groups:
        normalised = lambda c: acc_s[c, 0:HEAD_DIM, :] / acc_s[c, HEAD_DIM:HEAD_DIM + 1, :]
        o_slc = normalised(g)
        o_win = normalised(NSA_KV_HEADS + g)
        for hh in range(HEADS_PER_GROUP):
            h = g * HEADS_PER_GROUP + hh
            lanes = slice(hh * tq, (hh + 1) * tq)
            head_out[h] = (gate_t[3 * h:3 * h + 1, :] * o_cmps[g][:, lanes]
                           + gate_t[3 * h + 1:3 * h + 2, :] * o_slc[:, lanes]
                           + gate_t[3 * h + 2:3 * h + 3, :] * o_win[:, lanes])
    o_t = jnp.concatenate(head_out, axis=0).astype(BF16)
    o_ref[0] = _dot_nt(ident_ref[...], o_t).astype(o_ref.dtype)


def nsa_prompt(q_t, gate_t, kcmp, vcmp_t, kpm, vs_t, vw_t, mselt, ident, tq, tk):
    b, _, t = q_t.shape
    feat_tile = lambda f: pl.BlockSpec((1, f, tq), lambda i, j: (i, 0, j))
    whole = lambda a: pl.BlockSpec((1,) + a.shape[1:], lambda i, j: (i, 0, 0))
    per_batch = [kcmp, vcmp_t, kpm, vs_t, vw_t]
    n_chain = 2 * NSA_KV_HEADS
    assert tq % (2 * tk) == 0 and WINDOW % (2 * tk) == 0 and tk % SEL_BLOCK == 0
    return pl.pallas_call(
        functools.partial(_nsa_prompt_kernel, tq=tq, tk=tk),
        grid=(b, t // tq),
        in_specs=([feat_tile(NSA_WIDTH), feat_tile(LANE)] + [whole(a) for a in per_batch]
                  + [_full(mselt.shape), _full(ident.shape)]),
        out_specs=pl.BlockSpec((1, tq, NSA_WIDTH), lambda i, j: (i, j, 0)),
        out_shape=jax.ShapeDtypeStruct((b, t, NSA_WIDTH), BF16),
        scratch_shapes=[pltpu.VMEM((NSA_KV_HEADS, mselt.shape[0], tq), F32),
                        pltpu.VMEM((2, n_chain, tk, HEADS_PER_GROUP * tq), F32),
                        pltpu.VMEM((n_chain, 1, HEADS_PER_GROUP * tq), F32),
                        pltpu.VMEM((n_chain, HEAD_DIM + DENOM_ROWS, HEADS_PER_GROUP * tq), F32)],
        compiler_params=_cparams("parallel", "arbitrary"),
    )(q_t, gate_t, *per_batch, mselt, ident)


def _stack_heads(q, g):
    return jnp.concatenate(
        [q[:, (g * HEADS_PER_GROUP + hh) * HEAD_DIM:(g * HEADS_PER_GROUP + hh + 1) * HEAD_DIM]
         for hh in range(HEADS_PER_GROUP)], axis=0)


def _sample_cmp_kernel(pt_ref, k_hbm, v_hbm, kn_ref, vn_ref, q_ref, wkt_ref, wvt_ref, subpool_ref, ident_ref,
                       ocmp_ref, sel_ref, sub_k, sub_v, kbuf, vbuf, sems, *, n_valid, past_len, layer, n_layers):
    p = PAGES_PER_STEP
    tp = SAMPLE_T_PAD
    c = pl.program_id(1)
    n_sub = past_len // CMP_STRIDE
    steps = past_len // (p * PAGE_SIZE)

    pages = _fetch_pages(pt_ref, k_hbm, v_hbm, kbuf, vbuf, sems, layer, n_layers)
    for x, sub in zip(pages, (sub_k, sub_v)):
        hi = x.astype(BF16)
        lo = (x - hi.astype(F32)).astype(BF16)
        sub[c] = _dot(hi, subpool_ref[...]) + _dot(lo, subpool_ref[...])

    @pl.when(c == steps - 1)
    def _():
        t_col = lax.broadcasted_iota(jnp.int32, (tp, 1), 0)
        new_valid = lax.broadcasted_iota(jnp.int32, (1, tp), 1) < n_valid
        first_lane = lax.broadcasted_iota(jnp.int32, (1, SUB_PER_STEP), 1) == 0
        cmp_kv = []
        for new_ref, sub, wt_ref in ((kn_ref, sub_k, wkt_ref), (vn_ref, sub_v, wvt_ref)):
            new_t = None
            for part in _split3(new_ref[0]):
                d = _dot_nt(ident_ref[...], part)
                new_t = d if new_t is None else new_t + d
            new_sum = jnp.sum(jnp.where(new_valid, new_t, 0.0), axis=1, keepdims=True)
            sub[steps] = jnp.where(first_lane, new_sum, 0.0)
            sums = jnp.concatenate([sub[s] for s in range(steps + 1)], axis=1)
            mean_t = (sums[:, 0:n_sub] + sums[:, 1:n_sub + 1]) * (1.0 / CMP_BLOCK)
            cmp_kv.append(_dot(wt_ref[...], mean_t.astype(BF16)).astype(BF16))
        kcmp_t, vcmp_t = cmp_kv

        rows = HEADS_PER_GROUP * tp
        qpos_col = past_len + lax.broadcasted_iota(jnp.int32, (rows, 1), 0) % tp
        cmp_end_row = lax.broadcasted_iota(jnp.int32, (1, n_sub), 1) * CMP_STRIDE + (CMP_BLOCK - 1)
        cmp_mask = cmp_end_row <= qpos_col

        n_lane = sel_ref.shape[2]
        n_sel = -(-(past_len + n_valid) // SEL_BLOCK)
        lane = lax.broadcasted_iota(jnp.int32, (1, n_lane), 1)
        sel_j = lane // SEL_PER_CMP
        lane_used = (lane % SEL_PER_CMP == 0) & (sel_j < n_sel)
        qpos_t = past_len + t_col
        cur = qpos_t // SEL_BLOCK
        valid = lane_used & (sel_j * SEL_BLOCK <= qpos_t)
        forced = lane_used & ((sel_j == 0) | (sel_j == cur) | (sel_j == cur - 1))
        lane_pad = jnp.zeros((tp, n_lane - n_sub), F32)

        q = q_ref[0]
        o_heads = [None] * NSA_HEADS
        scores = []
        for g in range(NSA_KV_HEADS):
            gcols = slice(g * HEAD_DIM, (g + 1) * HEAD_DIM)
            prob = _masked_softmax_rows(_dot(_stack_heads(q, g), kcmp_t[gcols, :]), cmp_mask)
            o_g = _dot_nt(prob.astype(BF16), vcmp_t[gcols, :])
            psum = prob[0:tp]
            for hh in range(1, HEADS_PER_GROUP):
                psum = psum + prob[hh * tp:(hh + 1) * tp]
                o_heads[g * HEADS_PER_GROUP + hh] = o_g[hh * tp:(hh + 1) * tp]
            o_heads[g * HEADS_PER_GROUP] = o_g[0:tp]
            pz = jnp.concatenate([psum, lane_pad], axis=1)
            imp = pltpu.roll(pz, 1, 1) + pz
            for d in range(1, SEL_PER_CMP):
                imp = imp + pltpu.roll(pz, n_lane - d, 1)
            scores.append(jnp.where(forced, FORCE_SCORE, jnp.where(valid, imp, NEG_INF)))
        sel_ref[0] = jnp.concatenate(scores, axis=0)
        ocmp_ref[0] = jnp.concatenate(o_heads, axis=-1)


def _fetch_pages(pt_ref, k_hbm, v_hbm, kbuf, vbuf, sems, layer, n_layers):
    b, c = pl.program_id(0), pl.program_id(1)
    n_b, steps = pt_ref.shape[0], pt_ref.shape[1] // PAGES_PER_STEP
    step = b * steps + c
    slot = step % PAGE_SLOTS
    ahead = PAGE_SLOTS - 1

    def copies(bb, cc, to_slot, for_wait=False):
        out = []
        for k in range(PAGES_PER_STEP):
            page = 0 if for_wait else pt_ref[bb, cc * PAGES_PER_STEP + k] * n_layers + layer
            out.append(pltpu.make_async_copy(k_hbm.at[page], kbuf.at[to_slot, k], sems.at[to_slot, 0]))
            out.append(pltpu.make_async_copy(v_hbm.at[page], vbuf.at[to_slot, k], sems.at[to_slot, 1]))
        return out

    @pl.when(step == 0)
    def _():
        for s in range(min(ahead, n_b * steps)):
            for cp in copies(s // steps, s % steps, s % PAGE_SLOTS):
                cp.start()

    @pl.when(step + ahead < n_b * steps)
    def _():
        nxt = step + ahead
        for cp in copies(nxt // steps, nxt % steps, nxt % PAGE_SLOTS):
            cp.start()

    for cp in copies(b, c, slot, for_wait=True):
        cp.wait()
    k_all = jnp.concatenate([kbuf[slot, k] for k in range(PAGES_PER_STEP)], axis=1)
    v_all = jnp.concatenate([vbuf[slot, k] for k in range(PAGES_PER_STEP)], axis=1)
    return k_all, v_all


_PAGE_SCRATCH = [pltpu.VMEM((PAGE_SLOTS, PAGES_PER_STEP, KV_COLS, PAGE_SIZE), F32),
                 pltpu.VMEM((PAGE_SLOTS, PAGES_PER_STEP, KV_COLS, PAGE_SIZE), F32),
                 pltpu.SemaphoreType.DMA((PAGE_SLOTS, 2))]


def sample_cmp(page_table, cache_k, cache_v, kc_new, vc_new, q, wk_bd_t, wv_bd_t, subpool, ident,
               layer, n_layers, n_valid):
    b, n_pages = page_table.shape
    past_len = n_pages * PAGE_SIZE
    steps = n_pages // PAGES_PER_STEP
    assert SUB_PER_STEP == LANE
    n_lane = (steps + 1) * LANE
    per_b = lambda a: pl.BlockSpec((1,) + a.shape[1:], lambda i, c, pt: (i, 0, 0))
    const = lambda a: pl.BlockSpec(a.shape, lambda i, c, pt: (0,) * a.ndim)
    consts = [wk_bd_t, wv_bd_t, subpool, ident]
    hbm = pl.BlockSpec(memory_space=pl.ANY)
    grid_spec = pltpu.PrefetchScalarGridSpec(
        num_scalar_prefetch=1,
        grid=(b, steps),
        in_specs=[hbm, hbm, per_b(kc_new), per_b(vc_new), per_b(q)] + [const(a) for a in consts],
        out_specs=[pl.BlockSpec((1, SAMPLE_T_PAD, NSA_WIDTH), lambda i, c, pt: (i, 0, 0)),
                   pl.BlockSpec((1, 2 * SAMPLE_T_PAD, n_lane), lambda i, c, pt: (i, 0, 0))],
        scratch_shapes=[pltpu.VMEM((steps + 1, KV_COLS, SUB_PER_STEP), F32)] * 2 + _PAGE_SCRATCH,
    )
    return pl.pallas_call(
        functools.partial(_sample_cmp_kernel, n_valid=n_valid, past_len=past_len, layer=layer,
                          n_layers=n_layers),
        grid_spec=grid_spec,
        out_shape=[jax.ShapeDtypeStruct((b, SAMPLE_T_PAD, NSA_WIDTH), F32),
                   jax.ShapeDtypeStruct((b, 2 * SAMPLE_T_PAD, n_lane), F32)],
        compiler_params=_cparams("arbitrary", "arbitrary"),
    )(page_table, cache_k, cache_v, kc_new, vc_new, q, *consts)


def _topk_kernel(score_ref, o_ref):
    score = score_ref[...]
    n_lane = score.shape[-1]
    lane = lax.broadcasted_iota(jnp.int32, (1, n_lane), 1)
    chosen = jnp.zeros(score.shape, F32)
    for _ in range(SEL_TOPK):
        best = jnp.max(score, axis=-1, keepdims=True)
        first = jnp.min(jnp.where(score == best, lane, n_lane), axis=-1, keepdims=True)
        hit = lane == first
        chosen = jnp.where(hit, 1.0, chosen)
        score = jnp.where(hit, REMOVED_SCORE, score)
    o_ref[...] = chosen


def topk_mask(score, rows_per_step):
    r, n_lane = score.shape
    spec = pl.BlockSpec((rows_per_step, n_lane), lambda i: (i, 0))
    return pl.pallas_call(
        _topk_kernel,
        grid=(r // rows_per_step,),
        in_specs=[spec],
        out_specs=spec,
        out_shape=jax.ShapeDtypeStruct((r, n_lane), F32),
        compiler_params=_cparams("parallel"),
    )(score)


def _sample_slc_kernel(pt_ref, k_hbm, v_hbm, q_ref, sel_ref, expand_ref, ksn_ref, vsn_ref, wk_ref, wv_ref,
                       kwn_ref, vwn_ref, ocmp_ref, gate_ref, o_ref, m_s, l_s, acc_s, kbuf, vbuf, sems,
                       *, n_valid, past_len, layer, n_layers):
    tp = SAMPLE_T_PAD
    rows = HEADS_PER_GROUP * tp
    c = pl.program_id(1)
    q = q_ref[0]
    zero = jnp.zeros((rows, HEAD_DIM), BF16)
    q_bd = jnp.concatenate([jnp.concatenate([_stack_heads(q, 0), zero], axis=1),
                            jnp.concatenate([zero, _stack_heads(q, 1)], axis=1)], axis=0)
    t_col = lax.broadcasted_iota(jnp.int32, (NSA_KV_HEADS * rows, 1), 0) % tp
    new_row = lax.broadcasted_iota(jnp.int32, (1, tp), 1)
    new_mask = (new_row <= t_col) & (new_row < n_valid)

    @pl.when(c == 0)
    def _():
        m_s[...] = jnp.full(m_s.shape, NEG_INF, F32)
        l_s[...] = jnp.zeros(l_s.shape, F32)
        acc_s[...] = jnp.zeros(acc_s.shape, F32)

    k_all, v_all = _fetch_pages(pt_ref, k_hbm, v_hbm, kbuf, vbuf, sems, layer, n_layers)
    k_all_t, v_all_t = k_all.astype(BF16), v_all.astype(BF16)
    chosen = _dot(sel_ref[0].astype(BF16), expand_ref[...])
    keymask = jnp.concatenate([chosen[0:tp]] * HEADS_PER_GROUP + [chosen[tp:2 * tp]] * HEADS_PER_GROUP,
                              axis=0) > 0.5
    s = jnp.where(keymask, _dot(q_bd, k_all_t), NEG_INF)
    m_old = m_s[...]
    m_new = jnp.maximum(m_old, jnp.max(s, axis=-1, keepdims=True))
    alpha = jnp.exp(m_old - m_new)
    e = jnp.where(keymask, jnp.exp(s - m_new), 0.0)
    l_s[...] = alpha * l_s[...] + jnp.sum(e, axis=-1, keepdims=True)
    acc_s[...] = alpha * acc_s[...] + _dot_nt(e.astype(BF16), v_all_t)
    m_s[...] = m_new

    @pl.when(c == pl.num_programs(1) - 1)
    def _():
        n_win = wk_ref.shape[2]
        win_row = lax.broadcasted_iota(jnp.int32, (1, n_win), 1)
        dist = t_col + n_win - win_row
        win_mask = (dist >= 0) & (dist < WINDOW)
        s_n = jnp.where(new_mask, _dot_nt(q_bd, ksn_ref[0].astype(BF16)), NEG_INF)
        m_fin = jnp.maximum(m_new, jnp.max(s_n, axis=-1, keepdims=True))
        a_fin = jnp.exp(m_new - m_fin)
        e_n = jnp.where(new_mask, jnp.exp(s_n - m_fin), 0.0)
        l_fin = a_fin * l_s[...] + jnp.sum(e_n, axis=-1, keepdims=True)
        o_slc = (a_fin * acc_s[...] + _dot(e_n.astype(BF16), vsn_ref[0].astype(BF16))) / l_fin
        s_p = jnp.where(win_mask, _dot(q_bd, wk_ref[0].astype(BF16)), NEG_INF)
        s_w = jnp.where(new_mask, _dot_nt(q_bd, kwn_ref[0].astype(BF16)), NEG_INF)
        m_w = jnp.maximum(jnp.max(s_p, axis=-1, keepdims=True), jnp.max(s_w, axis=-1, keepdims=True))
        e_p = jnp.where(win_mask, jnp.exp(s_p - m_w), 0.0)
        e_w = jnp.where(new_mask, jnp.exp(s_w - m_w), 0.0)
        l_w = jnp.sum(e_p, axis=-1, keepdims=True) + jnp.sum(e_w, axis=-1, keepdims=True)
        o_win = (_dot_nt(e_p.astype(BF16), wv_ref[0].astype(BF16))
                 + _dot(e_w.astype(BF16), vwn_ref[0].astype(BF16))) / l_w
        gate_all = gate_ref[0]
        head_out = []
        for h in range(NSA_HEADS):
            g, hh = divmod(h, HEADS_PER_GROUP)
            r = slice(g * rows + hh * tp, g * rows + (hh + 1) * tp)
            cols = slice(g * HEAD_DIM, (g + 1) * HEAD_DIM)
            gate = gate_all[:, 3 * h:3 * h + 3]
            head_out.append(gate[:, 0:1] * ocmp_ref[0, :, h * HEAD_DIM:(h + 1) * HEAD_DIM]
                            + gate[:, 1:2] * o_slc[r, cols] + gate[:, 2:3] * o_win[r, cols])
        o_ref[0] = jnp.concatenate(head_out, axis=-1).astype(o_ref.dtype)


def sample_slc(page_table, cache_k, cache_v, q, sel, expand, ks_new, vs_new, win_k, win_v, kw_new, vw_new,
               o_cmp, gate, layer, n_layers, n_valid):
    b, n_pages = page_table.shape
    past_len = n_pages * PAGE_SIZE
    steps = n_pages // PAGES_PER_STEP
    rows = HEADS_PER_GROUP * SAMPLE_T_PAD
    per_b = lambda a: pl.BlockSpec((1,) + a.shape[1:], lambda i, c, pt: (i, 0, 0))
    const = lambda a: pl.BlockSpec(a.shape, lambda i, c, pt: (0,) * a.ndim)
    hbm = pl.BlockSpec(memory_space=pl.ANY)
    grid_spec = pltpu.PrefetchScalarGridSpec(
        num_scalar_prefetch=1,
        grid=(b, steps),
        in_specs=[hbm, hbm, per_b(q), pl.BlockSpec((1, 2 * SAMPLE_T_PAD, LANE), lambda i, c, pt: (i, 0, c)),
                  const(expand), per_b(ks_new), per_b(vs_new), per_b(win_k), per_b(win_v),
                  per_b(kw_new), per_b(vw_new), per_b(o_cmp), per_b(gate)],
        out_specs=pl.BlockSpec((1, SAMPLE_T_PAD, NSA_WIDTH), lambda i, c, pt: (i, 0, 0)),
        scratch_shapes=[pltpu.VMEM((NSA_KV_HEADS * rows, 1), F32), pltpu.VMEM((NSA_KV_HEADS * rows, 1), F32),
                        pltpu.VMEM((NSA_KV_HEADS * rows, KV_COLS), F32)] + _PAGE_SCRATCH,
    )
    return pl.pallas_call(
        functools.partial(_sample_slc_kernel, n_valid=n_valid, past_len=past_len, layer=layer,
                          n_layers=n_layers),
        grid_spec=grid_spec,
        out_shape=jax.ShapeDtypeStruct((b, SAMPLE_T_PAD, NSA_WIDTH), BF16),
        compiler_params=_cparams("arbitrary", "arbitrary"),
    )(page_table, cache_k, cache_v, q, sel, expand, ks_new, vs_new, win_k, win_v, kw_new, vw_new, o_cmp, gate)


def _lru_kernel(*refs, tt, first_pos_zero, fused_in_proj):
    n_lead = 3 if fused_in_proj else 2
    (cpast_ref, h0_ref, cw_ref, cb_ref, wa_ref, wx_ref, ba_ref, bx_ref, lam_ref,
     o_ref, tail_ref, xtail_ref, xbuf, h_carry, a_s, d_s) = refs[n_lead:]
    j = pl.program_id(1)

    @pl.when(j == 0)
    def _():
        xbuf[...] = cpast_ref[0]
        h_carry[...] = h0_ref[0]

    sub = lax.broadcasted_iota(jnp.int32, (8, LRU_WIDTH), 0)
    if fused_in_proj:
        x_ref, g_ref, win_ref = refs[:n_lead]
        h_in = _rms(x_ref[0], g_ref[...]).astype(BF16)
        x_cur = _dot(h_in, win_ref[:, 0:LRU_WIDTH])
        gy = jax.nn.gelu(_dot(h_in, win_ref[:, LRU_WIDTH:]))
    else:
        x_cur = refs[0][0]
        gy = refs[1][0]
    prev8 = xbuf[...]

    def delayed(k):
        if k == 0:
            return x_cur
        rolled = pltpu.roll(x_cur, k, 0)
        head = jnp.where(sub < k, pltpu.roll(prev8, k, 0), rolled[0:8])
        return jnp.concatenate([head, rolled[8:]], axis=0) if tt > 8 else head

    xc = delayed(CONV_WIDTH - 1) * cw_ref[0:1, :]
    for k in range(1, CONV_WIDTH):
        xc = xc + delayed(CONV_WIDTH - 1 - k) * cw_ref[k:k + 1, :]
    xf = xc + cb_ref[...]
    xf_b = xf.astype(BF16)
    r_parts, i_parts = [], []
    for n in range(LRU_BLOCKS):
        cols = slice(n * LRU_BLOCK, (n + 1) * LRU_BLOCK)
        r_parts.append(_dot(xf_b[:, cols], wa_ref[n]))
        i_parts.append(_dot(xf_b[:, cols], wx_ref[n]))
    sigmoid = lambda v: 0.5 * jnp.tanh(0.5 * v) + 0.5
    r = sigmoid(jnp.concatenate(r_parts, axis=-1) + ba_ref[...])
    gate_i = sigmoid(jnp.concatenate(i_parts, axis=-1) + bx_ref[...])
    neg_lam = -lam_ref[...]
    softplus = jnp.maximum(neg_lam, 0.0) + jnp.log1p(jnp.exp(-jnp.abs(neg_lam)))
    log_a = r * (-LRU_C * softplus)
    a = jnp.exp(log_a)
    gap = -jnp.tanh(log_a) * (a * a + 1.0)
    mult = jnp.where(gap > 0.0, gap * lax.rsqrt(gap), 0.0)
    if first_pos_zero:
        row = lax.broadcasted_iota(jnp.int32, (tt, 1), 0)
        mult = jnp.where((row == 0) & (j == 0), 1.0, mult)
    a_s[...] = a
    d_s[...] = mult * gate_i * xf

    def body(blk, h):
        rows = pl.ds(pl.multiple_of(blk * 8, 8), 8)
        a8 = a_s[rows, :]
        d8 = d_s[rows, :]
        for s in (1, 2, 4):
            keep = sub >= s
            d8 = jnp.where(keep, a8 * pltpu.roll(d8, s, 0) + d8, d8)
            a8 = jnp.where(keep, a8 * pltpu.roll(a8, s, 0), a8)
        hs = a8 * h + d8
        d_s[rows, :] = hs
        return hs[7:8, :]

    n_blk = tt // 8
    h_carry[...] = lax.fori_loop(0, n_blk, body, h_carry[...], unroll=min(4, n_blk))
    hs = d_s[...]
    o_ref[0] = (hs * gy).astype(o_ref.dtype)
    tail_ref[0] = d_s[tt - 8:tt, :]
    xtail_ref[0] = x_cur[tt - 8:tt, :]
    xbuf[...] = x_cur[tt - 8:tt, :]


def lru(lead, conv_past, h0, cw, cb, wa, wx, ba, bx, lam, tt, first_pos_zero, fused_in_proj):
    b, t, _ = lead[0].shape
    w = LRU_WIDTH
    tile = lambda a: pl.BlockSpec((1, tt, a.shape[2]), lambda i, j: (i, j, 0))
    per_b = lambda a: pl.BlockSpec((1,) + a.shape[1:], lambda i, j: (i, 0, 0))
    last8 = pl.BlockSpec((1, 8, w), lambda i, j: (i, 0, 0))
    consts = [cw, cb, wa, wx, ba, bx, lam]
    lead_specs = ([tile(lead[0]), _full(lead[1].shape), _full(lead[2].shape)] if fused_in_proj
                  else [tile(lead[0]), tile(lead[1])])
    return pl.pallas_call(
        functools.partial(_lru_kernel, tt=tt, first_pos_zero=first_pos_zero, fused_in_proj=fused_in_proj),
        grid=(b, t // tt),
        in_specs=lead_specs + [per_b(conv_past), per_b(h0)] + [_full(a.shape) for a in consts],
        out_specs=[pl.BlockSpec((1, tt, w), lambda i, j: (i, j, 0)), last8, last8],
        out_shape=[jax.ShapeDtypeStruct((b, t, w), BF16), jax.ShapeDtypeStruct((b, 8, w), F32),
                   jax.ShapeDtypeStruct((b, 8, w), F32)],
        scratch_shapes=[pltpu.VMEM((8, w), F32), pltpu.VMEM((1, w), F32),
                        pltpu.VMEM((tt, w), F32), pltpu.VMEM((tt, w), F32)],
        compiler_params=_cparams("parallel", "arbitrary"),
    )(*lead, conv_past, h0, *consts)


def _block_mean_matrix(t):
    n_cmp = t // CMP_STRIDE - CMP_BLOCK // CMP_STRIDE + 1
    m = np.zeros((t // CMP_STRIDE, t), np.float32)
    for i in range(n_cmp):
        m[i, i * CMP_STRIDE:i * CMP_STRIDE + CMP_BLOCK] = 1.0 / CMP_BLOCK
    return m


def _sel_from_cmp(n_cmp_rows, n_cmp, n_sel):
    m = np.zeros((n_cmp_rows, n_sel), np.float32)
    for j in range(n_sel):
        for c in range(4 * j - 1, 4 * j + 4):
            if 0 <= c < n_cmp:
                m[c, j] = 1.0
    return m


def _block_diag(w):
    z = jnp.zeros((HEAD_DIM, HEAD_DIM), w.dtype)
    return jnp.concatenate([jnp.concatenate([w[0], z], axis=1), jnp.concatenate([z, w[1]], axis=1)], axis=0)


def _sigmoid(v):
    return jax.nn.sigmoid(v)


def _scale_q(v):
    return v * (HEAD_DIM ** -0.5)


_EVEN_GROUPS = (
    (0, POOL_WIDTH, ((0, POOL_WIDTH, None),)),
    (POOL_WIDTH, NSA_WIDTH, ((0, NSA_WIDTH, _scale_q),)),
    (POOL_WIDTH + NSA_WIDTH, 6 * KV_COLS, tuple((k * KV_COLS, KV_COLS, None) for k in range(6))),
    (POOL_WIDTH + NSA_WIDTH + 6 * KV_COLS, LANE, ((0, LANE, _sigmoid),)),
)
_EVEN_DTYPES = (F32, BF16) + (F32,) * 6 + (F32,)


def _even_in_proj(x2d, g, w_in_pad, tm):
    return norm_matmul(x2d, g, w_in_pad, _EVEN_GROUPS, _EVEN_DTYPES, tm)


def _xattn_block(x2d, b, t, mk, mv, layer, g_pre, g_post, wq, wo, tm, tq):
    q, = norm_matmul(x2d, g_pre, wq, ((0, D_MODEL, ((0, D_MODEL, lambda v: v * (XATTN_HEAD_DIM ** -0.5)),)),),
                     (BF16,), tm)
    o = xattn(q.reshape(b, t, D_MODEL), mk, mv, layer, tq).reshape(b * t, D_MODEL)
    return proj_norm_res([o], [wo], g_post, x2d, tm)


def kernel(x_prompt, mem_prompt, x_sample, state_pool, cache_cmp_k, cache_cmp_v, cache_slc_k, cache_slc_v, cache_win_k, cache_win_v, state_lru_h, state_lru_conv, cache_mem_k, cache_mem_v, page_table, norm_gain, mem_norm_gain, w_in_even, pool_w, pool_scale, w_cmp_k, w_cmp_v, w_out_even, w_in_odd, conv_w, conv_b, lru_wa, lru_ba, lru_wx, lru_bx, lru_lambda, w_out_odd, w_xq, w_xk, w_xv, w_xo, w_ffn_gate, w_ffn_up, w_ffn_down):
    bp, tp, d = x_prompt.shape
    bs, ts, _ = x_sample.shape
    depth = norm_gain.shape[0]
    n_even = w_in_even.shape[0]
    n_pages = page_table.shape[1]
    past_len = n_pages * PAGE_SIZE
    n_phys = cache_cmp_k.shape[0]
    tsp = SAMPLE_T_PAD
    np_tok, ns_tok = bp * tp, bs * tsp
    tm_p, tm_s = 512, ns_tok
    tq, tk = 256, 128
    tt_lru = min(512, tp)

    xp = x_prompt.reshape(np_tok, d)
    xs = jnp.pad(x_sample, ((0, 0), (0, tsp - ts), (0, 0))).reshape(ns_tok, d)

    n_sub_p = tp // CMP_STRIDE
    n_cmp_p = n_sub_p - 1
    n_sel_p = tp // SEL_BLOCK
    pool_mat = jnp.asarray(_block_mean_matrix(tp).T, BF16)
    mselt = jnp.asarray(_sel_from_cmp(n_sub_p, n_cmp_p, n_sel_p).T, BF16)
    ident = jnp.asarray(np.eye(tq, dtype=np.float32), BF16)
    ident_kv = jnp.asarray(np.eye(KV_COLS, dtype=np.float32), BF16)
    subpool = jnp.asarray(np.repeat(np.eye(SUB_PER_STEP, dtype=np.float32), CMP_STRIDE, axis=0), BF16)
    expand_np = np.zeros((LANE, PAGES_PER_STEP * PAGE_SIZE), np.float32)
    expand_np[::SEL_PER_CMP] = np.repeat(np.eye(SEL_PER_STEP, dtype=np.float32), SEL_BLOCK, axis=1)
    expand_s = jnp.asarray(expand_np, BF16)

    row = lambda v: v.reshape(1, -1)
    out_ev_p, out_ev_s, out_lru_p, out_lru_s = [], [], [], []

    mem_k_all, mem_v_all = memory_kv(mem_prompt.reshape(bp * N_MEM, d), mem_norm_gain[:, None, :],
                                     jnp.concatenate([w_xk, w_xv], axis=2).astype(BF16), min(tm_p, bp * N_MEM))

    for li in range(depth):
        gn = norm_gain[li]
        if li % 2 == 0:
            e = li // 2
            w_in = jnp.pad(w_in_even[e], ((0, 0), (0, EVEN_IN_PAD - EVEN_IN))).astype(BF16)
            pw = pool_w[e].astype(BF16)
            ps = row(pool_scale[e])
            wk_bd_t = _block_diag(w_cmp_k[e]).T.astype(BF16)
            wv_bd_t = _block_diag(w_cmp_v[e]).T.astype(BF16)
            w_out = w_out_even[e].astype(BF16)
            kv0 = POOL_WIDTH + NSA_WIDTH
            kcol = lambda k: slice(kv0 + k * KV_COLS, kv0 + (k + 1) * KV_COLS)
            w_u = w_in[:, :POOL_WIDTH]
            w_kpm = jnp.concatenate([w_in[:, kcol(2)], w_in[:, kcol(4)]], axis=1)
            w_t = w_in[:, POOL_WIDTH:kv0 + 7 * KV_COLS].T

            u, kpm, q_t, gate_t, kc, vc, ks, vs, kw, vw = even_in_prompt(xp, row(gn[0]), w_u, w_kpm, w_t,
                                                                         bp, tp, tm_p)
            u3 = u.reshape(bp, tp, POOL_WIDTH)
            a_out = pool_mix(u3, jnp.zeros((bp, POOL_HALO, POOL_WIDTH), F32), pw, ps, 0)
            kcmp, vcmp_t = compress_kv(kc, vc, pool_mat, wk_bd_t, wv_bd_t, ident_kv)
            o_nsa = nsa_prompt(q_t, gate_t, kcmp, vcmp_t, kpm.reshape(bp, tp, 2 * KV_COLS), vs, vw,
                               mselt, ident, tq, tk)
            mix_p = ([a_out, o_nsa], [w_out[:POOL_WIDTH], w_out[POOL_WIDTH:]])
            kv5 = lambda a: a.reshape(bp, NSA_KV_HEADS, HEAD_DIM, a.shape[-1]).transpose(0, 3, 1, 2)
            n_keep = min(WINDOW, tp)
            out_ev_p.append((u3[:, -POOL_STATE:], kv5(kc), kv5(vc), kv5(ks), kv5(vs),
                             kv5(kw[:, :, -n_keep:]), kv5(vw[:, :, -n_keep:])))

            u, q, kc, vc, ks, vs, kw, vw, gate = _even_in_proj(xs, row(gn[0]), w_in, tm_s)
            as3 = lambda a: a.reshape(bs, tsp, a.shape[-1])
            u3 = as3(u)
            past_pool = jnp.pad(state_pool[e], ((0, 0), (POOL_HALO - POOL_STATE, 0), (0, 0)))
            a_out = pool_mix(u3, past_pool, pw, ps, past_len)
            paged = lambda c: c.transpose(0, 1, 3, 4, 2).reshape(n_phys * n_even, KV_COLS, PAGE_SIZE)
            o_cmp, sel_score = sample_cmp(page_table, paged(cache_cmp_k), paged(cache_cmp_v), as3(kc), as3(vc),
                                          as3(q), wk_bd_t, wv_bd_t, subpool, ident_kv, e, n_even, ts)
            sel_rows = bs * NSA_KV_HEADS * tsp
            sel = topk_mask(sel_score.reshape(sel_rows, -1), min(128, sel_rows)).reshape(sel_score.shape)
            n_win = cache_win_k.shape[2]
            win_k = cache_win_k[e].transpose(0, 2, 3, 1).reshape(bs, KV_COLS, n_win)
            win_v = cache_win_v[e].transpose(0, 2, 3, 1).reshape(bs, KV_COLS, n_win)
            o_nsa = sample_slc(page_table, paged(cache_slc_k), paged(cache_slc_v), as3(q), sel, expand_s,
                               as3(ks), as3(vs), win_k, win_v, as3(kw), as3(vw), o_cmp, as3(gate),
                               e, n_even, ts)
            xs = proj_norm_res([a_out.reshape(ns_tok, POOL_WIDTH), o_nsa.reshape(ns_tok, NSA_WIDTH)],
                               [w_out[:POOL_WIDTH], w_out[POOL_WIDTH:]], row(gn[1]), xs, tm_s)
            new4 = lambda a: as3(a)[:, :ts].reshape(bs, ts, NSA_KV_HEADS, HEAD_DIM)
            out_ev_s.append((jnp.concatenate([state_pool[e], u3[:, :ts]], axis=1)[:, -POOL_STATE:],
                             new4(kc), new4(vc), new4(ks), new4(vs),
                             jnp.concatenate([cache_win_k[e], new4(kw)], axis=1)[:, -n_win:],
                             jnp.concatenate([cache_win_v[e], new4(vw)], axis=1)[:, -n_win:]))
        else:
            o = li // 2
            w_in = w_in_odd[o].astype(BF16)
            cw = jnp.pad(conv_w[o], ((0, 8 - CONV_WIDTH), (0, 0)))
            consts = (cw, row(conv_b[o]), lru_wa[o].astype(BF16), lru_wx[o].astype(BF16),
                      row(lru_ba[o]), row(lru_bx[o]), row(lru_lambda[o]))
            w_out = w_out_odd[o].astype(BF16)
            groups = ((0, LRU_WIDTH, ((0, LRU_WIDTH, None),)),
                      (LRU_WIDTH, LRU_WIDTH, ((0, LRU_WIDTH, jax.nn.gelu),)))

            gated, tail, xb_tail = lru((xp.reshape(bp, tp, d), row(gn[0]), w_in),
                                       jnp.zeros((bp, 8, LRU_WIDTH), F32), jnp.zeros((bp, 1, LRU_WIDTH), F32),
                                       *consts, tt=tt_lru, first_pos_zero=True, fused_in_proj=True)
            mix_p = ([gated], [w_out])
            out_lru_p.append((xb_tail[:, -(CONV_WIDTH - 1):], tail[:, 7]))

            xb, yb = norm_matmul(xs, row(gn[0]), w_in, groups, (F32, F32), tm_s)
            xb3, yb3 = xb.reshape(bs, tsp, LRU_WIDTH), yb.reshape(bs, tsp, LRU_WIDTH)
            conv_past = jnp.pad(state_lru_conv[o], ((0, 0), (8 - (CONV_WIDTH - 1), 0), (0, 0)))
            gated, tail, _ = lru((xb3, yb3), conv_past, state_lru_h[o][:, None, :], *consts, tt=tsp,
                                 first_pos_zero=False, fused_in_proj=False)
            xs = proj_norm_res([gated.reshape(ns_tok, LRU_WIDTH)], [w_out], row(gn[1]), xs, tm_s)
            conv_new = jnp.concatenate([state_lru_conv[o], xb3[:, :ts]], axis=1)[:, -(CONV_WIDTH - 1):]
            out_lru_s.append((conv_new, tail[:, ts - 1]))

        wq, wo = w_xq[li].astype(BF16), w_xo[li].astype(BF16)
        mk3, mv3 = mem_k_all[li].reshape(bp, N_MEM, d), mem_v_all[li].reshape(bp, N_MEM, d)
        wg, wu, wd = w_ffn_gate[li].astype(BF16), w_ffn_up[li].astype(BF16), w_ffn_down[li].astype(BF16)
        xp = post_mixer(mix_p[0], mix_p[1], xp.reshape(bp, tp, d), gn, mk3, mv3, wq, wo, wg, wu, wd,
                        tm_p).reshape(np_tok, d)
        xs = _xattn_block(xs, bs, tsp, cache_mem_k, cache_mem_v, li, row(gn[2]), row(gn[3]), wq, wo, tm_s, tsp)
        xs = ffn(xs, row(gn[4]), row(gn[5]), wg, wu, wd, tm_s)

    stack = lambda items, k, axis=0: jnp.stack([s[k] for s in items], axis=axis)
    y_prompt = xp.reshape(bp, tp, d)
    y_sample = xs.reshape(bs, tsp, d)[:, :ts]
    return (y_prompt, y_sample,
            stack(out_ev_p, 0), stack(out_ev_s, 0),
            stack(out_ev_p, 1, 1), stack(out_ev_s, 1, 1),
            stack(out_ev_p, 2, 1), stack(out_ev_s, 2, 1),
            stack(out_ev_p, 3, 1), stack(out_ev_s, 3, 1),
            stack(out_ev_p, 4, 1), stack(out_ev_s, 4, 1),
            stack(out_ev_p, 5), stack(out_ev_s, 5),
            stack(out_ev_p, 6), stack(out_ev_s, 6),
            stack(out_lru_p, 1), stack(out_lru_s, 1),
            stack(out_lru_p, 0), stack(out_lru_s, 0),
            mem_k_all.reshape(depth, bp, N_MEM, XATTN_HEADS, XATTN_HEAD_DIM),
            mem_v_all.reshape(depth, bp, N_MEM, XATTN_HEADS, XATTN_HEAD_DIM))
```
